```python
import math
import jax, jax.numpy as jnp
from jax import lax
import numpy as np

D_MODEL = 4096
BATCH = 2
SEQ = 4096
DEPTH = 2

D_MIX = D_MODEL
D_POOL = D_MIX // 4
D_ATTN = D_MIX // 2
D_SSM = D_MIX - D_POOL - D_ATTN
POOL_WINDOWS = (2, 4, 8, 16)
N_POOL_GROUPS = len(POOL_WINDOWS)
POOL_GROUP = D_POOL // N_POOL_GROUPS
HEAD_DIM = 128
N_HEADS = D_ATTN // HEAD_DIM
Q_BLOCK = 128
SSM_GROUP = 16
N_SSM_GROUPS = D_SSM // SSM_GROUP
SSM_STATE = 64
D_IN = 2 * D_POOL + 4 * D_ATTN + 2 * D_SSM
EPS = 1e-6

kernel_name = "hybrid_pool_stickbreak_s5_parallel"

_SPLIT_SIZES = (D_POOL, D_POOL, D_ATTN, D_ATTN, D_ATTN, D_ATTN, D_SSM, D_SSM)
_SPLIT_OFFSETS = tuple(int(v) for v in np.cumsum(_SPLIT_SIZES)[:-1])


def rmsnorm(x, g):
    xf = x.astype(jnp.float32)
    y = xf * lax.rsqrt(jnp.mean(xf * xf, axis=-1, keepdims=True) + EPS)
    return y * g.astype(jnp.float32)


def pool_mixer(xp, w_pool, pool_scale):
    bsz, L, _ = xp.shape
    xg = xp.astype(jnp.float32).reshape(bsz, L, N_POOL_GROUPS, POOL_GROUP)
    csum = jnp.cumsum(xg, axis=1)
    pos = jnp.arange(1, L + 1, dtype=jnp.float32)[None, :, None]
    outs = []
    for g, w in enumerate(POOL_WINDOWS):
        cg = csum[:, :, g]
        c_lag = jnp.pad(cg, ((0, 0), (w, 0), (0, 0)))[:, :L]
        mean = (cg - c_lag) / jnp.minimum(pos, float(w))
        outs.append(mean - xg[:, :, g])
    pooled = jnp.stack(outs, axis=2)
    mixed = jnp.einsum('blgc,gcd->blgd', pooled, w_pool.astype(jnp.float32))
    return mixed.reshape(bsz, L, D_POOL) * pool_scale.astype(jnp.float32)


def stick_breaking_attention(q, k, v):
    bsz, L = q.shape[:2]
    nb = L // Q_BLOCK
    qf = q.astype(jnp.float32) * (HEAD_DIM ** -0.5)
    kh = k.astype(jnp.float32).transpose(0, 2, 1, 3)
    vh = v.astype(jnp.float32).transpose(0, 2, 1, 3)
    q_blocks = qf.reshape(bsz, nb, Q_BLOCK, N_HEADS, HEAD_DIM).transpose(1, 0, 3, 2, 4)
    k_pos = jnp.arange(L)

    def one_block(args):
        qb, start = args
        z = jnp.einsum('bhqd,bhkd->bhqk', qb, kh)
        q_pos = start + jnp.arange(Q_BLOCK)
        causal = k_pos[None, :] < q_pos[:, None]
        log_1m = jnp.where(causal, jax.nn.log_sigmoid(-z), 0.0)
        suffix = lax.cumsum(log_1m, axis=3, reverse=True) - log_1m
        wts = jnp.where(causal, jnp.exp(jax.nn.log_sigmoid(z) + suffix), 0.0)
        return jnp.einsum('bhqk,bhkd->bhqd', wts, vh)

    starts = jnp.arange(nb, dtype=jnp.int32) * Q_BLOCK
    out = lax.map(one_block, (q_blocks, starts))
    return out.transpose(1, 0, 3, 2, 4).reshape(bsz, L, D_ATTN)


def s5_mixer(u, lam_re, lam_im, log_dt, b_re, b_im, c_re, c_im, d_skip, w_glu, b_glu):
    bsz, L, _ = u.shape
    uf = u.astype(jnp.float32)
    lam = lax.complex(lam_re.astype(jnp.float32), lam_im.astype(jnp.float32))
    dt = jnp.exp(log_dt.astype(jnp.float32))[:, None]
    lam_bar = jnp.exp(lam * dt)
    b_mat = lax.complex(b_re.astype(jnp.float32), b_im.astype(jnp.float32))
    b_bar = ((lam_bar - 1.0) / lam)[..., None] * b_mat
    c_mat = lax.complex(c_re.astype(jnp.float32), c_im.astype(jnp.float32))
    ug = uf.reshape(bsz, L, N_SSM_GROUPS, SSM_GROUP)
    bu = jnp.einsum('blgc,gpc->blgp', ug.astype(jnp.complex64), b_bar)
    a = jnp.broadcast_to(lam_bar, bu.shape)

    def combine(e_prev, e_next):
        a1, x1 = e_prev
        a2, x2 = e_next
        return a2 * a1, a2 * x1 + x2

    _, states = lax.associative_scan(combine, (a, bu), axis=1)
    y = jnp.einsum('blgp,gcp->blgc', states, c_mat).real.reshape(bsz, L, D_SSM)
    y = y + d_skip.astype(jnp.float32) * uf
    h = jax.nn.gelu(y)
    val, gate = jnp.split(h @ w_glu.astype(jnp.float32) + b_glu.astype(jnp.float32), 2, axis=-1)
    return val * jax.nn.sigmoid(gate)


def hybrid_layer(x, ln_g, w_in, w_pool, pool_scale, lam_re, lam_im, log_dt,
                 b_re, b_im, c_re, c_im, d_skip, w_glu, b_glu, branch_g, w_out):
    bsz, L, _ = x.shape
    h = rmsnorm(x, ln_g).astype(x.dtype)
    proj = h @ w_in
    p_x, p_gate, q, k, v, a_gate, s_u, s_gate = jnp.split(proj, _SPLIT_OFFSETS, axis=-1)

    y_pool = pool_mixer(p_x, w_pool, pool_scale)
    shp = (bsz, L, N_HEADS, HEAD_DIM)
    y_attn = stick_breaking_attention(q.reshape(shp), k.reshape(shp), v.reshape(shp))
    y_ssm = s5_mixer(s_u, lam_re, lam_im, log_dt, b_re, b_im, c_re, c_im, d_skip, w_glu, b_glu)

    g_pool, g_attn, g_ssm = jnp.split(branch_g, [D_POOL, D_POOL + D_ATTN])
    y_pool = rmsnorm(y_pool, g_pool) * jax.nn.silu(p_gate.astype(jnp.float32))
    y_attn = rmsnorm(y_attn, g_attn) * jax.nn.silu(a_gate.astype(jnp.float32))
    y_ssm = rmsnorm(y_ssm, g_ssm) * jax.nn.silu(s_gate.astype(jnp.float32))
    y = jnp.concatenate([y_pool, y_attn, y_ssm], axis=-1).astype(x.dtype)
    return x + y @ w_out


def setup_inputs(seed: int = 0) -> dict:
    key = jax.random.key(seed)
    ks = jax.random.split(key, 20)
    f32 = jnp.float32
    nrm = lambda k, shape, s: jax.random.normal(k, shape, f32) * s
    n_idx = jnp.arange(SSM_STATE, dtype=f32)
    return {
        "x": jax.random.normal(ks[0], (BATCH, SEQ, D_MODEL), f32),
        "ln_g": 1.0 + nrm(ks[1], (DEPTH, D_MODEL), 0.02),
        "w_in": nrm(ks[2], (DEPTH, D_MODEL, D_IN), D_MODEL ** -0.5),
        "w_pool": nrm(ks[3], (DEPTH, N_POOL_GROUPS, POOL_GROUP, POOL_GROUP), POOL_GROUP ** -0.5),
        "pool_scale": 1.0 + nrm(ks[4], (DEPTH, D_POOL), 0.1),
        "lam_re": -0.5 + nrm(ks[5], (DEPTH, N_SSM_GROUPS, SSM_STATE), 0.01),
        "lam_im": math.pi * n_idx + nrm(ks[6], (DEPTH, N_SSM_GROUPS, SSM_STATE), 0.01),
        "log_dt": jax.random.uniform(ks[7], (DEPTH, N_SSM_GROUPS), f32,
                                     math.log(1e-3), math.log(1e-1)),
        "b_re": nrm(ks[8], (DEPTH, N_SSM_GROUPS, SSM_STATE, SSM_GROUP), (2 * SSM_GROUP) ** -0.5),
        "b_im": nrm(ks[9], (DEPTH, N_SSM_GROUPS, SSM_STATE, SSM_GROUP), (2 * SSM_GROUP) ** -0.5),
        "c_re": nrm(ks[10], (DEPTH, N_SSM_GROUPS, SSM_GROUP, SSM_STATE), (2 * SSM_STATE) ** -0.5),
        "c_im": nrm(ks[11], (DEPTH, N_SSM_GROUPS, SSM_GROUP, SSM_STATE), (2 * SSM_STATE) ** -0.5),
        "d_skip": nrm(ks[12], (DEPTH, D_SSM), 1.0),
        "w_glu": nrm(ks[13], (DEPTH, D_SSM, 2 * D_SSM), D_SSM ** -0.5),
        "b_glu": nrm(ks[14], (DEPTH, 2 * D_SSM), 0.01),
        "branch_g": 1.0 + nrm(ks[15], (DEPTH, D_MIX), 0.02),
        "w_out": nrm(ks[16], (DEPTH, D_MIX, D_MODEL), (2 * DEPTH * D_MIX) ** -0.5),
        "final_g": 1.0 + nrm(ks[17], (D_MODEL,), 0.02),
    }


def reference(x, ln_g, w_in, w_pool, pool_scale, lam_re, lam_im, log_dt,
              b_re, b_im, c_re, c_im, d_skip, w_glu, b_glu, branch_g, w_out, final_g):
    h = x
    for l in range(DEPTH):
        h = hybrid_layer(h, ln_g[l], w_in[l], w_pool[l], pool_scale[l], lam_re[l], lam_im[l],
                         log_dt[l], b_re[l], b_im[l], c_re[l], c_im[l], d_skip[l],
                         w_glu[l], b_glu[l], branch_g[l], w_out[l])
    return rmsnorm(h, final_g).astype(x.dtype)
```

```python
import functools
import math

import jax
import jax.numpy as jnp
from jax import lax
from jax.experimental import pallas as pl
from jax.experimental.pallas import tpu as pltpu

D_MODEL = 4096
D_POOL = 1024
D_ATTN = 2048
D_SSM = 1024
POOL_WINDOWS = (2, 4, 8, 16)
POOL_GROUP = 256
POOL_HALO = 16
HEAD_DIM = 128
N_HEADS = 16
SSM_GROUP = 16
SSM_STATE = 64
N_SSM_GROUPS = 64
D_IN = 12288
EPS = 1e-6

COL_Q_128 = 2048 // 128
COL_K_128 = 4096 // 128
COL_V_128 = 6144 // 128
COL_PX_1024 = 0
COL_PGATE_1024 = 1
COL_AGATE_2048 = 4
COL_SU_1024 = 10
COL_SGATE_1024 = 11

V7X_VMEM_BYTES = 64 * 1024 * 1024
SUBLANES = 8
LANES = 128

F32_EXP_ZERO_BELOW = -104.0

BF16 = jnp.bfloat16
F32 = jnp.float32


def _vmem_limit(nbytes):
    return int(min(nbytes * 3 // 2 + (4 << 20), V7X_VMEM_BYTES - (6 << 20)))


def _silu(x):
    return x * jax.nn.sigmoid(x)


def _rmsnorm_kernel(x_ref, g_ref, o_ref):
    x = x_ref[...]
    y = x * lax.rsqrt(jnp.mean(x * x, axis=-1, keepdims=True) + EPS)
    o_ref[...] = (y * g_ref[...]).astype(o_ref.dtype)


def _rmsnorm(x2d, g, out_dtype, tm=512):
    m, d = x2d.shape
    return pl.pallas_call(
        _rmsnorm_kernel,
        grid=(m // tm,),
        in_specs=[pl.BlockSpec((tm, d), lambda i: (i, 0)),
                  pl.BlockSpec((1, d), lambda i: (0, 0))],
        out_specs=pl.BlockSpec((tm, d), lambda i: (i, 0)),
        out_shape=jax.ShapeDtypeStruct((m, d), out_dtype),
        compiler_params=pltpu.CompilerParams(
            dimension_semantics=("parallel",),
            vmem_limit_bytes=_vmem_limit(2 * tm * d * 8)),
        name="rmsnorm",
    )(x2d, g.reshape(1, d))


def _matmul_kernel(a_ref, b_ref, o_ref):
    o_ref[...] = jnp.dot(a_ref[...], b_ref[...], preferred_element_type=F32).astype(o_ref.dtype)


def _in_proj(h, w, tm=1024, tn=512):
    m, k = h.shape
    n = w.shape[1]
    est = 2 * (tm * k * 2 + k * tn * 2 + tm * tn * 4)
    return pl.pallas_call(
        _matmul_kernel,
        grid=(m // tm, n // tn),
        in_specs=[pl.BlockSpec((tm, k), lambda i, j: (i, 0)),
                  pl.BlockSpec((k, tn), lambda i, j: (0, j))],
        out_specs=pl.BlockSpec((tm, tn), lambda i, j: (i, j)),
        out_shape=jax.ShapeDtypeStruct((m, n), F32),
        compiler_params=pltpu.CompilerParams(
            dimension_semantics=("parallel", "arbitrary"),
            vmem_limit_bytes=_vmem_limit(est)),
        name="in_proj",
    )(h, w)


def _out_proj_kernel(yp_ref, ya_ref, ys_ref, w_ref, x_ref, o_ref):
    acc = jnp.dot(yp_ref[...], w_ref[0:D_POOL, :], preferred_element_type=F32)
    acc += jnp.dot(ya_ref[...], w_ref[D_POOL:D_POOL + D_ATTN, :], preferred_element_type=F32)
    acc += jnp.dot(ys_ref[...], w_ref[D_POOL + D_ATTN:, :], preferred_element_type=F32)
    o_ref[...] = x_ref[...] + acc


def _out_proj(y_pool, y_attn, y_ssm, w, x2d, tm=1024, tn=512):
    m = x2d.shape[0]
    k, n = w.shape
    est = 2 * (tm * k * 2 + k * tn * 2 + 2 * tm * tn * 4)
    return pl.pallas_call(
        _out_proj_kernel,
        grid=(m // tm, n // tn),
        in_specs=[pl.BlockSpec((tm, D_POOL), lambda i, j: (i, 0)),
                  pl.BlockSpec((tm, D_ATTN), lambda i, j: (i, 0)),
                  pl.BlockSpec((tm, D_SSM), lambda i, j: (i, 0)),
                  pl.BlockSpec((k, tn), lambda i, j: (0, j)),
                  pl.BlockSpec((tm, tn), lambda i, j: (i, j))],
        out_specs=pl.BlockSpec((tm, tn), lambda i, j: (i, j)),
        out_shape=jax.ShapeDtypeStruct((m, n), F32),
        compiler_params=pltpu.CompilerParams(
            dimension_semantics=("parallel", "arbitrary"),
            vmem_limit_bytes=_vmem_limit(est)),
        name="out_proj",
    )(y_pool, y_attn, y_ssm, w, x2d)


def _pool_kernel(px_ref, halo_ref, gate_ref, w_ref, scale_ref, g_ref, o_ref, xs_ref, *, t_blk):
    i = pl.program_id(1)
    x = px_ref[0]
    xs_ref[0:POOL_HALO, :] = jnp.where(i > 0, halo_ref[0], 0.0)
    xs_ref[POOL_HALO:POOL_HALO + t_blk, :] = x
    pos = (i * t_blk + 1 + lax.broadcasted_iota(jnp.int32, (t_blk, 1), 0)).astype(F32)
    mixed = []
    for g, w in enumerate(POOL_WINDOWS):
        cols = slice(g * POOL_GROUP, (g + 1) * POOL_GROUP)
        s = x[:, cols]
        for j in range(1, w):
            s = s + xs_ref[POOL_HALO - j:POOL_HALO - j + t_blk, cols]
        pooled = s / jnp.minimum(pos, float(w)) - x[:, cols]
        mixed.append(jnp.dot(pooled.astype(BF16), w_ref[g], preferred_element_type=F32))
    y = jnp.concatenate(mixed, axis=1) * scale_ref[...]
    y = y * lax.rsqrt(jnp.mean(y * y, axis=-1, keepdims=True) + EPS) * g_ref[...]
    o_ref[0] = (y * _silu(gate_ref[0])).astype(o_ref.dtype)


def _pool_branch(proj3, w_pool, pool_scale, g_pool, t_blk=512):
    b, l, _ = proj3.shape
    halo_per_blk = t_blk // POOL_HALO
    kern = functools.partial(_pool_kernel, t_blk=t_blk)
    return pl.pallas_call(
        kern,
        grid=(b, l // t_blk),
        in_specs=[
            pl.BlockSpec((1, t_blk, D_POOL), lambda bi, i: (bi, i, COL_PX_1024)),
            pl.BlockSpec((1, POOL_HALO, D_POOL),
                         lambda bi, i: (bi, jnp.maximum(i * halo_per_blk - 1, 0), COL_PX_1024)),
            pl.BlockSpec((1, t_blk, D_POOL), lambda bi, i: (bi, i, COL_PGATE_1024)),
            pl.BlockSpec((len(POOL_WINDOWS), POOL_GROUP, POOL_GROUP), lambda bi, i: (0, 0, 0)),
            pl.BlockSpec((1, D_POOL), lambda bi, i: (0, 0)),
            pl.BlockSpec((1, D_POOL), lambda bi, i: (0, 0)),
        ],
        out_specs=pl.BlockSpec((1, t_blk, D_POOL), lambda bi, i: (bi, i, 0)),
        out_shape=jax.ShapeDtypeStruct((b, l, D_POOL), BF16),
        scratch_shapes=[pltpu.VMEM((POOL_HALO + t_blk, D_POOL), F32)],
        compiler_params=pltpu.CompilerParams(
            dimension_semantics=("parallel", "parallel"),
            vmem_limit_bytes=_vmem_limit(8 * t_blk * D_POOL * 4)),
        name="pool_branch",
    )(proj3, proj3, proj3, w_pool, pool_scale.reshape(1, D_POOL), g_pool.reshape(1, D_POOL))


def _attn_kernel(q_ref, k_ref, v_ref, o_ref, *, t_blk):
    i = pl.program_id(2)
    q = (q_ref[0] * (HEAD_DIM ** -0.5)).astype(BF16)
    row = lax.broadcasted_iota(jnp.int32, (t_blk, t_blk), 0)
    col = lax.broadcasted_iota(jnp.int32, (t_blk, t_blk), 1)
    later = jnp.where(row > col, 1.0, 0.0).astype(BF16)
    causal = col < row

    def tile(j, carry, diagonal):
        start = pl.multiple_of(j * t_blk, t_blk)
        kj = k_ref[0, pl.ds(start, t_blk), :].astype(BF16)
        vj = v_ref[0, pl.ds(start, t_blk), :].astype(BF16)
        z = lax.dot_general(q, kj, (((1,), (1,)), ((), ())), preferred_element_type=F32)
        softplus = jnp.maximum(z, 0.0) + jnp.log1p(jnp.exp(-jnp.abs(z)))
        log_1m = -softplus
        if diagonal:
            log_1m = jnp.where(causal, log_1m, 0.0)
        hi = log_1m.astype(BF16)
        lo = (log_1m - hi.astype(F32)).astype(BF16)
        suffix = (jnp.dot(hi, later, preferred_element_type=F32)
                  + jnp.dot(lo, later, preferred_element_type=F32))
        wts = jnp.exp((z - softplus) + suffix + carry)
        if diagonal:
            wts = jnp.where(causal, wts, 0.0)
        pv = jnp.dot(wts.astype(BF16), vj, preferred_element_type=F32)
        return pv, carry + jnp.sum(log_1m, axis=1, keepdims=True)

    acc, carry = tile(i, jnp.zeros((t_blk, 1), F32), True)

    def cond(state):
        j, _, _, alive = state
        return jnp.logical_and(j >= 0, alive)

    def body(state):
        j, acc, carry, _ = state
        pv, carry = tile(j, carry, False)
        return j - 1, acc + pv, carry, jnp.max(carry) > F32_EXP_ZERO_BELOW

    _, acc, _, _ = lax.while_loop(cond, body, (i - 1, acc, carry, jnp.max(carry) > F32_EXP_ZERO_BELOW))
    o_ref[0] = acc


def _attention(proj3, t_blk=256):
    b, l, _ = proj3.shape
    kern = functools.partial(_attn_kernel, t_blk=t_blk)
    est = 2 * (2 * l * HEAD_DIM * 4 + 2 * t_blk * HEAD_DIM * 4) + 16 * t_blk * t_blk * 4
    return pl.pallas_call(
        kern,
        grid=(b, N_HEADS, l // t_blk),
        in_specs=[
            pl.BlockSpec((1, t_blk, HEAD_DIM), lambda bi, h, i: (bi, i, COL_Q_128 + h)),
            pl.BlockSpec((1, l, HEAD_DIM), lambda bi, h, i: (bi, 0, COL_K_128 + h)),
            pl.BlockSpec((1, l, HEAD_DIM), lambda bi, h, i: (bi, 0, COL_V_128 + h)),
        ],
        out_specs=pl.BlockSpec((1, t_blk, HEAD_DIM), lambda bi, h, i: (bi, i, h)),
        out_shape=jax.ShapeDtypeStruct((b, l, D_ATTN), F32),
        compiler_params=pltpu.CompilerParams(
            dimension_semantics=("parallel", "parallel", "arbitrary"),
            vmem_limit_bytes=_vmem_limit(est)),
        name="stickbreak_attn",
    )(proj3, proj3, proj3)


def _attn_finalize_kernel(y_ref, gate_ref, g_ref, o_ref):
    y = y_ref[...]
    y = y * lax.rsqrt(jnp.mean(y * y, axis=-1, keepdims=True) + EPS) * g_ref[...]
    o_ref[...] = (y * _silu(gate_ref[...])).astype(o_ref.dtype)


def _attn_finalize(y_attn2d, proj2d, g_attn, tm=512):
    m = y_attn2d.shape[0]
    return pl.pallas_call(
        _attn_finalize_kernel,
        grid=(m // tm,),
        in_specs=[pl.BlockSpec((tm, D_ATTN), lambda i: (i, 0)),
                  pl.BlockSpec((tm, D_ATTN), lambda i: (i, COL_AGATE_2048)),
                  pl.BlockSpec((1, D_ATTN), lambda i: (0, 0))],
        out_specs=pl.BlockSpec((tm, D_ATTN), lambda i: (i, 0)),
        out_shape=jax.ShapeDtypeStruct((m, D_ATTN), BF16),
        compiler_params=pltpu.CompilerParams(
            dimension_semantics=("parallel",),
            vmem_limit_bytes=_vmem_limit(6 * tm * D_ATTN * 4)),
        name="attn_finalize",
    )(y_attn2d, proj2d, g_attn.reshape(1, D_ATTN))


N_SLAB = 4
SLAB_IN = D_SSM // N_SLAB
SLAB_STATES = 1024
GROUPS_PER_SLAB = N_SSM_GROUPS // N_SLAB
N_CHUNK = SLAB_STATES // LANES


def _discretize_kernel(lam_re_ref, lam_im_ref, log_dt_ref, b_re_ref, b_im_ref,
                       a_re_ref, a_im_ref, wb_re_ref, wb_im_ref):
    lam_re = lam_re_ref[...]
    lam_im = lam_im_ref[...]
    dt = jnp.exp(log_dt_ref[...])
    mag = jnp.exp(lam_re * dt)
    a_re = mag * jnp.cos(lam_im * dt)
    a_im = mag * jnp.sin(lam_im * dt)
    a_re_ref[...] = a_re
    a_im_ref[...] = a_im
    num_re = a_re - 1.0
    den = lam_re * lam_re + lam_im * lam_im
    c_re = ((num_re * lam_re + a_im * lam_im) / den)[:, None, :]
    c_im = ((a_im * lam_re - num_re * lam_im) / den)[:, None, :]
    b_re = b_re_ref[...]
    b_im = b_im_ref[...]
    wb_re_ref[...] = (c_re * b_re - c_im * b_im).astype(wb_re_ref.dtype)
    wb_im_ref[...] = (c_re * b_im + c_im * b_re).astype(wb_im_ref.dtype)


def _block_diag_in(b):
    eye = jnp.eye(GROUPS_PER_SLAB, dtype=b.dtype)
    b4 = b.reshape(N_SLAB, GROUPS_PER_SLAB, SSM_STATE, SSM_GROUP)
    return jnp.einsum('jgpc,gh->jgchp', b4, eye).reshape(N_SLAB, SLAB_IN, SLAB_STATES)


def _block_diag_out(c):
    eye = jnp.eye(GROUPS_PER_SLAB, dtype=c.dtype)
    c4 = c.reshape(N_SLAB, GROUPS_PER_SLAB, SSM_GROUP, SSM_STATE)
    return jnp.einsum('jgcp,gh->jgphc', c4, eye).reshape(N_SLAB, SLAB_STATES, SLAB_IN)


def _discretize(lam_re, lam_im, log_dt, b_re, b_im):
    shape4 = (N_SLAB, SLAB_STATES)
    log_dt_full = jnp.broadcast_to(log_dt[:, None], (N_SSM_GROUPS, SSM_STATE)).reshape(shape4)
    out_shape = (jax.ShapeDtypeStruct(shape4, F32), jax.ShapeDtypeStruct(shape4, F32),
                 jax.ShapeDtypeStruct((N_SLAB, SLAB_IN, SLAB_STATES), BF16),
                 jax.ShapeDtypeStruct((N_SLAB, SLAB_IN, SLAB_STATES), BF16))
    return pl.pallas_call(
        _discretize_kernel, out_shape=out_shape, name="s5_discretize",
        compiler_params=pltpu.CompilerParams(vmem_limit_bytes=_vmem_limit(32 << 20)),
    )(lam_re.reshape(shape4), lam_im.reshape(shape4), log_dt_full,
      _block_diag_in(b_re), _block_diag_in(b_im))


def _s5_kernel(u_ref, gate_ref, a_re_ref, a_im_ref, wb_re_ref, wb_im_ref, wc_re_ref, wc_im_ref,
               dskip_ref, wglu_ref, bglu_ref, g_ref, o_ref,
               s_re_ref, s_im_ref, x_re_ref, x_im_ref, *, t_blk, n_batch):
    n_row = n_batch * N_SLAB
    assert n_row == SUBLANES

    @pl.when(pl.program_id(0) == 0)
    def _():
        x_re_ref[...] = jnp.zeros_like(x_re_ref)
        x_im_ref[...] = jnp.zeros_like(x_im_ref)

    for b in range(n_batch):
        for j in range(N_SLAB):
            uj = u_ref[b, :, j * SLAB_IN:(j + 1) * SLAB_IN].astype(BF16)
            bu_re = jnp.dot(uj, wb_re_ref[j], preferred_element_type=F32)
            bu_im = jnp.dot(uj, wb_im_ref[j], preferred_element_type=F32)
            k = b * N_SLAB + j
            for c in range(N_CHUNK):
                lanes = slice(c * LANES, (c + 1) * LANES)
                s_re_ref[c, pl.ds(k, t_blk, stride=n_row), :] = bu_re[:, lanes]
                s_im_ref[c, pl.ds(k, t_blk, stride=n_row), :] = bu_im[:, lanes]

    a_re = [a_re_ref[:, c * LANES:(c + 1) * LANES] for c in range(N_CHUNK)]
    a_im = [a_im_ref[:, c * LANES:(c + 1) * LANES] for c in range(N_CHUNK)]
    x_re0 = tuple(x_re_ref[:, c * LANES:(c + 1) * LANES] for c in range(N_CHUNK))
    x_im0 = tuple(x_im_ref[:, c * LANES:(c + 1) * LANES] for c in range(N_CHUNK))

    def step(t, state):
        x_re, x_im = state
        rows = pl.ds(pl.multiple_of(t * n_row, n_row), n_row)
        new_re, new_im = [], []
        for c in range(N_CHUNK):
            nr = a_re[c] * x_re[c] - a_im[c] * x_im[c] + s_re_ref[c, rows, :]
            ni = a_re[c] * x_im[c] + a_im[c] * x_re[c] + s_im_ref[c, rows, :]
            s_re_ref[c, rows, :] = nr
            s_im_ref[c, rows, :] = ni
            new_re.append(nr)
            new_im.append(ni)
        return tuple(new_re), tuple(new_im)

    x_re, x_im = lax.fori_loop(0, t_blk, step, (x_re0, x_im0), unroll=4)
    for c in range(N_CHUNK):
        x_re_ref[:, c * LANES:(c + 1) * LANES] = x_re[c]
        x_im_ref[:, c * LANES:(c + 1) * LANES] = x_im[c]

    for b in range(n_batch):
        ys = []
        for j in range(N_SLAB):
            k = b * N_SLAB + j
            xr = jnp.concatenate([s_re_ref[c, pl.ds(k, t_blk, stride=n_row), :] for c in range(N_CHUNK)], axis=1)
            xi = jnp.concatenate([s_im_ref[c, pl.ds(k, t_blk, stride=n_row), :] for c in range(N_CHUNK)], axis=1)
            ys.append(jnp.dot(xr.astype(BF16), wc_re_ref[j], preferred_element_type=F32)
                      - jnp.dot(xi.astype(BF16), wc_im_ref[j], preferred_element_type=F32))
        y = jnp.concatenate(ys, axis=1) + dskip_ref[...] * u_ref[b]
        h = jax.nn.gelu(y)
        glu = jnp.dot(h.astype(BF16), wglu_ref[...], preferred_element_type=F32) + bglu_ref[...]
        out = glu[:, :D_SSM] * jax.nn.sigmoid(glu[:, D_SSM:])
        out = out * lax.rsqrt(jnp.mean(out * out, axis=-1, keepdims=True) + EPS) * g_ref[...]
        o_ref[b] = (out * _silu(gate_ref[b])).astype(o_ref.dtype)


def _s5_branch(proj3, a_re, a_im, wb_re, wb_im, wc_re, wc_im, d_skip, w_glu, b_glu, g_ssm, t_blk=128):
    b, l, _ = proj3.shape
    n_row = b * N_SLAB
    a_re_t = jnp.tile(a_re, (b, 1))
    a_im_t = jnp.tile(a_im, (b, 1))
    kern = functools.partial(_s5_kernel, t_blk=t_blk, n_batch=b)
    const = lambda *shape: pl.BlockSpec(shape, lambda i: (0,) * len(shape))
    weights = 2 * (4 * N_SLAB * SLAB_IN * SLAB_STATES * 2 + D_SSM * 2 * D_SSM * 2)
    est = weights + 2 * N_CHUNK * t_blk * n_row * LANES * 4 + 2 * (2 * b * t_blk * D_SSM * 4 + b * t_blk * D_SSM * 2)
    return pl.pallas_call(
        kern,
        grid=(l // t_blk,),
        in_specs=[
            pl.BlockSpec((b, t_blk, D_SSM), lambda i: (0, i, COL_SU_1024)),
            pl.BlockSpec((b, t_blk, D_SSM), lambda i: (0, i, COL_SGATE_1024)),
            const(n_row, SLAB_STATES), const(n_row, SLAB_STATES),
            const(N_SLAB, SLAB_IN, SLAB_STATES), const(N_SLAB, SLAB_IN, SLAB_STATES),
            const(N_SLAB, SLAB_STATES, SLAB_IN), const(N_SLAB, SLAB_STATES, SLAB_IN),
            const(1, D_SSM), const(D_SSM, 2 * D_SSM), const(1, 2 * D_SSM), const(1, D_SSM),
        ],
        out_specs=pl.BlockSpec((b, t_blk, D_SSM), lambda i: (0, i, 0)),
        out_shape=jax.ShapeDtypeStruct((b, l, D_SSM), BF16),
        scratch_shapes=[pltpu.VMEM((N_CHUNK, t_blk * n_row, LANES), F32),
                        pltpu.VMEM((N_CHUNK, t_blk * n_row, LANES), F32),
                        pltpu.VMEM((n_row, SLAB_STATES), F32),
                        pltpu.VMEM((n_row, SLAB_STATES), F32)],
        compiler_params=pltpu.CompilerParams(
            dimension_semantics=("arbitrary",),
            vmem_limit_bytes=_vmem_limit(est)),
        name="s5_branch",
    )(proj3, proj3, a_re_t, a_im_t, wb_re, wb_im, wc_re, wc_im,
      d_skip.reshape(1, D_SSM), w_glu, b_glu.reshape(1, 2 * D_SSM), g_ssm.reshape(1, D_SSM))


def _layer(x2d, bsz, ln_g, w_in, w_pool, pool_scale, lam_re, lam_im, log_dt,
           b_re, b_im, c_re, c_im, d_skip, w_glu, b_glu, branch_g, w_out):
    m = x2d.shape[0]
    l = m // bsz
    h = _rmsnorm(x2d, ln_g, BF16)
    proj = _in_proj(h, w_in.astype(BF16))
    proj3 = proj.reshape(bsz, l, D_IN)

    g_pool = branch_g[:D_POOL]
    g_attn = branch_g[D_POOL:D_POOL + D_ATTN]
    g_ssm = branch_g[D_POOL + D_ATTN:]

    y_pool = _pool_branch(proj3, w_pool.astype(BF16), pool_scale, g_pool)
    y_attn = _attention(proj3)
    y_attn = _attn_finalize(y_attn.reshape(m, D_ATTN), proj, g_attn)
    a_re, a_im, wb_re, wb_im = _discretize(lam_re, lam_im, log_dt, b_re, b_im)
    y_ssm = _s5_branch(proj3, a_re, a_im, wb_re, wb_im,
                       _block_diag_out(c_re).astype(BF16), _block_diag_out(c_im).astype(BF16),
                       d_skip, w_glu.astype(BF16), b_glu, g_ssm)
    return _out_proj(y_pool.reshape(m, D_POOL), y_attn, y_ssm.reshape(m, D_SSM),
                     w_out.astype(BF16), x2d)


def kernel(x, ln_g, w_in, w_pool, pool_scale, lam_re, lam_im, log_dt, b_re, b_im, c_re, c_im,
           d_skip, w_glu, b_glu, branch_g, w_out, final_g):
    bsz, l, d = x.shape
    h = x.reshape(bsz * l, d)
    for i in range(ln_g.shape[0]):
        h = _layer(h, bsz, ln_g[i], w_in[i], w_pool[i], pool_scale[i], lam_re[i], lam_im[i],
                   log_dt[i], b_re[i], b_im[i], c_re[i], c_im[i], d_skip[i],
                   w_glu[i], b_glu[i], branch_g[i], w_out[i])
    return _rmsnorm(h, final_g, x.dtype).reshape(bsz, l, d)
```

```python
import functools

import jax
import jax.numpy as jnp
from jax import lax
from jax.experimental import pallas as pl
from jax.experimental.pallas import tpu as pltpu

D_MODEL = 4096
D_POOL = 1024
D_ATTN = 2048
D_SSM = 1024
POOL_WINDOWS = (2, 4, 8, 16)
POOL_GROUP = 256
POOL_HALO = 16
HEAD_DIM = 128
N_HEADS = 16
SSM_GROUP = 16
SSM_STATE = 64
N_SSM_GROUPS = 64
D_IN = 12288
EPS = 1e-6

COL_Q_128 = 2048 // 128
COL_K_128 = 4096 // 128
COL_V_128 = 6144 // 128
COL_PX_1024 = 0
COL_PGATE_1024 = 1
COL_AGATE_2048 = 4
COL_SU_1024 = 10
COL_SGATE_1024 = 11

V7X_VMEM_BYTES = 64 * 1024 * 1024
SUBLANES = 8
LANES = 128

F32_EXP_ZERO_ABOVE = 104.0
NO_BLOCK_CARRY = 1e30

BF16 = jnp.bfloat16
F32 = jnp.float32


def _vmem_limit(nbytes):
    return int(min(nbytes * 3 // 2 + (4 << 20), V7X_VMEM_BYTES - (6 << 20)))


def _silu(x):
    return x * jax.nn.sigmoid(x)


def _resident(*shape):
    return pl.BlockSpec(shape, lambda *_: (0,) * len(shape), pipeline_mode=pl.Buffered(1))


def _rmsnorm_kernel(x_ref, g_ref, o_ref):
    x = x_ref[...]
    y = x * lax.rsqrt(jnp.mean(x * x, axis=-1, keepdims=True) + EPS)
    o_ref[...] = (y * g_ref[...]).astype(o_ref.dtype)


def _rmsnorm(x2d, g, out_dtype, tm=512):
    m, d = x2d.shape
    return pl.pallas_call(
        _rmsnorm_kernel,
        grid=(m // tm,),
        in_specs=[pl.BlockSpec((tm, d), lambda i: (i, 0)),
                  pl.BlockSpec((1, d), lambda i: (0, 0))],
        out_specs=pl.BlockSpec((tm, d), lambda i: (i, 0)),
        out_shape=jax.ShapeDtypeStruct((m, d), out_dtype),
        compiler_params=pltpu.CompilerParams(
            dimension_semantics=("parallel",),
            vmem_limit_bytes=_vmem_limit(2 * tm * d * 8)),
        name="rmsnorm",
    )(x2d, g.reshape(1, d))


def _matmul_kernel(a_ref, b_ref, o_ref):
    o_ref[...] = jnp.dot(a_ref[...], b_ref[...], preferred_element_type=F32).astype(o_ref.dtype)


def _in_proj(h, w, tm=1024, tn=1024):
    m, k = h.shape
    n = w.shape[1]
    est = 2 * (tm * k * 2 + k * tn * 2 + tm * tn * 2)
    return pl.pallas_call(
        _matmul_kernel,
        grid=(m // tm, n // tn),
        in_specs=[pl.BlockSpec((tm, k), lambda i, j: (i, 0)),
                  pl.BlockSpec((k, tn), lambda i, j: (0, j))],
        out_specs=pl.BlockSpec((tm, tn), lambda i, j: (i, j)),
        out_shape=jax.ShapeDtypeStruct((m, n), BF16),
        compiler_params=pltpu.CompilerParams(
            dimension_semantics=("parallel", "arbitrary"),
            vmem_limit_bytes=_vmem_limit(est)),
        name="in_proj",
    )(h, w)


def _out_proj_kernel(yp_ref, ya_ref, ys_ref, w_ref, x_ref, g_ref, *refs, tn, last):
    normed_ref = refs[0] if last else refs[1]
    row_ref = refs[1] if last else refs[0]
    j = pl.program_id(1)
    acc = jnp.dot(yp_ref[...], w_ref[0:D_POOL, :], preferred_element_type=F32)
    acc += jnp.dot(ya_ref[...], w_ref[D_POOL:D_POOL + D_ATTN, :], preferred_element_type=F32)
    acc += jnp.dot(ys_ref[...], w_ref[D_POOL + D_ATTN:, :], preferred_element_type=F32)
    row_ref[:, pl.ds(pl.multiple_of(j * tn, tn), tn)] = x_ref[...] + acc

    @pl.when(j == pl.num_programs(1) - 1)
    def _():
        x = row_ref[...]
        y = x * lax.rsqrt(jnp.mean(x * x, axis=-1, keepdims=True) + EPS)
        normed_ref[...] = (y * g_ref[...]).astype(normed_ref.dtype)


def _out_proj(y_pool, y_attn, y_ssm, w, x2d, next_g, last, tm=512, tn=512):
    m = x2d.shape[0]
    k, n = w.shape
    row_spec = pl.BlockSpec((tm, n), lambda i, j: (i, 0))
    if last:
        out_specs = row_spec
        out_shape = jax.ShapeDtypeStruct((m, n), F32)
        scratch = [pltpu.VMEM((tm, n), F32)]
        est = 3 * tm * n * 4
    else:
        out_specs = (row_spec, row_spec)
        out_shape = (jax.ShapeDtypeStruct((m, n), F32), jax.ShapeDtypeStruct((m, n), BF16))
        scratch = []
        est = 2 * tm * n * (4 + 2)
    est += 2 * (tm * k * 2 + k * tn * 2 + tm * tn * 4)
    return pl.pallas_call(
        functools.partial(_out_proj_kernel, tn=tn, last=last),
        grid=(m // tm, n // tn),
        in_specs=[pl.BlockSpec((tm, D_POOL), lambda i, j: (i, 0)),
                  pl.BlockSpec((tm, D_ATTN), lambda i, j: (i, 0)),
                  pl.BlockSpec((tm, D_SSM), lambda i, j: (i, 0)),
                  pl.BlockSpec((k, tn), lambda i, j: (0, j)),
                  pl.BlockSpec((tm, tn), lambda i, j: (i, j)),
                  pl.BlockSpec((1, n), lambda i, j: (0, 0))],
        out_specs=out_specs,
        out_shape=out_shape,
        scratch_shapes=scratch,
        compiler_params=pltpu.CompilerParams(
            dimension_semantics=("parallel", "arbitrary"),
            vmem_limit_bytes=_vmem_limit(est)),
        name="out_proj",
    )(y_pool, y_attn, y_ssm, w, x2d, next_g.reshape(1, n))


def _pool_kernel(px_ref, halo_ref, gate_ref, w_ref, scale_ref, g_ref, o_ref, xs_ref, *, t_blk):
    i = pl.program_id(1)
    x = px_ref[0].astype(F32)
    xs_ref[0:POOL_HALO, :] = jnp.where(i > 0, halo_ref[0].astype(F32), 0.0)
    xs_ref[POOL_HALO:POOL_HALO + t_blk, :] = x
    pos = (i * t_blk + 1 + lax.broadcasted_iota(jnp.int32, (t_blk, 1), 0)).astype(F32)
    mixed = []
    for g, w in enumerate(POOL_WINDOWS):
        cols = slice(g * POOL_GROUP, (g + 1) * POOL_GROUP)
        s = x[:, cols]
        for j in range(1, w):
            s = s + xs_ref[POOL_HALO - j:POOL_HALO - j + t_blk, cols]
        pooled = s / jnp.minimum(pos, float(w)) - x[:, cols]
        mixed.append(jnp.dot(pooled.astype(BF16), w_ref[g], preferred_element_type=F32))
    y = jnp.concatenate(mixed, axis=1) * scale_ref[...]
    y = y * lax.rsqrt(jnp.mean(y * y, axis=-1, keepdims=True) + EPS) * g_ref[...]
    o_ref[0] = (y * _silu(gate_ref[0].astype(F32))).astype(o_ref.dtype)


def _pool_branch(proj3, w_pool, pool_scale, g_pool, t_blk=512):
    b, l, _ = proj3.shape
    halo_per_blk = t_blk // POOL_HALO
    kern = functools.partial(_pool_kernel, t_blk=t_blk)
    return pl.pallas_call(
        kern,
        grid=(b, l // t_blk),
        in_specs=[
            pl.BlockSpec((1, t_blk, D_POOL), lambda bi, i: (bi, i, COL_PX_1024)),
            pl.BlockSpec((1, POOL_HALO, D_POOL),
                         lambda bi, i: (bi, jnp.maximum(i * halo_per_blk - 1, 0), COL_PX_1024)),
            pl.BlockSpec((1, t_blk, D_POOL), lambda bi, i: (bi, i, COL_PGATE_1024)),
            pl.BlockSpec((len(POOL_WINDOWS), POOL_GROUP, POOL_GROUP), lambda bi, i: (0, 0, 0)),
            pl.BlockSpec((1, D_POOL), lambda bi, i: (0, 0)),
            pl.BlockSpec((1, D_POOL), lambda bi, i: (0, 0)),
        ],
        out_specs=pl.BlockSpec((1, t_blk, D_POOL), lambda bi, i: (bi, i, 0)),
        out_shape=jax.ShapeDtypeStruct((b, l, D_POOL), BF16),
        scratch_shapes=[pltpu.VMEM((POOL_HALO + t_blk, D_POOL), F32)],
        compiler_params=pltpu.CompilerParams(
            dimension_semantics=("parallel", "parallel"),
            vmem_limit_bytes=_vmem_limit(8 * t_blk * D_POOL * 4)),
        name="pool_branch",
    )(proj3, proj3, proj3, w_pool, pool_scale.reshape(1, D_POOL), g_pool.reshape(1, D_POOL))


def _attn_kernel(q_ref, k_ref, v_ref, o_ref, *, t_blk, n_head):
    i = pl.program_id(2)
    row = lax.broadcasted_iota(jnp.int32, (t_blk, t_blk), 0)
    col = lax.broadcasted_iota(jnp.int32, (t_blk, t_blk), 1)
    later = jnp.where(row > col, 1.0, 0.0).astype(BF16)
    later2 = jnp.concatenate([later, later], axis=0)
    causal = col < row

    qs = [(q_ref[0, :, h * HEAD_DIM:(h + 1) * HEAD_DIM].astype(F32) * (HEAD_DIM ** -0.5)).astype(BF16)
          for h in range(n_head)]

    def tile(h, j, carry, diagonal):
        lanes = slice(h * HEAD_DIM, (h + 1) * HEAD_DIM)
        start = pl.multiple_of(j * t_blk, t_blk)
        q = qs[h]
        kj = k_ref[0, pl.ds(start, t_blk), lanes]
        vj = v_ref[0, pl.ds(start, t_blk), lanes]
        z = lax.dot_general(q, kj, (((1,), (1,)), ((), ())), preferred_element_type=F32)
        softplus = jnp.maximum(z, 0.0) + jnp.log(1.0 + jnp.exp(-jnp.abs(z)))
        if diagonal:
            softplus = jnp.where(causal, softplus, 0.0)
        hi = softplus.astype(BF16)
        lo = (softplus - hi.astype(F32)).astype(BF16)
        suffix = jnp.dot(jnp.concatenate([hi, lo], axis=1), later2, preferred_element_type=F32)
        wts = jnp.exp(z - softplus - suffix - carry)
        if diagonal:
            wts = jnp.where(causal, wts, 0.0)
        pv = jnp.dot(wts.astype(BF16), vj, preferred_element_type=F32)
        return pv, carry + jnp.sum(softplus, axis=1, keepdims=True)

    prev = jnp.maximum(i - 1, 0)
    accs, carries = [], []
    for h in range(n_head):
        acc, carry = tile(h, i, jnp.zeros((t_blk, 1), F32), True)
        carry = jnp.where(i > 0, carry, NO_BLOCK_CARRY)
        pv, carry = tile(h, prev, carry, False)
        accs.append(acc + pv)
        carries.append(carry)

    def alive_of(carries):
        return functools.reduce(jnp.minimum, [jnp.min(c) for c in carries]) < F32_EXP_ZERO_ABOVE

    def cond(state):
        j, _, _, alive = state
        return jnp.logical_and(j >= 0, alive)

    def body(state):
        j, accs, carries, _ = state
        new_accs, new_carries = [], []
        for h in range(n_head):
            pv, carry = tile(h, j, carries[h], False)
            new_accs.append(accs[h] + pv)
            new_carries.append(carry)
        return j - 1, tuple(new_accs), tuple(new_carries), alive_of(new_carries)

    _, accs, _, _ = lax.while_loop(cond, body, (i - 2, tuple(accs), tuple(carries), alive_of(carries)))
    for h in range(n_head):
        o_ref[0, :, h * HEAD_DIM:(h + 1) * HEAD_DIM] = accs[h]


def _attention(proj3, t_blk=256, n_head=2):
    b, l, _ = proj3.shape
    width = n_head * HEAD_DIM
    per_blk = HEAD_DIM // LANES * n_head
    assert COL_Q_128 % per_blk == 0 and COL_K_128 % per_blk == 0 and COL_V_128 % per_blk == 0
    kern = functools.partial(_attn_kernel, t_blk=t_blk, n_head=n_head)
    est = 2 * (2 * l * width * 2 + t_blk * width * (2 + 4)) + n_head * 16 * t_blk * t_blk * 4
    return pl.pallas_call(
        kern,
        grid=(b, N_HEADS // n_head, l // t_blk),
        in_specs=[
            pl.BlockSpec((1, t_blk, width), lambda bi, h, i: (bi, i, COL_Q_128 // per_blk + h)),
            pl.BlockSpec((1, l, width), lambda bi, h, i: (bi, 0, COL_K_128 // per_blk + h)),
            pl.BlockSpec((1, l, width), lambda bi, h, i: (bi, 0, COL_V_128 // per_blk + h)),
        ],
        out_specs=pl.BlockSpec((1, t_blk, width), lambda bi, h, i: (bi, i, h)),
        out_shape=jax.ShapeDtypeStruct((b, l, D_ATTN), F32),
        compiler_params=pltpu.CompilerParams(
            dimension_semantics=("parallel", "parallel", "arbitrary"),
            vmem_limit_bytes=_vmem_limit(est)),
        name="stickbreak_attn",
    )(proj3, proj3, proj3)


def _attn_finalize_kernel(y_ref, gate_ref, g_ref, o_ref):
    y = y_ref[...]
    y = y * lax.rsqrt(jnp.mean(y * y, axis=-1, keepdims=True) + EPS) * g_ref[...]
    o_ref[...] = (y * _silu(gate_ref[...].astype(F32))).astype(o_ref.dtype)


def _attn_finalize(y_attn2d, proj2d, g_attn, tm=512):
    m = y_attn2d.shape[0]
    return pl.pallas_call(
        _attn_finalize_kernel,
        grid=(m // tm,),
        in_specs=[pl.BlockSpec((tm, D_ATTN), lambda i: (i, 0)),
                  pl.BlockSpec((tm, D_ATTN), lambda i: (i, COL_AGATE_2048)),
                  pl.BlockSpec((1, D_ATTN), lambda i: (0, 0))],
        out_specs=pl.BlockSpec((tm, D_ATTN), lambda i: (i, 0)),
        out_shape=jax.ShapeDtypeStruct((m, D_ATTN), BF16),
        compiler_params=pltpu.CompilerParams(
            dimension_semantics=("parallel",),
            vmem_limit_bytes=_vmem_limit(6 * tm * D_ATTN * 4)),
        name="attn_finalize",
    )(y_attn2d, proj2d, g_attn.reshape(1, D_ATTN))


N_SLAB = 4
SLAB_IN = D_SSM // N_SLAB
SLAB_STATES = 1024
GROUPS_PER_SLAB = N_SSM_GROUPS // N_SLAB
N_CHUNK = SLAB_STATES // LANES


def _discretize_kernel(lam_re_ref, lam_im_ref, log_dt_ref, b_re_ref, b_im_ref,
                       a_re_ref, a_im_ref, wb_re_ref, wb_im_ref):
    lam_re = lam_re_ref[...]
    lam_im = lam_im_ref[...]
    dt = jnp.exp(log_dt_ref[...])
    mag = jnp.exp(lam_re * dt)
    a_re = mag * jnp.cos(lam_im * dt)
    a_im = mag * jnp.sin(lam_im * dt)
    a_re_ref[...] = a_re
    a_im_ref[...] = a_im
    num_re = a_re - 1.0
    den = lam_re * lam_re + lam_im * lam_im
    c_re = ((num_re * lam_re + a_im * lam_im) / den)[:, None, :]
    c_im = ((a_im * lam_re - num_re * lam_im) / den)[:, None, :]
    b_re = b_re_ref[...]
    b_im = b_im_ref[...]
    wb_re_ref[...] = (c_re * b_re - c_im * b_im).astype(wb_re_ref.dtype)
    wb_im_ref[...] = (c_re * b_im + c_im * b_re).astype(wb_im_ref.dtype)


def _block_diag_in(b):
    eye = jnp.eye(GROUPS_PER_SLAB, dtype=b.dtype)
    b4 = b.reshape(N_SLAB, GROUPS_PER_SLAB, SSM_STATE, SSM_GROUP)
    return jnp.einsum('jgpc,gh->jgchp', b4, eye).reshape(N_SLAB, SLAB_IN, SLAB_STATES)


def _block_diag_out(c):
    eye = jnp.eye(GROUPS_PER_SLAB, dtype=c.dtype)
    c4 = c.reshape(N_SLAB, GROUPS_PER_SLAB, SSM_GROUP, SSM_STATE)
    return jnp.einsum('jgcp,gh->jgphc', c4, eye).reshape(N_SLAB, SLAB_STATES, SLAB_IN)


def _discretize(lam_re, lam_im, log_dt, b_re, b_im):
    shape4 = (N_SLAB, SLAB_STATES)
    log_dt_full = jnp.broadcast_to(log_dt[:, None], (N_SSM_GROUPS, SSM_STATE)).reshape(shape4)
    out_shape = (jax.ShapeDtypeStruct(shape4, F32), jax.ShapeDtypeStruct(shape4, F32),
                 jax.ShapeDtypeStruct((N_SLAB, SLAB_IN, SLAB_STATES), BF16),
                 jax.ShapeDtypeStruct((N_SLAB, SLAB_IN, SLAB_STATES), BF16))
    return pl.pallas_call(
        _discretize_kernel, out_shape=out_shape, name="s5_discretize",
        compiler_params=pltpu.CompilerParams(vmem_limit_bytes=_vmem_limit(32 << 20)),
    )(lam_re.reshape(shape4), lam_im.reshape(shape4), log_dt_full,
      _block_diag_in(b_re), _block_diag_in(b_im))


def _s5_kernel(u_ref, gate_ref, a_re_ref, a_im_ref, wb_re_ref, wb_im_ref, wc_re_ref, wc_im_ref,
               dskip_ref, wglu_ref, bglu_ref, g_ref, o_ref,
               s_re_ref, s_im_ref, x_re_ref, x_im_ref, *, t_blk, n_batch):
    n_row = n_batch * N_SLAB
    assert n_row == SUBLANES

    @pl.when(pl.program_id(0) == 0)
    def _():
        x_re_ref[...] = jnp.zeros_like(x_re_ref)
        x_im_ref[...] = jnp.zeros_like(x_im_ref)

    for j in range(N_SLAB):
        cols = slice(j * SLAB_IN, (j + 1) * SLAB_IN)
        uj = jnp.concatenate([u_ref[b, :, cols] for b in range(n_batch)], axis=0)
        bu_re = jnp.dot(uj, wb_re_ref[j], preferred_element_type=F32)
        bu_im = jnp.dot(uj, wb_im_ref[j], preferred_element_type=F32)
        for b in range(n_batch):
            rows = slice(b * t_blk, (b + 1) * t_blk)
            k = b * N_SLAB + j
            for c in range(N_CHUNK):
                lanes = slice(c * LANES, (c + 1) * LANES)
                s_re_ref[c, pl.ds(k, t_blk, stride=n_row), :] = bu_re[rows, lanes]
                s_im_ref[c, pl.ds(k, t_blk, stride=n_row), :] = bu_im[rows, lanes]

    a_re = [a_re_ref[:, c * LANES:(c + 1) * LANES] for c in range(N_CHUNK)]
    a_im = [a_im_ref[:, c * LANES:(c + 1) * LANES] for c in range(N_CHUNK)]
    x_re0 = tuple(x_re_ref[:, c * LANES:(c + 1) * LANES] for c in range(N_CHUNK))
    x_im0 = tuple(x_im_ref[:, c * LANES:(c + 1) * LANES] for c in range(N_CHUNK))

    def step(t, state):
        x_re, x_im = state
        rows = pl.ds(pl.multiple_of(t * n_row, n_row), n_row)
        new_re, new_im = [], []
        for c in range(N_CHUNK):
            nr = a_re[c] * x_re[c] - a_im[c] * x_im[c] + s_re_ref[c, rows, :]
            ni = a_re[c] * x_im[c] + a_im[c] * x_re[c] + s_im_ref[c, rows, :]
            s_re_ref[c, rows, :] = nr
            s_im_ref[c, rows, :] = ni
            new_re.append(nr)
            new_im.append(ni)
        return tuple(new_re), tuple(new_im)

    x_re, x_im = lax.fori_loop(0, t_blk, step, (x_re0, x_im0), unroll=4)
    for c in range(N_CHUNK):
        x_re_ref[:, c * LANES:(c + 1) * LANES] = x_re[c]
        x_im_ref[:, c * LANES:(c + 1) * LANES] = x_im[c]

    def states(s_ref, j):
        return jnp.concatenate(
            [jnp.concatenate([s_ref[c, pl.ds(b * N_SLAB + j, t_blk, stride=n_row), :] for c in range(N_CHUNK)],
                             axis=1) for b in range(n_batch)], axis=0).astype(BF16)

    ys = [jnp.dot(states(s_re_ref, j), wc_re_ref[j], preferred_element_type=F32)
          - jnp.dot(states(s_im_ref, j), wc_im_ref[j], preferred_element_type=F32) for j in range(N_SLAB)]
    u_all = jnp.concatenate([u_ref[b] for b in range(n_batch)], axis=0).astype(F32)
    y = jnp.concatenate(ys, axis=1) + dskip_ref[...] * u_all
    h = jax.nn.gelu(y)
    glu = jnp.dot(h.astype(BF16), wglu_ref[...], preferred_element_type=F32) + bglu_ref[...]
    out = glu[:, :D_SSM] * jax.nn.sigmoid(glu[:, D_SSM:])
    out = out * lax.rsqrt(jnp.mean(out * out, axis=-1, keepdims=True) + EPS) * g_ref[...]
    for b in range(n_batch):
        rows = slice(b * t_blk, (b + 1) * t_blk)
        o_ref[b] = (out[rows] * _silu(gate_ref[b].astype(F32))).astype(o_ref.dtype)


def _s5_branch(proj3, a_re, a_im, wb_re, wb_im, wc_re, wc_im, d_skip, w_glu, b_glu, g_ssm, t_blk=256):
    b, l, _ = proj3.shape
    n_row = b * N_SLAB
    a_re_t = jnp.tile(a_re, (b, 1))
    a_im_t = jnp.tile(a_im, (b, 1))
    kern = functools.partial(_s5_kernel, t_blk=t_blk, n_batch=b)
    weights = 4 * N_SLAB * SLAB_IN * SLAB_STATES * 2 + D_SSM * 2 * D_SSM * 2
    est = (weights + 2 * N_CHUNK * t_blk * n_row * LANES * 4 + 2 * 3 * b * t_blk * D_SSM * 2
           + 6 * b * t_blk * 2 * D_SSM * 4)
    return pl.pallas_call(
        kern,
        grid=(l // t_blk,),
        in_specs=[
            pl.BlockSpec((b, t_blk, D_SSM), lambda i: (0, i, COL_SU_1024)),
            pl.BlockSpec((b, t_blk, D_SSM), lambda i: (0, i, COL_SGATE_1024)),
            _resident(n_row, SLAB_STATES), _resident(n_row, SLAB_STATES),
            _resident(N_SLAB, SLAB_IN, SLAB_STATES), _resident(N_SLAB, SLAB_IN, SLAB_STATES),
            _resident(N_SLAB, SLAB_STATES, SLAB_IN), _resident(N_SLAB, SLAB_STATES, SLAB_IN),
            _resident(1, D_SSM), _resident(D_SSM, 2 * D_SSM), _resident(1, 2 * D_SSM), _resident(1, D_SSM),
        ],
        out_specs=pl.BlockSpec((b, t_blk, D_SSM), lambda i: (0, i, 0)),
        out_shape=jax.ShapeDtypeStruct((b, l, D_SSM), BF16),
        scratch_shapes=[pltpu.VMEM((N_CHUNK, t_blk * n_row, LANES), F32),
                        pltpu.VMEM((N_CHUNK, t_blk * n_row, LANES), F32),
                        pltpu.VMEM((n_row, SLAB_STATES), F32),
                        pltpu.VMEM((n_row, SLAB_STATES), F32)],
        compiler_params=pltpu.CompilerParams(
            dimension_semantics=("arbitrary",),
            vmem_limit_bytes=_vmem_limit(est)),
        name="s5_branch",
    )(proj3, proj3, a_re_t, a_im_t, wb_re, wb_im, wc_re, wc_im,
      d_skip.reshape(1, D_SSM), w_glu, b_glu.reshape(1, 2 * D_SSM), g_ssm.reshape(1, D_SSM))


def _layer(x2d, h, bsz, next_g, last, w_in, w_pool, pool_scale, lam_re, lam_im, log_dt,
           b_re, b_im, c_re, c_im, d_skip, w_glu, b_glu, branch_g, w_out):
    m = x2d.shape[0]
    l = m // bsz
    proj = _in_proj(h, w_in.astype(BF16))
    proj3 = proj.reshape(bsz, l, D_IN)

    g_pool = branch_g[:D_POOL]
    g_attn = branch_g[D_POOL:D_POOL + D_ATTN]
    g_ssm = branch_g[D_POOL + D_ATTN:]

    y_pool = _pool_branch(proj3, w_pool.astype(BF16), pool_scale, g_pool)
    y_attn = _attention(proj3)
    y_attn = _attn_finalize(y_attn.reshape(m, D_ATTN), proj, g_attn)
    a_re, a_im, wb_re, wb_im = _discretize(lam_re, lam_im, log_dt, b_re, b_im)
    y_ssm = _s5_branch(proj3, a_re, a_im, wb_re, wb_im,
                       _block_diag_out(c_re).astype(BF16), _block_diag_out(c_im).astype(BF16),
                       d_skip, w_glu.astype(BF16), b_glu, g_ssm)
    return _out_proj(y_pool.reshape(m, D_POOL), y_attn, y_ssm.reshape(m, D_SSM),
                     w_out.astype(BF16), x2d, next_g, last)


def kernel(x, ln_g, w_in, w_pool, pool_scale, lam_re, lam_im, log_dt, b_re, b_im, c_re, c_im,
           d_skip, w_glu, b_glu, branch_g, w_out, final_g):
    bsz, l, d = x.shape
    depth = ln_g.shape[0]
    x2d = x.reshape(bsz * l, d)
    h = _rmsnorm(x2d, ln_g[0], BF16)
    for i in range(depth):
        last = i == depth - 1
        next_g = final_g if last else ln_g[i + 1]
        res = _layer(x2d, h, bsz, next_g, last, w_in[i], w_pool[i], pool_scale[i], lam_re[i], lam_im[i],
                     log_dt[i], b_re[i], b_im[i], c_re[i], c_im[i], d_skip[i],
                     w_glu[i], b_glu[i], branch_g[i], w_out[i])
        if last:
            return res.reshape(bsz, l, d)
        x2d, h = res
```

```python
import functools
import math

import jax
import jax.numpy as jnp
from jax import lax
from jax.experimental import pallas as pl
from jax.experimental.pallas import tpu as pltpu

D_MODEL = 4096
D_POOL = 1024
D_ATTN = 2048
D_SSM = 1024
POOL_WINDOWS = (2, 4, 8, 16)
POOL_GROUP = 256
POOL_HALO = 16
HEAD_DIM = 128
N_HEADS = 16
SSM_GROUP = 16
SSM_STATE = 64
N_SSM_GROUPS = 64
D_IN = 12288
EPS = 1e-6

COL_PX_1024 = 0
COL_PGATE_1024 = 1
COL_Q_2048 = 1
COL_K_2048 = 2
COL_V_2048 = 3
COL_AGATE_2048 = 4
COL_SU_1024 = 10
COL_SGATE_1024 = 11

V7X_VMEM_BYTES = 64 * 1024 * 1024
SUBLANES = 8
LANES = 128

LOG2_E = 1.0 / math.log(2.0)
F32_EXP2_ZERO_ABOVE = 150.0
NO_BLOCK_CARRY = 1e30

BF16 = jnp.bfloat16
F32 = jnp.float32


def _vmem_limit(nbytes):
    return int(min(nbytes * 3 // 2 + (4 << 20), V7X_VMEM_BYTES - (6 << 20)))


def _silu(x):
    return x * jax.nn.sigmoid(x)


def _resident(*shape):
    return pl.BlockSpec(shape, lambda *_: (0,) * len(shape), pipeline_mode=pl.Buffered(1))


def _rmsnorm_kernel(x_ref, g_ref, o_ref):
    x = x_ref[...]
    y = x * lax.rsqrt(jnp.mean(x * x, axis=-1, keepdims=True) + EPS)
    o_ref[...] = (y * g_ref[...]).astype(o_ref.dtype)


def _rmsnorm(x2d, g, out_dtype, tm=512):
    m, d = x2d.shape
    return pl.pallas_call(
        _rmsnorm_kernel,
        grid=(m // tm,),
        in_specs=[pl.BlockSpec((tm, d), lambda i: (i, 0)),
                  pl.BlockSpec((1, d), lambda i: (0, 0))],
        out_specs=pl.BlockSpec((tm, d), lambda i: (i, 0)),
        out_shape=jax.ShapeDtypeStruct((m, d), out_dtype),
        compiler_params=pltpu.CompilerParams(
            dimension_semantics=("parallel",),
            vmem_limit_bytes=_vmem_limit(2 * tm * d * 8)),
        name="rmsnorm",
    )(x2d, g.reshape(1, d))


def _in_proj_kernel(a_ref, w_ref, o_ref, wbf_ref):
    @pl.when(pl.program_id(1) == 0)
    def _():
        wbf_ref[...] = w_ref[...].astype(BF16)

    o_ref[...] = jnp.dot(a_ref[...], wbf_ref[...], preferred_element_type=F32).astype(o_ref.dtype)


def _in_proj(h, w_all, layer, tm=512, tn=1024):
    m, k = h.shape
    n = w_all.shape[2]
    est = 2 * (tm * k * 2 + k * tn * 4 + tm * tn * 2) + k * tn * 2
    return pl.pallas_call(
        _in_proj_kernel,
        grid=(n // tn, m // tm),
        in_specs=[pl.BlockSpec((tm, k), lambda j, i: (i, 0)),
                  pl.BlockSpec((None, k, tn), lambda j, i: (layer, 0, j))],
        out_specs=pl.BlockSpec((tm, tn), lambda j, i: (i, j)),
        out_shape=jax.ShapeDtypeStruct((m, n), BF16),
        scratch_shapes=[pltpu.VMEM((k, tn), BF16)],
        compiler_params=pltpu.CompilerParams(
            dimension_semantics=("parallel", "arbitrary"),
            vmem_limit_bytes=_vmem_limit(est)),
        name="in_proj",
    )(h, w_all)


def _out_proj_kernel(yp_ref, ya_ref, ys_ref, w_ref, x_ref, o_ref, wbf_ref):
    @pl.when(pl.program_id(1) == 0)
    def _():
        wbf_ref[...] = w_ref[...].astype(BF16)

    acc = jnp.dot(yp_ref[...], wbf_ref[0:D_POOL, :], preferred_element_type=F32)
    acc += jnp.dot(ya_ref[...], wbf_ref[D_POOL:D_POOL + D_ATTN, :], preferred_element_type=F32)
    acc += jnp.dot(ys_ref[...], wbf_ref[D_POOL + D_ATTN:, :], preferred_element_type=F32)
    o_ref[...] = x_ref[...] + acc


def _out_proj(y_pool, y_attn, y_ssm, w_all, layer, x2d, tm=512, tn=1024):
    m = x2d.shape[0]
    k, n = w_all.shape[1:]
    est = 2 * (tm * k * 2 + 2 * tm * tn * 4) + k * tn * (4 + 2)
    return pl.pallas_call(
        _out_proj_kernel,
        grid=(n // tn, m // tm),
        in_specs=[pl.BlockSpec((tm, D_POOL), lambda j, i: (i, 0)),
                  pl.BlockSpec((tm, D_ATTN), lambda j, i: (i, 0)),
                  pl.BlockSpec((tm, D_SSM), lambda j, i: (i, 0)),
                  pl.BlockSpec((None, k, tn), lambda j, i: (layer, 0, j), pipeline_mode=pl.Buffered(1)),
                  pl.BlockSpec((tm, tn), lambda j, i: (i, j))],
        out_specs=pl.BlockSpec((tm, tn), lambda j, i: (i, j)),
        out_shape=jax.ShapeDtypeStruct((m, n), F32),
        scratch_shapes=[pltpu.VMEM((k, tn), BF16)],
        compiler_params=pltpu.CompilerParams(
            dimension_semantics=("parallel", "arbitrary"),
            vmem_limit_bytes=_vmem_limit(est)),
        name="out_proj",
    )(y_pool, y_attn, y_ssm, w_all, x2d)


def _pool_kernel(px_ref, halo_ref, gate_ref, w_ref, scale_ref, g_ref, o_ref, xs_ref, *, t_blk):
    i = pl.program_id(1)
    x = px_ref[0].astype(F32)
    xs_ref[0:POOL_HALO, :] = jnp.where(i > 0, halo_ref[0].astype(F32), 0.0)
    xs_ref[POOL_HALO:POOL_HALO + t_blk, :] = x
    pos = (i * t_blk + 1 + lax.broadcasted_iota(jnp.int32, (t_blk, 1), 0)).astype(F32)
    mixed = []
    for g, w in enumerate(POOL_WINDOWS):
        cols = slice(g * POOL_GROUP, (g + 1) * POOL_GROUP)
        s = x[:, cols]
        for j in range(1, w):
            s = s + xs_ref[POOL_HALO - j:POOL_HALO - j + t_blk, cols]
        pooled = s / jnp.minimum(pos, float(w)) - x[:, cols]
        mixed.append(jnp.dot(pooled.astype(BF16), w_ref[g], preferred_element_type=F32))
    y = jnp.concatenate(mixed, axis=1) * scale_ref[...]
    y = y * lax.rsqrt(jnp.mean(y * y, axis=-1, keepdims=True) + EPS) * g_ref[...]
    o_ref[0] = (y * _silu(gate_ref[0].astype(F32))).astype(o_ref.dtype)


def _pool_branch(proj3, w_pool, pool_scale, g_pool, t_blk=512):
    b, l, _ = proj3.shape
    halo_per_blk = t_blk // POOL_HALO
    kern = functools.partial(_pool_kernel, t_blk=t_blk)
    return pl.pallas_call(
        kern,
        grid=(b, l // t_blk),
        in_specs=[
            pl.BlockSpec((1, t_blk, D_POOL), lambda bi, i: (bi, i, COL_PX_1024)),
            pl.BlockSpec((1, POOL_HALO, D_POOL),
                         lambda bi, i: (bi, jnp.maximum(i * halo_per_blk - 1, 0), COL_PX_1024)),
            pl.BlockSpec((1, t_blk, D_POOL), lambda bi, i: (bi, i, COL_PGATE_1024)),
            pl.BlockSpec((len(POOL_WINDOWS), POOL_GROUP, POOL_GROUP), lambda bi, i: (0, 0, 0)),
            pl.BlockSpec((1, D_POOL), lambda bi, i: (0, 0)),
            pl.BlockSpec((1, D_POOL), lambda bi, i: (0, 0)),
        ],
        out_specs=pl.BlockSpec((1, t_blk, D_POOL), lambda bi, i: (bi, i, 0)),
        out_shape=jax.ShapeDtypeStruct((b, l, D_POOL), BF16),
        scratch_shapes=[pltpu.VMEM((POOL_HALO + t_blk, D_POOL), F32)],
        compiler_params=pltpu.CompilerParams(
            dimension_semantics=("parallel", "parallel"),
            vmem_limit_bytes=_vmem_limit(8 * t_blk * D_POOL * 4)),
        name="pool_branch",
    )(proj3, proj3, proj3, w_pool, pool_scale.reshape(1, D_POOL), g_pool.reshape(1, D_POOL))


def _attn_kernel(q_ref, k_ref, v_ref, gate_ref, g_ref, o_ref, *, t_blk):
    n_head = N_HEADS
    i = pl.program_id(1)
    row = lax.broadcasted_iota(jnp.int32, (t_blk, t_blk), 0)
    col = lax.broadcasted_iota(jnp.int32, (t_blk, t_blk), 1)
    later = jnp.where(row > col, 1.0, 0.0).astype(BF16)
    causal = col < row
    qs = [(q_ref[0, :, h * HEAD_DIM:(h + 1) * HEAD_DIM].astype(F32) * (HEAD_DIM ** -0.5 * LOG2_E)).astype(BF16)
          for h in range(n_head)]

    def tile(h, j, carry, diagonal):
        lanes = slice(h * HEAD_DIM, (h + 1) * HEAD_DIM)
        start = pl.multiple_of(j * t_blk, t_blk)
        kj = k_ref[0, pl.ds(start, t_blk), lanes]
        vj = v_ref[0, pl.ds(start, t_blk), lanes]
        z = lax.dot_general(qs[h], kj, (((1,), (1,)), ((), ())), preferred_element_type=F32)
        softplus = jnp.maximum(z, 0.0) + jnp.log(1.0 + jnp.exp2(-jnp.abs(z))) * LOG2_E
        if diagonal:
            softplus = jnp.where(causal, softplus, 0.0)
        suffix = jnp.dot(softplus.astype(BF16), later, preferred_element_type=F32)
        wts = jnp.exp2(z - softplus - suffix - carry)
        if diagonal:
            wts = jnp.where(causal, wts, 0.0)
        pv = jnp.dot(wts.astype(BF16), vj, preferred_element_type=F32)
        return pv, carry + jnp.sum(softplus, axis=1, keepdims=True)

    prev = jnp.maximum(i - 1, 0)
    accs, carries = [], []
    for h in range(n_head):
        acc, carry = tile(h, i, jnp.zeros((t_blk, 1), F32), True)
        carry = jnp.where(i > 0, carry, NO_BLOCK_CARRY)
        pv, carry = tile(h, prev, carry, False)
        accs.append(acc + pv)
        carries.append(carry)

    def alive_of(carries):
        return jnp.min(functools.reduce(jnp.minimum, carries)) < F32_EXP2_ZERO_ABOVE

    def cond(state):
        j, _, _, alive = state
        return jnp.logical_and(j >= 0, alive)

    def body(state):
        j, accs, carries, _ = state
        new_accs, new_carries = [], []
        for h in range(n_head):
            pv, carry = tile(h, j, carries[h], False)
            new_accs.append(accs[h] + pv)
            new_carries.append(carry)
        return j - 1, tuple(new_accs), tuple(new_carries), alive_of(new_carries)

    _, accs, _, _ = lax.while_loop(cond, body, (i - 2, tuple(accs), tuple(carries), alive_of(carries)))
    y = jnp.concatenate(accs, axis=1)
    y = y * lax.rsqrt(jnp.mean(y * y, axis=-1, keepdims=True) + EPS) * g_ref[...]
    o_ref[0] = (y * _silu(gate_ref[0].astype(F32))).astype(o_ref.dtype)


def _attn_branch(proj3, g_attn, t_blk=256):
    b, l, _ = proj3.shape
    kern = functools.partial(_attn_kernel, t_blk=t_blk)
    est = 2 * l * D_ATTN * 2 + 2 * 3 * t_blk * D_ATTN * 2 + N_HEADS * 12 * t_blk * t_blk * 4
    whole_seq = lambda col: pl.BlockSpec((1, l, D_ATTN), lambda bi, i: (bi, 0, col), pipeline_mode=pl.Buffered(1))
    q_rows = lambda col: pl.BlockSpec((1, t_blk, D_ATTN), lambda bi, i: (bi, i, col))
    return pl.pallas_call(
        kern,
        grid=(b, l // t_blk),
        in_specs=[q_rows(COL_Q_2048), whole_seq(COL_K_2048), whole_seq(COL_V_2048), q_rows(COL_AGATE_2048),
                  pl.BlockSpec((1, D_ATTN), lambda bi, i: (0, 0))],
        out_specs=pl.BlockSpec((1, t_blk, D_ATTN), lambda bi, i: (bi, i, 0)),
        out_shape=jax.ShapeDtypeStruct((b, l, D_ATTN), BF16),
        compiler_params=pltpu.CompilerParams(
            dimension_semantics=("parallel", "arbitrary"),
            vmem_limit_bytes=_vmem_limit(est)),
        name="attn_branch",
    )(proj3, proj3, proj3, proj3, g_attn.reshape(1, D_ATTN))


N_SLAB = 4
SLAB_IN = D_SSM // N_SLAB
SLAB_STATES = 1024
GROUPS_PER_SLAB = N_SSM_GROUPS // N_SLAB
N_CHUNK = SLAB_STATES // LANES


def _discretize_kernel(lam_re_ref, lam_im_ref, log_dt_ref, b_re_ref, b_im_ref,
                       a_re_ref, a_im_ref, wb_re_ref, wb_im_ref):
    lam_re = lam_re_ref[...]
    lam_im = lam_im_ref[...]
    dt = jnp.exp(log_dt_ref[...])
    mag = jnp.exp(lam_re * dt)
    a_re = mag * jnp.cos(lam_im * dt)
    a_im = mag * jnp.sin(lam_im * dt)
    a_re_ref[...] = a_re
    a_im_ref[...] = a_im
    num_re = a_re - 1.0
    den = lam_re * lam_re + lam_im * lam_im
    c_re = ((num_re * lam_re + a_im * lam_im) / den)[:, None, :]
    c_im = ((a_im * lam_re - num_re * lam_im) / den)[:, None, :]
    b_re = b_re_ref[...]
    b_im = b_im_ref[...]
    wb_re_ref[...] = (c_re * b_re - c_im * b_im).astype(wb_re_ref.dtype)
    wb_im_ref[...] = (c_re * b_im + c_im * b_re).astype(wb_im_ref.dtype)


def _block_diag_in(b):
    eye = jnp.eye(GROUPS_PER_SLAB, dtype=b.dtype)
    b4 = b.reshape(N_SLAB, GROUPS_PER_SLAB, SSM_STATE, SSM_GROUP)
    return jnp.einsum('jgpc,gh->jgchp', b4, eye).reshape(N_SLAB, SLAB_IN, SLAB_STATES)


def _block_diag_out(c):
    eye = jnp.eye(GROUPS_PER_SLAB, dtype=c.dtype)
    c4 = c.reshape(N_SLAB, GROUPS_PER_SLAB, SSM_GROUP, SSM_STATE)
    return jnp.einsum('jgcp,gh->jgphc', c4, eye).reshape(N_SLAB, SLAB_STATES, SLAB_IN)


def _discretize(lam_re, lam_im, log_dt, b_re, b_im):
    shape4 = (N_SLAB, SLAB_STATES)
    log_dt_full = jnp.broadcast_to(log_dt[:, None], (N_SSM_GROUPS, SSM_STATE)).reshape(shape4)
    out_shape = (jax.ShapeDtypeStruct(shape4, F32), jax.ShapeDtypeStruct(shape4, F32),
                 jax.ShapeDtypeStruct((N_SLAB, SLAB_IN, SLAB_STATES), BF16),
                 jax.ShapeDtypeStruct((N_SLAB, SLAB_IN, SLAB_STATES), BF16))
    return pl.pallas_call(
        _discretize_kernel, out_shape=out_shape, name="s5_discretize",
        compiler_params=pltpu.CompilerParams(vmem_limit_bytes=_vmem_limit(32 << 20)),
    )(lam_re.reshape(shape4), lam_im.reshape(shape4), log_dt_full,
      _block_diag_in(b_re), _block_diag_in(b_im))


def _s5_kernel(u_ref, gate_ref, a_re_ref, a_im_ref, wb_re_ref, wb_im_ref, wc_re_ref, wc_im_ref,
               dskip_ref, wglu_ref, bglu_ref, g_ref, o_ref,
               s_re_ref, s_im_ref, x_re_ref, x_im_ref, *, t_blk, n_batch):
    n_row = n_batch * N_SLAB
    assert n_row == SUBLANES

    @pl.when(pl.program_id(0) == 0)
    def _():
        x_re_ref[...] = jnp.zeros_like(x_re_ref)
        x_im_ref[...] = jnp.zeros_like(x_im_ref)

    for j in range(N_SLAB):
        cols = slice(j * SLAB_IN, (j + 1) * SLAB_IN)
        uj = jnp.concatenate([u_ref[b, :, cols] for b in range(n_batch)], axis=0)
        bu_re = jnp.dot(uj, wb_re_ref[j], preferred_element_type=F32)
        bu_im = jnp.dot(uj, wb_im_ref[j], preferred_element_type=F32)
        for b in range(n_batch):
            rows = slice(b * t_blk, (b + 1) * t_blk)
            k = b * N_SLAB + j
            for c in range(N_CHUNK):
                lanes = slice(c * LANES, (c + 1) * LANES)
                s_re_ref[c, pl.ds(k, t_blk, stride=n_row), :] = bu_re[rows, lanes]
                s_im_ref[c, pl.ds(k, t_blk, stride=n_row), :] = bu_im[rows, lanes]

    a_re = [a_re_ref[:, c * LANES:(c + 1) * LANES] for c in range(N_CHUNK)]
    a_im = [a_im_ref[:, c * LANES:(c + 1) * LANES] for c in range(N_CHUNK)]
    x_re0 = tuple(x_re_ref[:, c * LANES:(c + 1) * LANES] for c in range(N_CHUNK))
    x_im0 = tuple(x_im_ref[:, c * LANES:(c + 1) * LANES] for c in range(N_CHUNK))

    def step(t, state):
        x_re, x_im = state
        rows = pl.ds(pl.multiple_of(t * n_row, n_row), n_row)
        new_re, new_im = [], []
        for c in range(N_CHUNK):
            nr = a_re[c] * x_re[c] - a_im[c] * x_im[c] + s_re_ref[c, rows, :]
            ni = a_re[c] * x_im[c] + a_im[c] * x_re[c] + s_im_ref[c, rows, :]
            s_re_ref[c, rows, :] = nr
            s_im_ref[c, rows, :] = ni
            new_re.append(nr)
            new_im.append(ni)
        return tuple(new_re), tuple(new_im)

    x_re, x_im = lax.fori_loop(0, t_blk, step, (x_re0, x_im0), unroll=4)
    for c in range(N_CHUNK):
        x_re_ref[:, c * LANES:(c + 1) * LANES] = x_re[c]
        x_im_ref[:, c * LANES:(c + 1) * LANES] = x_im[c]

    def states(s_ref, j):
        return jnp.concatenate(
            [jnp.concatenate([s_ref[c, pl.ds(b * N_SLAB + j, t_blk, stride=n_row), :] for c in range(N_CHUNK)],
                             axis=1) for b in range(n_batch)], axis=0).astype(BF16)

    ys = [jnp.dot(states(s_re_ref, j), wc_re_ref[j], preferred_element_type=F32)
          - jnp.dot(states(s_im_ref, j), wc_im_ref[j], preferred_element_type=F32) for j in range(N_SLAB)]
    u_all = jnp.concatenate([u_ref[b] for b in range(n_batch)], axis=0).astype(F32)
    y = jnp.concatenate(ys, axis=1) + dskip_ref[...] * u_all
    h = jax.nn.gelu(y)
    glu = jnp.dot(h.astype(BF16), wglu_ref[...], preferred_element_type=F32) + bglu_ref[...]
    out = glu[:, :D_SSM] * jax.nn.sigmoid(glu[:, D_SSM:])
    out = out * lax.rsqrt(jnp.mean(out * out, axis=-1, keepdims=True) + EPS) * g_ref[...]
    for b in range(n_batch):
        rows = slice(b * t_blk, (b + 1) * t_blk)
        o_ref[b] = (out[rows] * _silu(gate_ref[b].astype(F32))).astype(o_ref.dtype)


def _s5_branch(proj3, a_re, a_im, wb_re, wb_im, wc_re, wc_im, d_skip, w_glu, b_glu, g_ssm, t_blk=256):
    b, l, _ = proj3.shape
    n_row = b * N_SLAB
    a_re_t = jnp.tile(a_re, (b, 1))
    a_im_t = jnp.tile(a_im, (b, 1))
    kern = functools.partial(_s5_kernel, t_blk=t_blk, n_batch=b)
    weights = 4 * N_SLAB * SLAB_IN * SLAB_STATES * 2 + D_SSM * 2 * D_SSM * 2
    est = (weights + 2 * N_CHUNK * t_blk * n_row * LANES * 4 + 2 * 3 * b * t_blk * D_SSM * 2
           + 6 * b * t_blk * 2 * D_SSM * 4)
    return pl.pallas_call(
        kern,
        grid=(l // t_blk,),
        in_specs=[
            pl.BlockSpec((b, t_blk, D_SSM), lambda i: (0, i, COL_SU_1024)),
            pl.BlockSpec((b, t_blk, D_SSM), lambda i: (0, i, COL_SGATE_1024)),
            _resident(n_row, SLAB_STATES), _resident(n_row, SLAB_STATES),
            _resident(N_SLAB, SLAB_IN, SLAB_STATES), _resident(N_SLAB, SLAB_IN, SLAB_STATES),
            _resident(N_SLAB, SLAB_STATES, SLAB_IN), _resident(N_SLAB, SLAB_STATES, SLAB_IN),
            _resident(1, D_SSM), _resident(D_SSM, 2 * D_SSM), _resident(1, 2 * D_SSM), _resident(1, D_SSM),
        ],
        out_specs=pl.BlockSpec((b, t_blk, D_SSM), lambda i: (0, i, 0)),
        out_shape=jax.ShapeDtypeStruct((b, l, D_SSM), BF16),
        scratch_shapes=[pltpu.VMEM((N_CHUNK, t_blk * n_row, LANES), F32),
                        pltpu.VMEM((N_CHUNK, t_blk * n_row, LANES), F32),
                        pltpu.VMEM((n_row, SLAB_STATES), F32),
                        pltpu.VMEM((n_row, SLAB_STATES), F32)],
        compiler_params=pltpu.CompilerParams(
            dimension_semantics=("arbitrary",),
            vmem_limit_bytes=_vmem_limit(est)),
        name="s5_branch",
    )(proj3, proj3, a_re_t, a_im_t, wb_re, wb_im, wc_re, wc_im,
      d_skip.reshape(1, D_SSM), w_glu, b_glu.reshape(1, 2 * D_SSM), g_ssm.reshape(1, D_SSM))


def _layer(x2d, bsz, layer, ln_g, w_in_all, w_pool, pool_scale, lam_re, lam_im, log_dt,
           b_re, b_im, c_re, c_im, d_skip, w_glu, b_glu, branch_g, w_out_all):
    m = x2d.shape[0]
    l = m // bsz
    h = _rmsnorm(x2d, ln_g, BF16)
    proj3 = _in_proj(h, w_in_all, layer).reshape(bsz, l, D_IN)

    g_pool = branch_g[:D_POOL]
    g_attn = branch_g[D_POOL:D_POOL + D_ATTN]
    g_ssm = branch_g[D_POOL + D_ATTN:]

    y_pool = _pool_branch(proj3, w_pool.astype(BF16), pool_scale, g_pool)
    y_attn = _attn_branch(proj3, g_attn)
    a_re, a_im, wb_re, wb_im = _discretize(lam_re, lam_im, log_dt, b_re, b_im)
    y_ssm = _s5_branch(proj3, a_re, a_im, wb_re, wb_im,
                       _block_diag_out(c_re).astype(BF16), _block_diag_out(c_im).astype(BF16),
                       d_skip, w_glu.astype(BF16), b_glu, g_ssm)
    return _out_proj(y_pool.reshape(m, D_POOL), y_attn.reshape(m, D_ATTN), y_ssm.reshape(m, D_SSM),
                     w_out_all, layer, x2d)


def kernel(x, ln_g, w_in, w_pool, pool_scale, lam_re, lam_im, log_dt, b_re, b_im, c_re, c_im,
           d_skip, w_glu, b_glu, branch_g, w_out, final_g):
    bsz, l, d = x.shape
    h = x.reshape(bsz * l, d)
    for i in range(ln_g.shape[0]):
        h = _layer(h, bsz, i, ln_g[i], w_in, w_pool[i], pool_scale[i], lam_re[i], lam_im[i],
                   log_dt[i], b_re[i], b_im[i], c_re[i], c_im[i], d_skip[i],
                   w_glu[i], b_glu[i], branch_g[i], w_out)
    return _rmsnorm(h, final_g, x.dtype).reshape(bsz, l, d)
```

```python
import functools
import math

import jax
import jax.numpy as jnp
from jax import lax
from jax.experimental import pallas as pl
from jax.experimental.pallas import tpu as pltpu

D_MODEL = 4096
D_POOL = 1024
D_ATTN = 2048
D_SSM = 1024
POOL_WINDOWS = (2, 4, 8, 16)
POOL_GROUP = 256
POOL_HALO = 16
HEAD_DIM = 128
N_HEADS = 16
SSM_GROUP = 16
SSM_STATE = 64
N_SSM_GROUPS = 64
D_IN = 12288
EPS = 1e-6

COL_PX_1024 = 0
COL_PGATE_1024 = 1
COL_Q_2048 = 1
COL_K_2048 = 2
COL_V_2048 = 3
COL_AGATE_2048 = 4
COL_SU_1024 = 10
COL_SGATE_1024 = 11

V7X_VMEM_BYTES = 64 * 1024 * 1024
SUBLANES = 8
LANES = 128

LOG2_E = 1.0 / math.log(2.0)
F32_EXP2_ZERO_ABOVE = 150.0
NO_BLOCK_CARRY = 1e30

BF16 = jnp.bfloat16
F32 = jnp.float32


def _vmem_limit(nbytes):
    return int(min(nbytes * 3 // 2 + (4 << 20), V7X_VMEM_BYTES - (6 << 20)))


def _silu(x):
    return x * jax.nn.sigmoid(x)


def _resident(*shape):
    return pl.BlockSpec(shape, lambda *_: (0,) * len(shape), pipeline_mode=pl.Buffered(1))


def _rmsnorm_kernel(x_ref, g_ref, o_ref):
    x = x_ref[...]
    y = x * lax.rsqrt(jnp.mean(x * x, axis=-1, keepdims=True) + EPS)
    o_ref[...] = (y * g_ref[...]).astype(o_ref.dtype)


def _rmsnorm(x2d, g, out_dtype, tm=512):
    m, d = x2d.shape
    return pl.pallas_call(
        _rmsnorm_kernel,
        grid=(m // tm,),
        in_specs=[pl.BlockSpec((tm, d), lambda i: (i, 0)),
                  pl.BlockSpec((1, d), lambda i: (0, 0))],
        out_specs=pl.BlockSpec((tm, d), lambda i: (i, 0)),
        out_shape=jax.ShapeDtypeStruct((m, d), out_dtype),
        compiler_params=pltpu.CompilerParams(
            dimension_semantics=("parallel",),
            vmem_limit_bytes=_vmem_limit(2 * tm * d * 8)),
        name="rmsnorm",
    )(x2d, g.reshape(1, d))


CAST_ROWS = 256


def _in_proj_f32w_kernel(a_ref, w_ref, wnext_ref, o_ref, wnext_bf_ref, wbf_ref):
    @pl.when(pl.program_id(1) == 0)
    def _():
        wbf_ref[...] = w_ref[...].astype(BF16)

    o_ref[...] = jnp.dot(a_ref[...], wbf_ref[...], preferred_element_type=F32).astype(o_ref.dtype)
    wnext_bf_ref[...] = wnext_ref[...].astype(BF16)


def _in_proj_f32w(h, w_all, layer, tm=512, tn=1024):
    m, k = h.shape
    n = w_all.shape[2]
    assert (m // tm) * CAST_ROWS == k
    est = 2 * (tm * k * 2 + k * tn * 4 + tm * tn * 2 + CAST_ROWS * tn * 6) + k * tn * 2
    return pl.pallas_call(
        _in_proj_f32w_kernel,
        grid=(n // tn, m // tm),
        in_specs=[pl.BlockSpec((tm, k), lambda j, i: (i, 0)),
                  pl.BlockSpec((None, k, tn), lambda j, i: (layer, 0, j)),
                  pl.BlockSpec((None, CAST_ROWS, tn), lambda j, i: (layer + 1, i, j))],
        out_specs=(pl.BlockSpec((tm, tn), lambda j, i: (i, j)),
                   pl.BlockSpec((CAST_ROWS, tn), lambda j, i: (i, j))),
        out_shape=(jax.ShapeDtypeStruct((m, n), BF16), jax.ShapeDtypeStruct((k, n), BF16)),
        scratch_shapes=[pltpu.VMEM((k, tn), BF16)],
        compiler_params=pltpu.CompilerParams(
            dimension_semantics=("parallel", "arbitrary"),
            vmem_limit_bytes=_vmem_limit(est)),
        name="in_proj_f32w",
    )(h, w_all, w_all)


def _in_proj_scaled_kernel(a_ref, ssq_ref, w_ref, o_ref, r_ref):
    @pl.when(pl.program_id(1) == 0)
    def _():
        ssq = jnp.sum(jnp.sum(ssq_ref[...], axis=0), axis=-1, keepdims=True)
        r_ref[...] = lax.rsqrt(ssq / D_MODEL + EPS)

    acc = jnp.dot(a_ref[...], w_ref[...], preferred_element_type=F32)
    o_ref[...] = (acc * r_ref[...]).astype(o_ref.dtype)


def _in_proj_scaled(xg, ssq, w, tm=1024, tn=1024):
    m, k = xg.shape
    n = w.shape[1]
    parts = ssq.shape[0]
    est = 2 * (tm * k * 2 + k * tn * 2 + tm * tn * 2 + parts * tm * LANES * 4) + tm * LANES * 4
    return pl.pallas_call(
        _in_proj_scaled_kernel,
        grid=(m // tm, n // tn),
        in_specs=[pl.BlockSpec((tm, k), lambda i, j: (i, 0)),
                  pl.BlockSpec((parts, tm, LANES), lambda i, j: (0, i, 0)),
                  pl.BlockSpec((k, tn), lambda i, j: (0, j))],
        out_specs=pl.BlockSpec((tm, tn), lambda i, j: (i, j)),
        out_shape=jax.ShapeDtypeStruct((m, n), BF16),
        scratch_shapes=[pltpu.VMEM((tm, 1), F32)],
        compiler_params=pltpu.CompilerParams(
            dimension_semantics=("parallel", "arbitrary"),
            vmem_limit_bytes=_vmem_limit(est)),
        name="in_proj_scaled",
    )(xg, ssq, w)


def _branch_matmul(yp_ref, ya_ref, ys_ref, w_ref):
    acc = jnp.dot(yp_ref[...], w_ref[0:D_POOL, :], preferred_element_type=F32)
    acc += jnp.dot(ya_ref[...], w_ref[D_POOL:D_POOL + D_ATTN, :], preferred_element_type=F32)
    return acc + jnp.dot(ys_ref[...], w_ref[D_POOL + D_ATTN:, :], preferred_element_type=F32)


def _out_proj_f32w_kernel(yp_ref, ya_ref, ys_ref, w_ref, x_ref, g_ref, wnext_ref,
                          xo_ref, xg_ref, ssq_ref, wnext_bf_ref, wbf_ref):
    @pl.when(pl.program_id(1) == 0)
    def _():
        wbf_ref[...] = w_ref[...].astype(BF16)

    x_new = x_ref[...] + _branch_matmul(yp_ref, ya_ref, ys_ref, wbf_ref)
    xo_ref[...] = x_new
    xg_ref[...] = (x_new * g_ref[...]).astype(xg_ref.dtype)
    sq = x_new * x_new
    ssq_ref[...] = functools.reduce(
        lambda a, b: a + b, [sq[:, c * LANES:(c + 1) * LANES] for c in range(sq.shape[1] // LANES)])
    wnext_bf_ref[...] = wnext_ref[...].astype(BF16)


def _out_proj_f32w(y_pool, y_attn, y_ssm, w_all, layer, x2d, next_g, tm=512, tn=1024):
    m = x2d.shape[0]
    k, n = w_all.shape[1:]
    assert (m // tm) * CAST_ROWS == k
    est = (2 * (tm * k * 2 + tm * tn * (4 + 4 + 2) + tm * LANES * 4 + CAST_ROWS * tn * 6) + k * tn * (4 + 2))
    tile = pl.BlockSpec((tm, tn), lambda j, i: (i, j))
    return pl.pallas_call(
        _out_proj_f32w_kernel,
        grid=(n // tn, m // tm),
        in_specs=[pl.BlockSpec((tm, D_POOL), lambda j, i: (i, 0)),
                  pl.BlockSpec((tm, D_ATTN), lambda j, i: (i, 0)),
                  pl.BlockSpec((tm, D_SSM), lambda j, i: (i, 0)),
                  pl.BlockSpec((None, k, tn), lambda j, i: (layer, 0, j), pipeline_mode=pl.Buffered(1)),
                  tile,
                  pl.BlockSpec((1, tn), lambda j, i: (0, j)),
                  pl.BlockSpec((None, CAST_ROWS, tn), lambda j, i: (layer + 1, i, j))],
        out_specs=(tile, tile,
                   pl.BlockSpec((None, tm, LANES), lambda j, i: (j, i, 0)),
                   pl.BlockSpec((CAST_ROWS, tn), lambda j, i: (i, j))),
        out_shape=(jax.ShapeDtypeStruct((m, n), F32), jax.ShapeDtypeStruct((m, n), BF16),
                   jax.ShapeDtypeStruct((n // tn, m, LANES), F32), jax.ShapeDtypeStruct((k, n), BF16)),
        scratch_shapes=[pltpu.VMEM((k, tn), BF16)],
        compiler_params=pltpu.CompilerParams(
            dimension_semantics=("parallel", "arbitrary"),
            vmem_limit_bytes=_vmem_limit(est)),
        name="out_proj_f32w",
    )(y_pool, y_attn, y_ssm, w_all, x2d, next_g.reshape(1, n), w_all)


def _out_proj_kernel(yp_ref, ya_ref, ys_ref, w_ref, x_ref, o_ref):
    o_ref[...] = x_ref[...] + _branch_matmul(yp_ref, ya_ref, ys_ref, w_ref)


def _out_proj(y_pool, y_attn, y_ssm, w, x2d, tm=1024, tn=512):
    m = x2d.shape[0]
    k, n = w.shape
    est = 2 * (tm * k * 2 + k * tn * 2 + 2 * tm * tn * 4)
    return pl.pallas_call(
        _out_proj_kernel,
        grid=(m // tm, n // tn),
        in_specs=[pl.BlockSpec((tm, D_POOL), lambda i, j: (i, 0)),
                  pl.BlockSpec((tm, D_ATTN), lambda i, j: (i, 0)),
                  pl.BlockSpec((tm, D_SSM), lambda i, j: (i, 0)),
                  pl.BlockSpec((k, tn), lambda i, j: (0, j)),
                  pl.BlockSpec((tm, tn), lambda i, j: (i, j))],
        out_specs=pl.BlockSpec((tm, tn), lambda i, j: (i, j)),
        out_shape=jax.ShapeDtypeStruct((m, n), F32),
        compiler_params=pltpu.CompilerParams(
            dimension_semantics=("parallel", "arbitrary"),
            vmem_limit_bytes=_vmem_limit(est)),
        name="out_proj",
    )(y_pool, y_attn, y_ssm, w, x2d)


def _pool_kernel(px_ref, halo_ref, gate_ref, w_ref, scale_ref, g_ref, o_ref, xs_ref, *, t_blk):
    i = pl.program_id(1)
    x = px_ref[0].astype(F32)
    xs_ref[0:POOL_HALO, :] = jnp.where(i > 0, halo_ref[0].astype(F32), 0.0)
    xs_ref[POOL_HALO:POOL_HALO + t_blk, :] = x
    pos = (i * t_blk + 1 + lax.broadcasted_iota(jnp.int32, (t_blk, 1), 0)).astype(F32)
    mixed = []
    for g, w in enumerate(POOL_WINDOWS):
        cols = slice(g * POOL_GROUP, (g + 1) * POOL_GROUP)
        s = x[:, cols]
        for j in range(1, w):
            s = s + xs_ref[POOL_HALO - j:POOL_HALO - j + t_blk, cols]
        pooled = s / jnp.minimum(pos, float(w)) - x[:, cols]
        mixed.append(jnp.dot(pooled.astype(BF16), w_ref[g], preferred_element_type=F32))
    y = jnp.concatenate(mixed, axis=1) * scale_ref[...]
    y = y * lax.rsqrt(jnp.mean(y * y, axis=-1, keepdims=True) + EPS) * g_ref[...]
    o_ref[0] = (y * _silu(gate_ref[0].astype(F32))).astype(o_ref.dtype)


def _pool_branch(proj3, w_pool, pool_scale, g_pool, t_blk=512):
    b, l, _ = proj3.shape
    halo_per_blk = t_blk // POOL_HALO
    kern = functools.partial(_pool_kernel, t_blk=t_blk)
    return pl.pallas_call(
        kern,
        grid=(b, l // t_blk),
        in_specs=[
            pl.BlockSpec((1, t_blk, D_POOL), lambda bi, i: (bi, i, COL_PX_1024)),
            pl.BlockSpec((1, POOL_HALO, D_POOL),
                         lambda bi, i: (bi, jnp.maximum(i * halo_per_blk - 1, 0), COL_PX_1024)),
            pl.BlockSpec((1, t_blk, D_POOL), lambda bi, i: (bi, i, COL_PGATE_1024)),
            pl.BlockSpec((len(POOL_WINDOWS), POOL_GROUP, POOL_GROUP), lambda bi, i: (0, 0, 0)),
            pl.BlockSpec((1, D_POOL), lambda bi, i: (0, 0)),
            pl.BlockSpec((1, D_POOL), lambda bi, i: (0, 0)),
        ],
        out_specs=pl.BlockSpec((1, t_blk, D_POOL), lambda bi, i: (bi, i, 0)),
        out_shape=jax.ShapeDtypeStruct((b, l, D_POOL), BF16),
        scratch_shapes=[pltpu.VMEM((POOL_HALO + t_blk, D_POOL), F32)],
        compiler_params=pltpu.CompilerParams(
            dimension_semantics=("parallel", "parallel"),
            vmem_limit_bytes=_vmem_limit(8 * t_blk * D_POOL * 4)),
        name="pool_branch",
    )(proj3, proj3, proj3, w_pool, pool_scale.reshape(1, D_POOL), g_pool.reshape(1, D_POOL))


def _attn_kernel(q_ref, k_ref, v_ref, gate_ref, g_ref, o_ref, *, t_blk, n_before):
    n_head = N_HEADS
    i = pl.program_id(1)
    row = lax.broadcasted_iota(jnp.int32, (t_blk, t_blk), 0)
    col = lax.broadcasted_iota(jnp.int32, (t_blk, t_blk), 1)
    from_s = jnp.where(row >= col, 1.0, 0.0).astype(BF16)
    causal = col < row
    qs = [(q_ref[0, :, h * HEAD_DIM:(h + 1) * HEAD_DIM].astype(F32) * (HEAD_DIM ** -0.5 * LOG2_E)).astype(BF16)
          for h in range(n_head)]

    def tile(h, j, carry, diagonal):
        lanes = slice(h * HEAD_DIM, (h + 1) * HEAD_DIM)
        start = pl.multiple_of(j * t_blk, t_blk)
        kj = k_ref[0, pl.ds(start, t_blk), lanes]
        vj = v_ref[0, pl.ds(start, t_blk), lanes]
        z = lax.dot_general(qs[h], kj, (((1,), (1,)), ((), ())), preferred_element_type=F32)
        softplus = jnp.maximum(z, 0.0) + jnp.log(1.0 + jnp.exp2(-jnp.abs(z))) * LOG2_E
        if diagonal:
            softplus = jnp.where(causal, softplus, 0.0)
        tail = jnp.dot(softplus.astype(BF16), from_s, preferred_element_type=F32)
        wts = jnp.exp2(z - tail - carry)
        if diagonal:
            wts = jnp.where(causal, wts, 0.0)
        pv = jnp.dot(wts.astype(BF16), vj, preferred_element_type=F32)
        return pv, carry + tail[:, 0:1]

    accs, carries = [], []
    for h in range(n_head):
        acc, carry = tile(h, i, jnp.zeros((t_blk, 1), F32), True)
        for d in range(1, n_before + 1):
            carry = jnp.where(i >= d, carry, NO_BLOCK_CARRY)
            pv, carry = tile(h, jnp.maximum(i - d, 0), carry, False)
            acc = acc + pv
        accs.append(acc)
        carries.append(carry)

    def alive_of(carries):
        return jnp.min(functools.reduce(jnp.minimum, carries)) < F32_EXP2_ZERO_ABOVE

    def cond(state):
        j, _, _, alive = state
        return jnp.logical_and(j >= 0, alive)

    def body(state):
        j, accs, carries, _ = state
        new_accs, new_carries = [], []
        for h in range(n_head):
            pv, carry = tile(h, j, carries[h], False)
            new_accs.append(accs[h] + pv)
            new_carries.append(carry)
        return j - 1, tuple(new_accs), tuple(new_carries), alive_of(new_carries)

    _, accs, _, _ = lax.while_loop(cond, body,
                                   (i - n_before - 1, tuple(accs), tuple(carries), alive_of(carries)))
    y = jnp.concatenate(accs, axis=1)
    y = y * lax.rsqrt(jnp.mean(y * y, axis=-1, keepdims=True) + EPS) * g_ref[...]
    o_ref[0] = (y * _silu(gate_ref[0].astype(F32))).astype(o_ref.dtype)


def _attn_branch(proj3, g_attn, t_blk=256, n_before=1):
    b, l, _ = proj3.shape
    kern = functools.partial(_attn_kernel, t_blk=t_blk, n_before=n_before)
    est = 2 * l * D_ATTN * 2 + 2 * 3 * t_blk * D_ATTN * 2 + N_HEADS * 12 * t_blk * t_blk * 4
    whole_seq = lambda col: pl.BlockSpec((1, l, D_ATTN), lambda bi, i: (bi, 0, col), pipeline_mode=pl.Buffered(1))
    q_rows = lambda col: pl.BlockSpec((1, t_blk, D_ATTN), lambda bi, i: (bi, i, col))
    return pl.pallas_call(
        kern,
        grid=(b, l // t_blk),
        in_specs=[q_rows(COL_Q_2048), whole_seq(COL_K_2048), whole_seq(COL_V_2048), q_rows(COL_AGATE_2048),
                  pl.BlockSpec((1, D_ATTN), lambda bi, i: (0, 0))],
        out_specs=pl.BlockSpec((1, t_blk, D_ATTN), lambda bi, i: (bi, i, 0)),
        out_shape=jax.ShapeDtypeStruct((b, l, D_ATTN), BF16),
        compiler_params=pltpu.CompilerParams(
            dimension_semantics=("parallel", "arbitrary"),
            vmem_limit_bytes=_vmem_limit(est)),
        name="attn_branch",
    )(proj3, proj3, proj3, proj3, g_attn.reshape(1, D_ATTN))


N_SLAB = 4
SLAB_IN = D_SSM // N_SLAB
SLAB_STATES = 1024
GROUPS_PER_SLAB = N_SSM_GROUPS // N_SLAB
N_CHUNK = SLAB_STATES // LANES


def _discretize_kernel(lam_re_ref, lam_im_ref, log_dt_ref, b_re_ref, b_im_ref,
                       a_re_ref, a_im_ref, wb_re_ref, wb_im_ref):
    lam_re = lam_re_ref[...]
    lam_im = lam_im_ref[...]
    dt = jnp.exp(log_dt_ref[...])
    mag = jnp.exp(lam_re * dt)
    a_re = mag * jnp.cos(lam_im * dt)
    a_im = mag * jnp.sin(lam_im * dt)
    a_re_ref[...] = a_re
    a_im_ref[...] = a_im
    num_re = a_re - 1.0
    den = lam_re * lam_re + lam_im * lam_im
    c_re = ((num_re * lam_re + a_im * lam_im) / den)[:, None, :]
    c_im = ((a_im * lam_re - num_re * lam_im) / den)[:, None, :]
    b_re = b_re_ref[...]
    b_im = b_im_ref[...]
    wb_re_ref[...] = (c_re * b_re - c_im * b_im).astype(wb_re_ref.dtype)
    wb_im_ref[...] = (c_re * b_im + c_im * b_re).astype(wb_im_ref.dtype)


def _block_diag_in(b):
    eye = jnp.eye(GROUPS_PER_SLAB, dtype=b.dtype)
    b4 = b.reshape(N_SLAB, GROUPS_PER_SLAB, SSM_STATE, SSM_GROUP)
    return jnp.einsum('jgpc,gh->jgchp', b4, eye).reshape(N_SLAB, SLAB_IN, SLAB_STATES)


def _block_diag_out(c):
    eye = jnp.eye(GROUPS_PER_SLAB, dtype=c.dtype)
    c4 = c.reshape(N_SLAB, GROUPS_PER_SLAB, SSM_GROUP, SSM_STATE)
    return jnp.einsum('jgcp,gh->jgphc', c4, eye).reshape(N_SLAB, SLAB_STATES, SLAB_IN)


def _discretize(lam_re, lam_im, log_dt, b_re, b_im):
    shape4 = (N_SLAB, SLAB_STATES)
    log_dt_full = jnp.broadcast_to(log_dt[:, None], (N_SSM_GROUPS, SSM_STATE)).reshape(shape4)
    out_shape = (jax.ShapeDtypeStruct(shape4, F32), jax.ShapeDtypeStruct(shape4, F32),
                 jax.ShapeDtypeStruct((N_SLAB, SLAB_IN, SLAB_STATES), BF16),
                 jax.ShapeDtypeStruct((N_SLAB, SLAB_IN, SLAB_STATES), BF16))
    return pl.pallas_call(
        _discretize_kernel, out_shape=out_shape, name="s5_discretize",
        compiler_params=pltpu.CompilerParams(vmem_limit_bytes=_vmem_limit(32 << 20)),
    )(lam_re.reshape(shape4), lam_im.reshape(shape4), log_dt_full,
      _block_diag_in(b_re), _block_diag_in(b_im))


def _s5_kernel(u_ref, gate_ref, a_re_ref, a_im_ref, wb_re_ref, wb_im_ref, wc_re_ref, wc_im_ref,
               dskip_ref, wglu_ref, bglu_ref, g_ref, o_ref,
               s_re_ref, s_im_ref, x_re_ref, x_im_ref, *, t_blk, n_batch):
    n_row = n_batch * N_SLAB
    assert n_row == SUBLANES

    @pl.when(pl.program_id(0) == 0)
    def _():
        x_re_ref[...] = jnp.zeros_like(x_re_ref)
        x_im_ref[...] = jnp.zeros_like(x_im_ref)

    for j in range(N_SLAB):
        cols = slice(j * SLAB_IN, (j + 1) * SLAB_IN)
        uj = jnp.concatenate([u_ref[b, :, cols] for b in range(n_batch)], axis=0)
        bu_re = jnp.dot(uj, wb_re_ref[j], preferred_element_type=F32)
        bu_im = jnp.dot(uj, wb_im_ref[j], preferred_element_type=F32)
        for b in range(n_batch):
            rows = slice(b * t_blk, (b + 1) * t_blk)
            k = b * N_SLAB + j
            for c in range(N_CHUNK):
                lanes = slice(c * LANES, (c + 1) * LANES)
                s_re_ref[c, pl.ds(k, t_blk, stride=n_row), :] = bu_re[rows, lanes]
                s_im_ref[c, pl.ds(k, t_blk, stride=n_row), :] = bu_im[rows, lanes]

    a_re = [a_re_ref[:, c * LANES:(c + 1) * LANES] for c in range(N_CHUNK)]
    a_im = [a_im_ref[:, c * LANES:(c + 1) * LANES] for c in range(N_CHUNK)]
    x_re0 = tuple(x_re_ref[:, c * LANES:(c + 1) * LANES] for c in range(N_CHUNK))
    x_im0 = tuple(x_im_ref[:, c * LANES:(c + 1) * LANES] for c in range(N_CHUNK))

    def step(t, state):
        x_re, x_im = state
        rows = pl.ds(pl.multiple_of(t * n_row, n_row), n_row)
        new_re, new_im = [], []
        for c in range(N_CHUNK):
            nr = a_re[c] * x_re[c] - a_im[c] * x_im[c] + s_re_ref[c, rows, :]
            ni = a_re[c] * x_im[c] + a_im[c] * x_re[c] + s_im_ref[c, rows, :]
            s_re_ref[c, rows, :] = nr
            s_im_ref[c, rows, :] = ni
            new_re.append(nr)
            new_im.append(ni)
        return tuple(new_re), tuple(new_im)

    x_re, x_im = lax.fori_loop(0, t_blk, step, (x_re0, x_im0), unroll=4)
    for c in range(N_CHUNK):
        x_re_ref[:, c * LANES:(c + 1) * LANES] = x_re[c]
        x_im_ref[:, c * LANES:(c + 1) * LANES] = x_im[c]

    def states(s_ref, j):
        return jnp.concatenate(
            [jnp.concatenate([s_ref[c, pl.ds(b * N_SLAB + j, t_blk, stride=n_row), :] for c in range(N_CHUNK)],
                             axis=1) for b in range(n_batch)], axis=0).astype(BF16)

    ys = [jnp.dot(states(s_re_ref, j), wc_re_ref[j], preferred_element_type=F32)
          - jnp.dot(states(s_im_ref, j), wc_im_ref[j], preferred_element_type=F32) for j in range(N_SLAB)]
    u_all = jnp.concatenate([u_ref[b] for b in range(n_batch)], axis=0).astype(F32)
    y = jnp.concatenate(ys, axis=1) + dskip_ref[...] * u_all
    h = jax.nn.gelu(y)
    glu = jnp.dot(h.astype(BF16), wglu_ref[...], preferred_element_type=F32) + bglu_ref[...]
    out = glu[:, :D_SSM] * jax.nn.sigmoid(glu[:, D_SSM:])
    out = out * lax.rsqrt(jnp.mean(out * out, axis=-1, keepdims=True) + EPS) * g_ref[...]
    for b in range(n_batch):
        rows = slice(b * t_blk, (b + 1) * t_blk)
        o_ref[b] = (out[rows] * _silu(gate_ref[b].astype(F32))).astype(o_ref.dtype)


def _s5_branch(proj3, a_re, a_im, wb_re, wb_im, wc_re, wc_im, d_skip, w_glu, b_glu, g_ssm, t_blk=256):
    b, l, _ = proj3.shape
    n_row = b * N_SLAB
    a_re_t = jnp.tile(a_re, (b, 1))
    a_im_t = jnp.tile(a_im, (b, 1))
    kern = functools.partial(_s5_kernel, t_blk=t_blk, n_batch=b)
    weights = 4 * N_SLAB * SLAB_IN * SLAB_STATES * 2 + D_SSM * 2 * D_SSM * 2
    est = (weights + 2 * N_CHUNK * t_blk * n_row * LANES * 4 + 2 * 3 * b * t_blk * D_SSM * 2
           + 6 * b * t_blk * 2 * D_SSM * 4)
    return pl.pallas_call(
        kern,
        grid=(l // t_blk,),
        in_specs=[
            pl.BlockSpec((b, t_blk, D_SSM), lambda i: (0, i, COL_SU_1024)),
            pl.BlockSpec((b, t_blk, D_SSM), lambda i: (0, i, COL_SGATE_1024)),
            _resident(n_row, SLAB_STATES), _resident(n_row, SLAB_STATES),
            _resident(N_SLAB, SLAB_IN, SLAB_STATES), _resident(N_SLAB, SLAB_IN, SLAB_STATES),
            _resident(N_SLAB, SLAB_STATES, SLAB_IN), _resident(N_SLAB, SLAB_STATES, SLAB_IN),
            _resident(1, D_SSM), _resident(D_SSM, 2 * D_SSM), _resident(1, 2 * D_SSM), _resident(1, D_SSM),
        ],
        out_specs=pl.BlockSpec((b, t_blk, D_SSM), lambda i: (0, i, 0)),
        out_shape=jax.ShapeDtypeStruct((b, l, D_SSM), BF16),
        scratch_shapes=[pltpu.VMEM((N_CHUNK, t_blk * n_row, LANES), F32),
                        pltpu.VMEM((N_CHUNK, t_blk * n_row, LANES), F32),
                        pltpu.VMEM((n_row, SLAB_STATES), F32),
                        pltpu.VMEM((n_row, SLAB_STATES), F32)],
        compiler_params=pltpu.CompilerParams(
            dimension_semantics=("arbitrary",),
            vmem_limit_bytes=_vmem_limit(est)),
        name="s5_branch",
    )(proj3, proj3, a_re_t, a_im_t, wb_re, wb_im, wc_re, wc_im,
      d_skip.reshape(1, D_SSM), w_glu, b_glu.reshape(1, 2 * D_SSM), g_ssm.reshape(1, D_SSM))


def _mixers(proj, bsz, w_pool, pool_scale, lam_re, lam_im, log_dt,
            b_re, b_im, c_re, c_im, d_skip, w_glu, b_glu, branch_g):
    m = proj.shape[0]
    proj3 = proj.reshape(bsz, m // bsz, D_IN)
    g_pool = branch_g[:D_POOL]
    g_attn = branch_g[D_POOL:D_POOL + D_ATTN]
    g_ssm = branch_g[D_POOL + D_ATTN:]
    y_pool = _pool_branch(proj3, w_pool.astype(BF16), pool_scale, g_pool)
    y_attn = _attn_branch(proj3, g_attn)
    a_re, a_im, wb_re, wb_im = _discretize(lam_re, lam_im, log_dt, b_re, b_im)
    y_ssm = _s5_branch(proj3, a_re, a_im, wb_re, wb_im,
                       _block_diag_out(c_re).astype(BF16), _block_diag_out(c_im).astype(BF16),
                       d_skip, w_glu.astype(BF16), b_glu, g_ssm)
    return y_pool.reshape(m, D_POOL), y_attn.reshape(m, D_ATTN), y_ssm.reshape(m, D_SSM)


def kernel(x, ln_g, w_in, w_pool, pool_scale, lam_re, lam_im, log_dt, b_re, b_im, c_re, c_im,
           d_skip, w_glu, b_glu, branch_g, w_out, final_g):
    bsz, l, d = x.shape
    assert ln_g.shape[0] == 2
    mixer_params = (w_pool, pool_scale, lam_re, lam_im, log_dt, b_re, b_im, c_re, c_im,
                    d_skip, w_glu, b_glu, branch_g)
    x0 = x.reshape(bsz * l, d)

    h = _rmsnorm(x0, ln_g[0], BF16)
    proj, w_in1 = _in_proj_f32w(h, w_in, 0)
    ys = _mixers(proj, bsz, *[p[0] for p in mixer_params])
    x1, xg, ssq, w_out1 = _out_proj_f32w(*ys, w_out, 0, x0, ln_g[1])

    proj = _in_proj_scaled(xg, ssq, w_in1)
    ys = _mixers(proj, bsz, *[p[1] for p in mixer_params])
    x2 = _out_proj(*ys, w_out1, x1)
    return _rmsnorm(x2, final_g, x.dtype).reshape(bsz, l, d)
```

```python
import functools
import math

import jax
import jax.numpy as jnp
from jax import lax
from jax.experimental import pallas as pl
from jax.experimental.pallas import tpu as pltpu

D_MODEL = 4096
D_POOL = 1024
D_ATTN = 2048
D_SSM = 1024
POOL_WINDOWS = (2, 4, 8, 16)
POOL_GROUP = 256
POOL_HALO = 16
HEAD_DIM = 128
N_HEADS = 16
SSM_GROUP = 16
SSM_STATE = 64
N_SSM_GROUPS = 64
D_IN = 12288
EPS = 1e-6

COL_PX_1024 = 0
COL_PGATE_1024 = 1
COL_Q_2048 = 1
COL_K_2048 = 2
COL_V_2048 = 3
COL_AGATE_2048 = 4
COL_SU_1024 = 10
COL_SGATE_1024 = 11

V7X_VMEM_BYTES = 64 * 1024 * 1024
SUBLANES = 8
LANES = 128

LOG2_E = 1.0 / math.log(2.0)
F32_EXP2_ZERO_ABOVE = 150.0
NO_BLOCK_CARRY = 1e30

BF16 = jnp.bfloat16
F32 = jnp.float32


def _vmem_limit(nbytes):
    return int(min(nbytes * 3 // 2 + (4 << 20), V7X_VMEM_BYTES - (6 << 20)))


def _silu(x):
    return x * jax.nn.sigmoid(x)


def _resident(*shape):
    return pl.BlockSpec(shape, lambda *_: (0,) * len(shape), pipeline_mode=pl.Buffered(1))


def _rmsnorm_kernel(x_ref, g_ref, o_ref):
    x = x_ref[...]
    y = x * lax.rsqrt(jnp.mean(x * x, axis=-1, keepdims=True) + EPS)
    o_ref[...] = (y * g_ref[...]).astype(o_ref.dtype)


def _rmsnorm(x2d, g, out_dtype, tm=512):
    m, d = x2d.shape
    return pl.pallas_call(
        _rmsnorm_kernel,
        grid=(m // tm,),
        in_specs=[pl.BlockSpec((tm, d), lambda i: (i, 0)),
                  pl.BlockSpec((1, d), lambda i: (0, 0))],
        out_specs=pl.BlockSpec((tm, d), lambda i: (i, 0)),
        out_shape=jax.ShapeDtypeStruct((m, d), out_dtype),
        compiler_params=pltpu.CompilerParams(
            dimension_semantics=("parallel",),
            vmem_limit_bytes=_vmem_limit(2 * tm * d * 8)),
        name="rmsnorm",
    )(x2d, g.reshape(1, d))


CAST_ROWS = 256


def _in_proj_f32w_kernel(a_ref, w_ref, wnext_ref, o_ref, wnext_bf_ref, wbf_ref):
    @pl.when(pl.program_id(1) == 0)
    def _():
        wbf_ref[...] = w_ref[...].astype(BF16)

    o_ref[...] = jnp.dot(a_ref[...], wbf_ref[...], preferred_element_type=F32).astype(o_ref.dtype)
    wnext_bf_ref[...] = wnext_ref[...].astype(BF16)


def _in_proj_f32w(h, w_all, layer, tm=512, tn=1024):
    m, k = h.shape
    n = w_all.shape[2]
    assert (m // tm) * CAST_ROWS == k
    est = 2 * (tm * k * 2 + k * tn * 4 + tm * tn * 2 + CAST_ROWS * tn * 6) + k * tn * 2
    return pl.pallas_call(
        _in_proj_f32w_kernel,
        grid=(n // tn, m // tm),
        in_specs=[pl.BlockSpec((tm, k), lambda j, i: (i, 0)),
                  pl.BlockSpec((None, k, tn), lambda j, i: (layer, 0, j)),
                  pl.BlockSpec((None, CAST_ROWS, tn), lambda j, i: (layer + 1, i, j))],
        out_specs=(pl.BlockSpec((tm, tn), lambda j, i: (i, j)),
                   pl.BlockSpec((CAST_ROWS, tn), lambda j, i: (i, j))),
        out_shape=(jax.ShapeDtypeStruct((m, n), BF16), jax.ShapeDtypeStruct((k, n), BF16)),
        scratch_shapes=[pltpu.VMEM((k, tn), BF16)],
        compiler_params=pltpu.CompilerParams(
            dimension_semantics=("parallel", "arbitrary"),
            vmem_limit_bytes=_vmem_limit(est)),
        name="in_proj_f32w",
    )(h, w_all, w_all)


def _in_proj_scaled_kernel(a_ref, ssq_ref, w_ref, o_ref, r_ref):
    @pl.when(pl.program_id(1) == 0)
    def _():
        ssq = jnp.sum(jnp.sum(ssq_ref[...], axis=0), axis=-1, keepdims=True)
        r_ref[...] = lax.rsqrt(ssq / D_MODEL + EPS)

    acc = jnp.dot(a_ref[...], w_ref[...], preferred_element_type=F32)
    o_ref[...] = (acc * r_ref[...]).astype(o_ref.dtype)


def _in_proj_scaled(xg, ssq, w, tm=1024, tn=1024):
    m, k = xg.shape
    n = w.shape[1]
    parts = ssq.shape[0]
    est = 2 * (tm * k * 2 + k * tn * 2 + tm * tn * 2 + parts * tm * LANES * 4) + tm * LANES * 4
    return pl.pallas_call(
        _in_proj_scaled_kernel,
        grid=(m // tm, n // tn),
        in_specs=[pl.BlockSpec((tm, k), lambda i, j: (i, 0)),
                  pl.BlockSpec((parts, tm, LANES), lambda i, j: (0, i, 0)),
                  pl.BlockSpec((k, tn), lambda i, j: (0, j))],
        out_specs=pl.BlockSpec((tm, tn), lambda i, j: (i, j)),
        out_shape=jax.ShapeDtypeStruct((m, n), BF16),
        scratch_shapes=[pltpu.VMEM((tm, 1), F32)],
        compiler_params=pltpu.CompilerParams(
            dimension_semantics=("parallel", "arbitrary"),
            vmem_limit_bytes=_vmem_limit(est)),
        name="in_proj_scaled",
    )(xg, ssq, w)


def _branch_matmul(yp_ref, ya_ref, ys_ref, w_ref):
    acc = jnp.dot(yp_ref[...], w_ref[0:D_POOL, :], preferred_element_type=F32)
    acc += jnp.dot(ya_ref[...], w_ref[D_POOL:D_POOL + D_ATTN, :], preferred_element_type=F32)
    return acc + jnp.dot(ys_ref[...], w_ref[D_POOL + D_ATTN:, :], preferred_element_type=F32)


def _out_proj_f32w_kernel(yp_ref, ya_ref, ys_ref, w_ref, x_ref, g_ref, wnext_ref,
                          xo_ref, xg_ref, ssq_ref, wnext_bf_ref, wbf_ref):
    @pl.when(pl.program_id(1) == 0)
    def _():
        wbf_ref[...] = w_ref[...].astype(BF16)

    x_new = x_ref[...] + _branch_matmul(yp_ref, ya_ref, ys_ref, wbf_ref)
    xo_ref[...] = x_new
    xg_ref[...] = (x_new * g_ref[...]).astype(xg_ref.dtype)
    sq = x_new * x_new
    ssq_ref[...] = functools.reduce(
        lambda a, b: a + b, [sq[:, c * LANES:(c + 1) * LANES] for c in range(sq.shape[1] // LANES)])
    wnext_bf_ref[...] = wnext_ref[...].astype(BF16)


def _out_proj_f32w(y_pool, y_attn, y_ssm, w_all, layer, x2d, next_g, tm=512, tn=1024):
    m = x2d.shape[0]
    k, n = w_all.shape[1:]
    assert (m // tm) * CAST_ROWS == k
    est = (2 * (tm * k * 2 + tm * tn * (4 + 4 + 2) + tm * LANES * 4 + CAST_ROWS * tn * 6) + k * tn * (4 + 2))
    tile = pl.BlockSpec((tm, tn), lambda j, i: (i, j))
    return pl.pallas_call(
        _out_proj_f32w_kernel,
        grid=(n // tn, m // tm),
        in_specs=[pl.BlockSpec((tm, D_POOL), lambda j, i: (i, 0)),
                  pl.BlockSpec((tm, D_ATTN), lambda j, i: (i, 0)),
                  pl.BlockSpec((tm, D_SSM), lambda j, i: (i, 0)),
                  pl.BlockSpec((None, k, tn), lambda j, i: (layer, 0, j), pipeline_mode=pl.Buffered(1)),
                  tile,
                  pl.BlockSpec((1, tn), lambda j, i: (0, j)),
                  pl.BlockSpec((None, CAST_ROWS, tn), lambda j, i: (layer + 1, i, j))],
        out_specs=(tile, tile,
                   pl.BlockSpec((None, tm, LANES), lambda j, i: (j, i, 0)),
                   pl.BlockSpec((CAST_ROWS, tn), lambda j, i: (i, j))),
        out_shape=(jax.ShapeDtypeStruct((m, n), F32), jax.ShapeDtypeStruct((m, n), BF16),
                   jax.ShapeDtypeStruct((n // tn, m, LANES), F32), jax.ShapeDtypeStruct((k, n), BF16)),
        scratch_shapes=[pltpu.VMEM((k, tn), BF16)],
        compiler_params=pltpu.CompilerParams(
            dimension_semantics=("parallel", "arbitrary"),
            vmem_limit_bytes=_vmem_limit(est)),
        name="out_proj_f32w",
    )(y_pool, y_attn, y_ssm, w_all, x2d, next_g.reshape(1, n), w_all)


def _out_proj_kernel(yp_ref, ya_ref, ys_ref, w_ref, x_ref, o_ref):
    o_ref[...] = x_ref[...] + _branch_matmul(yp_ref, ya_ref, ys_ref, w_ref)


def _out_proj(y_pool, y_attn, y_ssm, w, x2d, tm=1024, tn=512):
    m = x2d.shape[0]
    k, n = w.shape
    est = 2 * (tm * k * 2 + k * tn * 2 + 2 * tm * tn * 4)
    return pl.pallas_call(
        _out_proj_kernel,
        grid=(m // tm, n // tn),
        in_specs=[pl.BlockSpec((tm, D_POOL), lambda i, j: (i, 0)),
                  pl.BlockSpec((tm, D_ATTN), lambda i, j: (i, 0)),
                  pl.BlockSpec((tm, D_SSM), lambda i, j: (i, 0)),
                  pl.BlockSpec((k, tn), lambda i, j: (0, j)),
                  pl.BlockSpec((tm, tn), lambda i, j: (i, j))],
        out_specs=pl.BlockSpec((tm, tn), lambda i, j: (i, j)),
        out_shape=jax.ShapeDtypeStruct((m, n), F32),
        compiler_params=pltpu.CompilerParams(
            dimension_semantics=("parallel", "arbitrary"),
            vmem_limit_bytes=_vmem_limit(est)),
        name="out_proj",
    )(y_pool, y_attn, y_ssm, w, x2d)


def _pool_kernel(px_ref, halo_ref, gate_ref, w_ref, scale_ref, g_ref, o_ref, xs_ref, *, t_blk):
    i = pl.program_id(1)
    x = px_ref[0].astype(F32)
    xs_ref[0:POOL_HALO, :] = jnp.where(i > 0, halo_ref[0].astype(F32), 0.0)
    xs_ref[POOL_HALO:POOL_HALO + t_blk, :] = x
    pos = (i * t_blk + 1 + lax.broadcasted_iota(jnp.int32, (t_blk, 1), 0)).astype(F32)
    mixed = []
    for g, w in enumerate(POOL_WINDOWS):
        cols = slice(g * POOL_GROUP, (g + 1) * POOL_GROUP)
        s = x[:, cols]
        for j in range(1, w):
            s = s + xs_ref[POOL_HALO - j:POOL_HALO - j + t_blk, cols]
        pooled = s / jnp.minimum(pos, float(w)) - x[:, cols]
        mixed.append(jnp.dot(pooled.astype(BF16), w_ref[g], preferred_element_type=F32))
    y = jnp.concatenate(mixed, axis=1) * scale_ref[...]
    y = y * lax.rsqrt(jnp.mean(y * y, axis=-1, keepdims=True) + EPS) * g_ref[...]
    o_ref[0] = (y * _silu(gate_ref[0].astype(F32))).astype(o_ref.dtype)


def _pool_branch(proj3, w_pool, pool_scale, g_pool, t_blk=512):
    b, l, _ = proj3.shape
    halo_per_blk = t_blk // POOL_HALO
    kern = functools.partial(_pool_kernel, t_blk=t_blk)
    return pl.pallas_call(
        kern,
        grid=(b, l // t_blk),
        in_specs=[
            pl.BlockSpec((1, t_blk, D_POOL), lambda bi, i: (bi, i, COL_PX_1024)),
            pl.BlockSpec((1, POOL_HALO, D_POOL),
                         lambda bi, i: (bi, jnp.maximum(i * halo_per_blk - 1, 0), COL_PX_1024)),
            pl.BlockSpec((1, t_blk, D_POOL), lambda bi, i: (bi, i, COL_PGATE_1024)),
            pl.BlockSpec((len(POOL_WINDOWS), POOL_GROUP, POOL_GROUP), lambda bi, i: (0, 0, 0)),
            pl.BlockSpec((1, D_POOL), lambda bi, i: (0, 0)),
            pl.BlockSpec((1, D_POOL), lambda bi, i: (0, 0)),
        ],
        out_specs=pl.BlockSpec((1, t_blk, D_POOL), lambda bi, i: (bi, i, 0)),
        out_shape=jax.ShapeDtypeStruct((b, l, D_POOL), BF16),
        scratch_shapes=[pltpu.VMEM((POOL_HALO + t_blk, D_POOL), F32)],
        compiler_params=pltpu.CompilerParams(
            dimension_semantics=("parallel", "parallel"),
            vmem_limit_bytes=_vmem_limit(8 * t_blk * D_POOL * 4)),
        name="pool_branch",
    )(proj3, proj3, proj3, w_pool, pool_scale.reshape(1, D_POOL), g_pool.reshape(1, D_POOL))


def _attn_kernel(q_ref, k_ref, v_ref, gate_ref, g_ref, o_ref, *, t_blk, n_before):
    n_head = N_HEADS
    i = pl.program_id(1)
    row = lax.broadcasted_iota(jnp.int32, (t_blk, t_blk), 0)
    col = lax.broadcasted_iota(jnp.int32, (t_blk, t_blk), 1)
    from_s = jnp.where(row >= col, 1.0, 0.0).astype(BF16)
    causal = col < row
    qs = [(q_ref[0, :, h * HEAD_DIM:(h + 1) * HEAD_DIM].astype(F32) * (HEAD_DIM ** -0.5 * LOG2_E)).astype(BF16)
          for h in range(n_head)]

    def tile(h, j, carry, diagonal):
        lanes = slice(h * HEAD_DIM, (h + 1) * HEAD_DIM)
        start = pl.multiple_of(j * t_blk, t_blk)
        kj = k_ref[0, pl.ds(start, t_blk), lanes]
        vj = v_ref[0, pl.ds(start, t_blk), lanes]
        z = lax.dot_general(qs[h], kj, (((1,), (1,)), ((), ())), preferred_element_type=F32)
        softplus = jnp.maximum(z, 0.0) + jnp.log(1.0 + jnp.exp2(-jnp.abs(z))) * LOG2_E
        if diagonal:
            softplus = jnp.where(causal, softplus, 0.0)
        tail = jnp.dot(softplus.astype(BF16), from_s, preferred_element_type=F32)
        wts = jnp.exp2(z - tail - carry)
        if diagonal:
            wts = jnp.where(causal, wts, 0.0)
        pv = jnp.dot(wts.astype(BF16), vj, preferred_element_type=F32)
        return pv, carry + tail[:, 0:1]

    accs, carries = [], []
    for h in range(n_head):
        acc, carry = tile(h, i, jnp.zeros((t_blk, 1), F32), True)
        for d in range(1, n_before + 1):
            carry = jnp.where(i >= d, carry, NO_BLOCK_CARRY)
            pv, carry = tile(h, jnp.maximum(i - d, 0), carry, False)
            acc = acc + pv
        accs.append(acc)
        carries.append(carry)

    def alive_of(carries):
        return jnp.min(functools.reduce(jnp.minimum, carries)) < F32_EXP2_ZERO_ABOVE

    def cond(state):
        j, _, _, alive = state
        return jnp.logical_and(j >= 0, alive)

    def body(state):
        j, accs, carries, _ = state
        new_accs, new_carries = [], []
        for h in range(n_head):
            pv, carry = tile(h, j, carries[h], False)
            new_accs.append(accs[h] + pv)
            new_carries.append(carry)
        return j - 1, tuple(new_accs), tuple(new_carries), alive_of(new_carries)

    _, accs, _, _ = lax.while_loop(cond, body,
                                   (i - n_before - 1, tuple(accs), tuple(carries), alive_of(carries)))
    y = jnp.concatenate(accs, axis=1)
    y = y * lax.rsqrt(jnp.mean(y * y, axis=-1, keepdims=True) + EPS) * g_ref[...]
    o_ref[0] = (y * _silu(gate_ref[0].astype(F32))).astype(o_ref.dtype)


def _attn_branch(proj3, g_attn, t_blk=256, n_before=1):
    b, l, _ = proj3.shape
    kern = functools.partial(_attn_kernel, t_blk=t_blk, n_before=n_before)
    est = 2 * l * D_ATTN * 2 + 2 * 3 * t_blk * D_ATTN * 2 + N_HEADS * 12 * t_blk * t_blk * 4
    whole_seq = lambda col: pl.BlockSpec((1, l, D_ATTN), lambda bi, i: (bi, 0, col), pipeline_mode=pl.Buffered(1))
    q_rows = lambda col: pl.BlockSpec((1, t_blk, D_ATTN), lambda bi, i: (bi, i, col))
    return pl.pallas_call(
        kern,
        grid=(b, l // t_blk),
        in_specs=[q_rows(COL_Q_2048), whole_seq(COL_K_2048), whole_seq(COL_V_2048), q_rows(COL_AGATE_2048),
                  pl.BlockSpec((1, D_ATTN), lambda bi, i: (0, 0))],
        out_specs=pl.BlockSpec((1, t_blk, D_ATTN), lambda bi, i: (bi, i, 0)),
        out_shape=jax.ShapeDtypeStruct((b, l, D_ATTN), BF16),
        compiler_params=pltpu.CompilerParams(
            dimension_semantics=("parallel", "arbitrary"),
            vmem_limit_bytes=_vmem_limit(est)),
        name="attn_branch",
    )(proj3, proj3, proj3, proj3, g_attn.reshape(1, D_ATTN))


N_SLAB = 4
SLAB_IN = D_SSM // N_SLAB
SLAB_STATES = 1024
GROUPS_PER_SLAB = N_SSM_GROUPS // N_SLAB
N_CHUNK = SLAB_STATES // LANES
TILE_PITCH = SUBLANES + 1


def _discretize_kernel(lam_re_ref, lam_im_ref, log_dt_ref, b_re_ref, b_im_ref,
                       a_re_ref, a_im_ref, wb_re_ref, wb_im_ref):
    lam_re = lam_re_ref[...]
    lam_im = lam_im_ref[...]
    dt = jnp.exp(log_dt_ref[...])
    mag = jnp.exp(lam_re * dt)
    a_re = mag * jnp.cos(lam_im * dt)
    a_im = mag * jnp.sin(lam_im * dt)
    a_re_ref[...] = a_re
    a_im_ref[...] = a_im
    num_re = a_re - 1.0
    den = lam_re * lam_re + lam_im * lam_im
    c_re = ((num_re * lam_re + a_im * lam_im) / den)[:, None, :]
    c_im = ((a_im * lam_re - num_re * lam_im) / den)[:, None, :]
    b_re = b_re_ref[...]
    b_im = b_im_ref[...]
    wb_re_ref[...] = (c_re * b_re - c_im * b_im).astype(wb_re_ref.dtype)
    wb_im_ref[...] = (c_re * b_im + c_im * b_re).astype(wb_im_ref.dtype)


def _block_diag_in(b):
    b4 = b.reshape(N_SLAB, GROUPS_PER_SLAB, SSM_STATE, SSM_GROUP).transpose(0, 1, 3, 2)
    on_diag = jnp.eye(GROUPS_PER_SLAB, dtype=bool)[None, :, None, :, None]
    return jnp.where(on_diag, b4[:, :, :, None, :], 0).reshape(N_SLAB, SLAB_IN, SLAB_STATES)


def _block_diag_out(c):
    c4 = c.reshape(N_SLAB, GROUPS_PER_SLAB, SSM_GROUP, SSM_STATE).transpose(0, 1, 3, 2)
    on_diag = jnp.eye(GROUPS_PER_SLAB, dtype=bool)[None, :, None, :, None]
    return jnp.where(on_diag, c4[:, :, :, None, :], 0).reshape(N_SLAB, SLAB_STATES, SLAB_IN)


def _discretize(lam_re, lam_im, log_dt, b_re, b_im):
    shape4 = (N_SLAB, SLAB_STATES)
    log_dt_full = jnp.broadcast_to(log_dt[:, None], (N_SSM_GROUPS, SSM_STATE)).reshape(shape4)
    out_shape = (jax.ShapeDtypeStruct(shape4, F32), jax.ShapeDtypeStruct(shape4, F32),
                 jax.ShapeDtypeStruct((N_SLAB, SLAB_IN, SLAB_STATES), BF16),
                 jax.ShapeDtypeStruct((N_SLAB, SLAB_IN, SLAB_STATES), BF16))
    return pl.pallas_call(
        _discretize_kernel, out_shape=out_shape, name="s5_discretize",
        compiler_params=pltpu.CompilerParams(vmem_limit_bytes=_vmem_limit(32 << 20)),
    )(lam_re.reshape(shape4), lam_im.reshape(shape4), log_dt_full,
      _block_diag_in(b_re), _block_diag_in(b_im))


def _s5_kernel(u_ref, gate_ref, a_re_ref, a_im_ref, wb_re_ref, wb_im_ref, wc_re_ref, wc_im_ref,
               dskip_ref, wglu_ref, bglu_ref, g_ref, o_ref,
               s_re_ref, s_im_ref, x_re_ref, x_im_ref, *, t_blk, n_batch):
    n_row = n_batch * N_SLAB
    assert n_row == SUBLANES

    @pl.when(pl.program_id(0) == 0)
    def _():
        x_re_ref[...] = jnp.zeros_like(x_re_ref)
        x_im_ref[...] = jnp.zeros_like(x_im_ref)

    for j in range(N_SLAB):
        cols = slice(j * SLAB_IN, (j + 1) * SLAB_IN)
        uj = jnp.concatenate([u_ref[b, :, cols] for b in range(n_batch)], axis=0)
        bu_re = jnp.dot(uj, wb_re_ref[j], preferred_element_type=F32)
        bu_im = jnp.dot(uj, wb_im_ref[j], preferred_element_type=F32)
        for b in range(n_batch):
            rows = slice(b * t_blk, (b + 1) * t_blk)
            k = b * N_SLAB + j
            for c in range(N_CHUNK):
                lanes = slice(c * LANES, (c + 1) * LANES)
                s_re_ref[c, pl.ds(k, t_blk, stride=TILE_PITCH), :] = bu_re[rows, lanes]
                s_im_ref[c, pl.ds(k, t_blk, stride=TILE_PITCH), :] = bu_im[rows, lanes]

    a_re = [a_re_ref[:, c * LANES:(c + 1) * LANES] for c in range(N_CHUNK)]
    a_im = [a_im_ref[:, c * LANES:(c + 1) * LANES] for c in range(N_CHUNK)]
    x_re0 = tuple(x_re_ref[:, c * LANES:(c + 1) * LANES] for c in range(N_CHUNK))
    x_im0 = tuple(x_im_ref[:, c * LANES:(c + 1) * LANES] for c in range(N_CHUNK))

    def step(t, state):
        x_re, x_im = state
        rows = pl.ds(t * TILE_PITCH, n_row)
        new_re, new_im = [], []
        for c in range(N_CHUNK):
            nr = a_re[c] * x_re[c] - a_im[c] * x_im[c] + s_re_ref[c, rows, :]
            ni = a_re[c] * x_im[c] + a_im[c] * x_re[c] + s_im_ref[c, rows, :]
            s_re_ref[c, rows, :] = nr
            s_im_ref[c, rows, :] = ni
            new_re.append(nr)
            new_im.append(ni)
        return tuple(new_re), tuple(new_im)

    x_re, x_im = lax.fori_loop(0, t_blk, step, (x_re0, x_im0), unroll=4)
    for c in range(N_CHUNK):
        x_re_ref[:, c * LANES:(c + 1) * LANES] = x_re[c]
        x_im_ref[:, c * LANES:(c + 1) * LANES] = x_im[c]

    def states(s_ref, j):
        return jnp.concatenate(
            [jnp.concatenate([s_ref[c, pl.ds(b * N_SLAB + j, t_blk, stride=TILE_PITCH), :] for c in range(N_CHUNK)],
                             axis=1) for b in range(n_batch)], axis=0).astype(BF16)

    ys = [jnp.dot(states(s_re_ref, j), wc_re_ref[j], preferred_element_type=F32)
          - jnp.dot(states(s_im_ref, j), wc_im_ref[j], preferred_element_type=F32) for j in range(N_SLAB)]
    u_all = jnp.concatenate([u_ref[b] for b in range(n_batch)], axis=0).astype(F32)
    y = jnp.concatenate(ys, axis=1) + dskip_ref[...] * u_all
    h = jax.nn.gelu(y)
    glu = jnp.dot(h.astype(BF16), wglu_ref[...], preferred_element_type=F32) + bglu_ref[...]
    out = glu[:, :D_SSM] * jax.nn.sigmoid(glu[:, D_SSM:])
    out = out * lax.rsqrt(jnp.mean(out * out, axis=-1, keepdims=True) + EPS) * g_ref[...]
    for b in range(n_batch):
        rows = slice(b * t_blk, (b + 1) * t_blk)
        o_ref[b] = (out[rows] * _silu(gate_ref[b].astype(F32))).astype(o_ref.dtype)


def _s5_branch(proj3, a_re, a_im, wb_re, wb_im, wc_re, wc_im, d_skip, w_glu, b_glu, g_ssm, t_blk=256):
    b, l, _ = proj3.shape
    n_row = b * N_SLAB
    a_re_t = jnp.tile(a_re, (b, 1))
    a_im_t = jnp.tile(a_im, (b, 1))
    kern = functools.partial(_s5_kernel, t_blk=t_blk, n_batch=b)
    weights = 4 * N_SLAB * SLAB_IN * SLAB_STATES * 2 + D_SSM * 2 * D_SSM * 2
    est = (weights + 2 * N_CHUNK * t_blk * TILE_PITCH * LANES * 4 + 2 * 3 * b * t_blk * D_SSM * 2
           + 6 * b * t_blk * 2 * D_SSM * 4)
    return pl.pallas_call(
        kern,
        grid=(l // t_blk,),
        in_specs=[
            pl.BlockSpec((b, t_blk, D_SSM), lambda i: (0, i, COL_SU_1024)),
            pl.BlockSpec((b, t_blk, D_SSM), lambda i: (0, i, COL_SGATE_1024)),
            _resident(n_row, SLAB_STATES), _resident(n_row, SLAB_STATES),
            _resident(N_SLAB, SLAB_IN, SLAB_STATES), _resident(N_SLAB, SLAB_IN, SLAB_STATES),
            _resident(N_SLAB, SLAB_STATES, SLAB_IN), _resident(N_SLAB, SLAB_STATES, SLAB_IN),
            _resident(1, D_SSM), _resident(D_SSM, 2 * D_SSM), _resident(1, 2 * D_SSM), _resident(1, D_SSM),
        ],
        out_specs=pl.BlockSpec((b, t_blk, D_SSM), lambda i: (0, i, 0)),
        out_shape=jax.ShapeDtypeStruct((b, l, D_SSM), BF16),
        scratch_shapes=[pltpu.VMEM((N_CHUNK, t_blk * TILE_PITCH, LANES), F32),
                        pltpu.VMEM((N_CHUNK, t_blk * TILE_PITCH, LANES), F32),
                        pltpu.VMEM((n_row, SLAB_STATES), F32),
                        pltpu.VMEM((n_row, SLAB_STATES), F32)],
        compiler_params=pltpu.CompilerParams(
            dimension_semantics=("arbitrary",),
            vmem_limit_bytes=_vmem_limit(est)),
        name="s5_branch",
    )(proj3, proj3, a_re_t, a_im_t, wb_re, wb_im, wc_re, wc_im,
      d_skip.reshape(1, D_SSM), w_glu, b_glu.reshape(1, 2 * D_SSM), g_ssm.reshape(1, D_SSM))


def _mixers(proj, bsz, w_pool, pool_scale, lam_re, lam_im, log_dt,
            b_re, b_im, c_re, c_im, d_skip, w_glu, b_glu, branch_g):
    m = proj.shape[0]
    proj3 = proj.reshape(bsz, m // bsz, D_IN)
    g_pool = branch_g[:D_POOL]
    g_attn = branch_g[D_POOL:D_POOL + D_ATTN]
    g_ssm = branch_g[D_POOL + D_ATTN:]
    y_pool = _pool_branch(proj3, w_pool.astype(BF16), pool_scale, g_pool)
    y_attn = _attn_branch(proj3, g_attn)
    a_re, a_im, wb_re, wb_im = _discretize(lam_re, lam_im, log_dt, b_re, b_im)
    y_ssm = _s5_branch(proj3, a_re, a_im, wb_re, wb_im,
                       _block_diag_out(c_re).astype(BF16), _block_diag_out(c_im).astype(BF16),
                       d_skip, w_glu.astype(BF16), b_glu, g_ssm)
    return y_pool.reshape(m, D_POOL), y_attn.reshape(m, D_ATTN), y_ssm.reshape(m, D_SSM)


def kernel(x, ln_g, w_in, w_pool, pool_scale, lam_re, lam_im, log_dt, b_re, b_im, c_re, c_im,
           d_skip, w_glu, b_glu, branch_g, w_out, final_g):
    bsz, l, d = x.shape
    assert ln_g.shape[0] == 2
    mixer_params = (w_pool, pool_scale, lam_re, lam_im, log_dt, b_re, b_im, c_re, c_im,
                    d_skip, w_glu, b_glu, branch_g)
    x0 = x.reshape(bsz * l, d)

    h = _rmsnorm(x0, ln_g[0], BF16)
    proj, w_in1 = _in_proj_f32w(h, w_in, 0)
    ys = _mixers(proj, bsz, *[p[0] for p in mixer_params])
    x1, xg, ssq, w_out1 = _out_proj_f32w(*ys, w_out, 0, x0, ln_g[1])

    proj = _in_proj_scaled(xg, ssq, w_in1)
    ys = _mixers(proj, bsz, *[p[1] for p in mixer_params])
    x2 = _out_proj(*ys, w_out1, x1)
    return _rmsnorm(x2, final_g, x.dtype).reshape(bsz, l, d)
```

```python
import functools
import math

import jax
import jax.numpy as jnp
from jax import lax
from jax.experimental import pallas as pl
from jax.experimental.pallas import tpu as pltpu

D_MODEL = 4096
D_POOL = 1024
D_ATTN = 2048
D_SSM = 1024
POOL_WINDOWS = (2, 4, 8, 16)
POOL_GROUP = 256
POOL_HALO = 16
HEAD_DIM = 128
N_HEADS = 16
SSM_GROUP = 16
SSM_STATE = 64
N_SSM_GROUPS = 64
D_IN = 12288
EPS = 1e-6

COL_PX_1024 = 0
COL_PGATE_1024 = 1
COL_Q_2048 = 1
COL_K_2048 = 2
COL_V_2048 = 3
COL_AGATE_2048 = 4
COL_SU_1024 = 10
COL_SGATE_1024 = 11

V7X_VMEM_BYTES = 64 * 1024 * 1024
SUBLANES = 8
LANES = 128

LOG2_E = 1.0 / math.log(2.0)
F32_EXP2_ZERO_ABOVE = 150.0
NO_BLOCK_CARRY = 1e30

BF16 = jnp.bfloat16
F32 = jnp.float32


def _vmem_limit(nbytes):
    return int(min(nbytes * 3 // 2 + (4 << 20), V7X_VMEM_BYTES - (6 << 20)))


def _silu(x):
    return x * jax.nn.sigmoid(x)


def _resident(*shape):
    return pl.BlockSpec(shape, lambda *_: (0,) * len(shape), pipeline_mode=pl.Buffered(1))


def _rmsnorm_kernel(x_ref, g_ref, o_ref):
    x = x_ref[...]
    y = x * lax.rsqrt(jnp.mean(x * x, axis=-1, keepdims=True) + EPS)
    o_ref[...] = (y * g_ref[...]).astype(o_ref.dtype)


def _rmsnorm(x2d, g, out_dtype, tm=512):
    m, d = x2d.shape
    return pl.pallas_call(
        _rmsnorm_kernel,
        grid=(m // tm,),
        in_specs=[pl.BlockSpec((tm, d), lambda i: (i, 0)),
                  pl.BlockSpec((1, d), lambda i: (0, 0))],
        out_specs=pl.BlockSpec((tm, d), lambda i: (i, 0)),
        out_shape=jax.ShapeDtypeStruct((m, d), out_dtype),
        compiler_params=pltpu.CompilerParams(
            dimension_semantics=("parallel",),
            vmem_limit_bytes=_vmem_limit(2 * tm * d * 8)),
        name="rmsnorm",
    )(x2d, g.reshape(1, d))


CAST_ROWS = 256


def _in_proj_f32w_kernel(a_ref, w_ref, wnext_ref, o_ref, wnext_bf_ref, wbf_ref):
    @pl.when(pl.program_id(1) == 0)
    def _():
        wbf_ref[...] = w_ref[...].astype(BF16)

    o_ref[...] = jnp.dot(a_ref[...], wbf_ref[...], preferred_element_type=F32).astype(o_ref.dtype)
    wnext_bf_ref[...] = wnext_ref[...].astype(BF16)


def _in_proj_f32w(h, w_all, layer, tm=512, tn=1024):
    m, k = h.shape
    n = w_all.shape[2]
    assert (m // tm) * CAST_ROWS == k
    est = 2 * (tm * k * 2 + k * tn * 4 + tm * tn * 2 + CAST_ROWS * tn * 6) + k * tn * 2
    return pl.pallas_call(
        _in_proj_f32w_kernel,
        grid=(n // tn, m // tm),
        in_specs=[pl.BlockSpec((tm, k), lambda j, i: (i, 0)),
                  pl.BlockSpec((None, k, tn), lambda j, i: (layer, 0, j)),
                  pl.BlockSpec((None, CAST_ROWS, tn), lambda j, i: (layer + 1, i, j))],
        out_specs=(pl.BlockSpec((tm, tn), lambda j, i: (i, j)),
                   pl.BlockSpec((CAST_ROWS, tn), lambda j, i: (i, j))),
        out_shape=(jax.ShapeDtypeStruct((m, n), BF16), jax.ShapeDtypeStruct((k, n), BF16)),
        scratch_shapes=[pltpu.VMEM((k, tn), BF16)],
        compiler_params=pltpu.CompilerParams(
            dimension_semantics=("parallel", "arbitrary"),
            vmem_limit_bytes=_vmem_limit(est)),
        name="in_proj_f32w",
    )(h, w_all, w_all)


def _in_proj_scaled_kernel(a_ref, ssq_ref, w_ref, o_ref, r_ref):
    @pl.when(pl.program_id(1) == 0)
    def _():
        ssq = jnp.sum(jnp.sum(ssq_ref[...], axis=0), axis=-1, keepdims=True)
        r_ref[...] = lax.rsqrt(ssq / D_MODEL + EPS)

    acc = jnp.dot(a_ref[...], w_ref[...], preferred_element_type=F32)
    o_ref[...] = (acc * r_ref[...]).astype(o_ref.dtype)


def _in_proj_scaled(xg, ssq, w, tm=1024, tn=1024):
    m, k = xg.shape
    n = w.shape[1]
    parts = ssq.shape[0]
    est = 2 * (tm * k * 2 + k * tn * 2 + tm * tn * 2 + parts * tm * LANES * 4) + tm * LANES * 4
    return pl.pallas_call(
        _in_proj_scaled_kernel,
        grid=(m // tm, n // tn),
        in_specs=[pl.BlockSpec((tm, k), lambda i, j: (i, 0)),
                  pl.BlockSpec((parts, tm, LANES), lambda i, j: (0, i, 0)),
                  pl.BlockSpec((k, tn), lambda i, j: (0, j))],
        out_specs=pl.BlockSpec((tm, tn), lambda i, j: (i, j)),
        out_shape=jax.ShapeDtypeStruct((m, n), BF16),
        scratch_shapes=[pltpu.VMEM((tm, 1), F32)],
        compiler_params=pltpu.CompilerParams(
            dimension_semantics=("parallel", "arbitrary"),
            vmem_limit_bytes=_vmem_limit(est)),
        name="in_proj_scaled",
    )(xg, ssq, w)


def _branch_matmul(yp_ref, ya_ref, ys_ref, w_ref):
    acc = jnp.dot(yp_ref[...], w_ref[0:D_POOL, :], preferred_element_type=F32)
    acc += jnp.dot(ya_ref[...], w_ref[D_POOL:D_POOL + D_ATTN, :], preferred_element_type=F32)
    return acc + jnp.dot(ys_ref[...], w_ref[D_POOL + D_ATTN:, :], preferred_element_type=F32)


def _out_proj_f32w_kernel(yp_ref, ya_ref, ys_ref, w_ref, x_ref, g_ref, wnext_ref,
                          xo_ref, xg_ref, ssq_ref, wnext_bf_ref, wbf_ref):
    @pl.when(pl.program_id(1) == 0)
    def _():
        wbf_ref[...] = w_ref[...].astype(BF16)

    x_new = x_ref[...] + _branch_matmul(yp_ref, ya_ref, ys_ref, wbf_ref)
    xo_ref[...] = x_new
    xg_ref[...] = (x_new * g_ref[...]).astype(xg_ref.dtype)
    sq = x_new * x_new
    ssq_ref[...] = functools.reduce(
        lambda a, b: a + b, [sq[:, c * LANES:(c + 1) * LANES] for c in range(sq.shape[1] // LANES)])
    wnext_bf_ref[...] = wnext_ref[...].astype(BF16)


def _out_proj_f32w(y_pool, y_attn, y_ssm, w_all, layer, x2d, next_g, tm=512, tn=1024):
    m = x2d.shape[0]
    k, n = w_all.shape[1:]
    assert (m // tm) * CAST_ROWS == k
    est = (2 * (tm * k * 2 + tm * tn * (4 + 4 + 2) + tm * LANES * 4 + CAST_ROWS * tn * 6) + k * tn * (4 + 2))
    tile = pl.BlockSpec((tm, tn), lambda j, i: (i, j))
    return pl.pallas_call(
        _out_proj_f32w_kernel,
        grid=(n // tn, m // tm),
        in_specs=[pl.BlockSpec((tm, D_POOL), lambda j, i: (i, 0)),
                  pl.BlockSpec((tm, D_ATTN), lambda j, i: (i, 0)),
                  pl.BlockSpec((tm, D_SSM), lambda j, i: (i, 0)),
                  pl.BlockSpec((None, k, tn), lambda j, i: (layer, 0, j), pipeline_mode=pl.Buffered(1)),
                  tile,
                  pl.BlockSpec((1, tn), lambda j, i: (0, j)),
                  pl.BlockSpec((None, CAST_ROWS, tn), lambda j, i: (layer + 1, i, j))],
        out_specs=(tile, tile,
                   pl.BlockSpec((None, tm, LANES), lambda j, i: (j, i, 0)),
                   pl.BlockSpec((CAST_ROWS, tn), lambda j, i: (i, j))),
        out_shape=(jax.ShapeDtypeStruct((m, n), F32), jax.ShapeDtypeStruct((m, n), BF16),
                   jax.ShapeDtypeStruct((n // tn, m, LANES), F32), jax.ShapeDtypeStruct((k, n), BF16)),
        scratch_shapes=[pltpu.VMEM((k, tn), BF16)],
        compiler_params=pltpu.CompilerParams(
            dimension_semantics=("parallel", "arbitrary"),
            vmem_limit_bytes=_vmem_limit(est)),
        name="out_proj_f32w",
    )(y_pool, y_attn, y_ssm, w_all, x2d, next_g.reshape(1, n), w_all)


def _out_proj_kernel(yp_ref, ya_ref, ys_ref, w_ref, x_ref, o_ref):
    o_ref[...] = x_ref[...] + _branch_matmul(yp_ref, ya_ref, ys_ref, w_ref)


def _out_proj(y_pool, y_attn, y_ssm, w, x2d, tm=1024, tn=512):
    m = x2d.shape[0]
    k, n = w.shape
    est = 2 * (tm * k * 2 + k * tn * 2 + 2 * tm * tn * 4)
    return pl.pallas_call(
        _out_proj_kernel,
        grid=(m // tm, n // tn),
        in_specs=[pl.BlockSpec((tm, D_POOL), lambda i, j: (i, 0)),
                  pl.BlockSpec((tm, D_ATTN), lambda i, j: (i, 0)),
                  pl.BlockSpec((tm, D_SSM), lambda i, j: (i, 0)),
                  pl.BlockSpec((k, tn), lambda i, j: (0, j)),
                  pl.BlockSpec((tm, tn), lambda i, j: (i, j))],
        out_specs=pl.BlockSpec((tm, tn), lambda i, j: (i, j)),
        out_shape=jax.ShapeDtypeStruct((m, n), F32),
        compiler_params=pltpu.CompilerParams(
            dimension_semantics=("parallel", "arbitrary"),
            vmem_limit_bytes=_vmem_limit(est)),
        name="out_proj",
    )(y_pool, y_attn, y_ssm, w, x2d)


def _pool_kernel(px_ref, halo_ref, gate_ref, w_ref, scale_ref, g_ref, o_ref, xs_ref, *, t_blk):
    i = pl.program_id(1)
    x = px_ref[0].astype(F32)
    xs_ref[0:POOL_HALO, :] = jnp.where(i > 0, halo_ref[0].astype(F32), 0.0)
    xs_ref[POOL_HALO:POOL_HALO + t_blk, :] = x
    pos = (i * t_blk + 1 + lax.broadcasted_iota(jnp.int32, (t_blk, 1), 0)).astype(F32)
    mixed = []
    for g, w in enumerate(POOL_WINDOWS):
        cols = slice(g * POOL_GROUP, (g + 1) * POOL_GROUP)
        s = x[:, cols]
        for j in range(1, w):
            s = s + xs_ref[POOL_HALO - j:POOL_HALO - j + t_blk, cols]
        pooled = s / jnp.minimum(pos, float(w)) - x[:, cols]
        mixed.append(jnp.dot(pooled.astype(BF16), w_ref[g], preferred_element_type=F32))
    y = jnp.concatenate(mixed, axis=1) * scale_ref[...]
    y = y * lax.rsqrt(jnp.mean(y * y, axis=-1, keepdims=True) + EPS) * g_ref[...]
    o_ref[0] = (y * _silu(gate_ref[0].astype(F32))).astype(o_ref.dtype)


def _pool_branch(proj3, w_pool, pool_scale, g_pool, t_blk=512):
    b, l, _ = proj3.shape
    halo_per_blk = t_blk // POOL_HALO
    kern = functools.partial(_pool_kernel, t_blk=t_blk)
    return pl.pallas_call(
        kern,
        grid=(b, l // t_blk),
        in_specs=[
            pl.BlockSpec((1, t_blk, D_POOL), lambda bi, i: (bi, i, COL_PX_1024)),
            pl.BlockSpec((1, POOL_HALO, D_POOL),
                         lambda bi, i: (bi, jnp.maximum(i * halo_per_blk - 1, 0), COL_PX_1024)),
            pl.BlockSpec((1, t_blk, D_POOL), lambda bi, i: (bi, i, COL_PGATE_1024)),
            pl.BlockSpec((len(POOL_WINDOWS), POOL_GROUP, POOL_GROUP), lambda bi, i: (0, 0, 0)),
            pl.BlockSpec((1, D_POOL), lambda bi, i: (0, 0)),
            pl.BlockSpec((1, D_POOL), lambda bi, i: (0, 0)),
        ],
        out_specs=pl.BlockSpec((1, t_blk, D_POOL), lambda bi, i: (bi, i, 0)),
        out_shape=jax.ShapeDtypeStruct((b, l, D_POOL), BF16),
        scratch_shapes=[pltpu.VMEM((POOL_HALO + t_blk, D_POOL), F32)],
        compiler_params=pltpu.CompilerParams(
            dimension_semantics=("parallel", "parallel"),
            vmem_limit_bytes=_vmem_limit(8 * t_blk * D_POOL * 4)),
        name="pool_branch",
    )(proj3, proj3, proj3, w_pool, pool_scale.reshape(1, D_POOL), g_pool.reshape(1, D_POOL))


def _attn_kernel(q_ref, k_ref, v_ref, gate_ref, g_ref, o_ref, *, t_blk, n_before):
    n_head = N_HEADS
    i = pl.program_id(1)
    row = lax.broadcasted_iota(jnp.int32, (t_blk, t_blk), 0)
    col = lax.broadcasted_iota(jnp.int32, (t_blk, t_blk), 1)
    later = jnp.where(row > col, 1.0, 0.0).astype(BF16)
    causal = col < row
    qs = [(q_ref[0, :, h * HEAD_DIM:(h + 1) * HEAD_DIM].astype(F32) * (HEAD_DIM ** -0.5 * LOG2_E)).astype(BF16)
          for h in range(n_head)]

    def tile(h, j, carry, diagonal):
        lanes = slice(h * HEAD_DIM, (h + 1) * HEAD_DIM)
        start = pl.multiple_of(j * t_blk, t_blk)
        kj = k_ref[0, pl.ds(start, t_blk), lanes]
        vj = v_ref[0, pl.ds(start, t_blk), lanes]
        z = lax.dot_general(qs[h], kj, (((1,), (1,)), ((), ())), preferred_element_type=F32)
        softplus = jnp.maximum(z, 0.0) + jnp.log(1.0 + jnp.exp2(-jnp.abs(z))) * LOG2_E
        if diagonal:
            softplus = jnp.where(causal, softplus, 0.0)
        suffix = jnp.dot(softplus.astype(BF16), later, preferred_element_type=F32)
        wts = jnp.exp2(z - softplus - suffix - carry)
        if diagonal:
            wts = jnp.where(causal, wts, 0.0)
        pv = jnp.dot(wts.astype(BF16), vj, preferred_element_type=F32)
        return pv, carry + jnp.sum(softplus, axis=1, keepdims=True)

    accs, carries = [], []
    for h in range(n_head):
        acc, carry = tile(h, i, jnp.zeros((t_blk, 1), F32), True)
        for d in range(1, n_before + 1):
            carry = jnp.where(i >= d, carry, NO_BLOCK_CARRY)
            pv, carry = tile(h, jnp.maximum(i - d, 0), carry, False)
            acc = acc + pv
        accs.append(acc)
        carries.append(carry)

    def alive_of(carries):
        return jnp.min(functools.reduce(jnp.minimum, carries)) < F32_EXP2_ZERO_ABOVE

    def cond(state):
        j, _, _, alive = state
        return jnp.logical_and(j >= 0, alive)

    def body(state):
        j, accs, carries, _ = state
        new_accs, new_carries = [], []
        for h in range(n_head):
            pv, carry = tile(h, j, carries[h], False)
            new_accs.append(accs[h] + pv)
            new_carries.append(carry)
        return j - 1, tuple(new_accs), tuple(new_carries), alive_of(new_carries)

    _, accs, _, _ = lax.while_loop(cond, body,
                                   (i - n_before - 1, tuple(accs), tuple(carries), alive_of(carries)))
    y = jnp.concatenate(accs, axis=1)
    y = y * lax.rsqrt(jnp.mean(y * y, axis=-1, keepdims=True) + EPS) * g_ref[...]
    o_ref[0] = (y * _silu(gate_ref[0].astype(F32))).astype(o_ref.dtype)


def _attn_branch(proj3, g_attn, t_blk=256, n_before=1):
    b, l, _ = proj3.shape
    kern = functools.partial(_attn_kernel, t_blk=t_blk, n_before=n_before)
    est = 2 * l * D_ATTN * 2 + 2 * 3 * t_blk * D_ATTN * 2 + N_HEADS * 12 * t_blk * t_blk * 4
    whole_seq = lambda col: pl.BlockSpec((1, l, D_ATTN), lambda bi, i: (bi, 0, col), pipeline_mode=pl.Buffered(1))
    q_rows = lambda col: pl.BlockSpec((1, t_blk, D_ATTN), lambda bi, i: (bi, i, col))
    return pl.pallas_call(
        kern,
        grid=(b, l // t_blk),
        in_specs=[q_rows(COL_Q_2048), whole_seq(COL_K_2048), whole_seq(COL_V_2048), q_rows(COL_AGATE_2048),
                  pl.BlockSpec((1, D_ATTN), lambda bi, i: (0, 0))],
        out_specs=pl.BlockSpec((1, t_blk, D_ATTN), lambda bi, i: (bi, i, 0)),
        out_shape=jax.ShapeDtypeStruct((b, l, D_ATTN), BF16),
        compiler_params=pltpu.CompilerParams(
            dimension_semantics=("parallel", "arbitrary"),
            vmem_limit_bytes=_vmem_limit(est)),
        name="attn_branch",
    )(proj3, proj3, proj3, proj3, g_attn.reshape(1, D_ATTN))


N_SLAB = 4
SLAB_IN = D_SSM // N_SLAB
SLAB_STATES = 1024
GROUPS_PER_SLAB = N_SSM_GROUPS // N_SLAB
N_CHUNK = SLAB_STATES // LANES
TILE_PITCH = SUBLANES + 1


def _discretize_kernel(lam_re_ref, lam_im_ref, log_dt_ref, b_re_ref, b_im_ref,
                       a_re_ref, a_im_ref, wb_re_ref, wb_im_ref):
    lam_re = lam_re_ref[...]
    lam_im = lam_im_ref[...]
    dt = jnp.exp(log_dt_ref[...])
    mag = jnp.exp(lam_re * dt)
    a_re = mag * jnp.cos(lam_im * dt)
    a_im = mag * jnp.sin(lam_im * dt)
    a_re_ref[...] = a_re
    a_im_ref[...] = a_im
    num_re = a_re - 1.0
    den = lam_re * lam_re + lam_im * lam_im
    c_re = ((num_re * lam_re + a_im * lam_im) / den)[:, None, :]
    c_im = ((a_im * lam_re - num_re * lam_im) / den)[:, None, :]
    row_group = lax.broadcasted_iota(jnp.int32, (SLAB_IN, SLAB_STATES), 0) // SSM_GROUP
    col_group = lax.broadcasted_iota(jnp.int32, (SLAB_IN, SLAB_STATES), 1) // SSM_STATE
    on_diag = (row_group == col_group)[None]

    def block_diag(ref):
        return jnp.where(on_diag, jnp.concatenate([ref[...]] * GROUPS_PER_SLAB, axis=-1), 0.0)

    b_re = block_diag(b_re_ref)
    b_im = block_diag(b_im_ref)
    wb_re_ref[...] = (c_re * b_re - c_im * b_im).astype(wb_re_ref.dtype)
    wb_im_ref[...] = (c_re * b_im + c_im * b_re).astype(wb_im_ref.dtype)


def _group_rows_in(b):
    b4 = b.reshape(N_SLAB, GROUPS_PER_SLAB, SSM_STATE, SSM_GROUP).transpose(0, 1, 3, 2)
    return b4.reshape(N_SLAB, SLAB_IN, SSM_STATE)


def _block_diag_out(c):
    c4 = c.reshape(N_SLAB, GROUPS_PER_SLAB, SSM_GROUP, SSM_STATE).transpose(0, 1, 3, 2)
    on_diag = jnp.eye(GROUPS_PER_SLAB, dtype=bool)[None, :, None, :, None]
    return jnp.where(on_diag, c4[:, :, :, None, :], 0).reshape(N_SLAB, SLAB_STATES, SLAB_IN)


def _discretize(lam_re, lam_im, log_dt, b_re, b_im):
    shape4 = (N_SLAB, SLAB_STATES)
    log_dt_full = jnp.broadcast_to(log_dt[:, None], (N_SSM_GROUPS, SSM_STATE)).reshape(shape4)
    out_shape = (jax.ShapeDtypeStruct(shape4, F32), jax.ShapeDtypeStruct(shape4, F32),
                 jax.ShapeDtypeStruct((N_SLAB, SLAB_IN, SLAB_STATES), BF16),
                 jax.ShapeDtypeStruct((N_SLAB, SLAB_IN, SLAB_STATES), BF16))
    return pl.pallas_call(
        _discretize_kernel, out_shape=out_shape, name="s5_discretize",
        compiler_params=pltpu.CompilerParams(vmem_limit_bytes=_vmem_limit(32 << 20)),
    )(lam_re.reshape(shape4), lam_im.reshape(shape4), log_dt_full,
      _group_rows_in(b_re), _group_rows_in(b_im))


def _s5_kernel(u_ref, gate_ref, a_re_ref, a_im_ref, wb_re_ref, wb_im_ref, wc_re_ref, wc_im_ref,
               dskip_ref, wglu_ref, bglu_ref, g_ref, o_ref,
               s_re_ref, s_im_ref, x_re_ref, x_im_ref, *, t_blk, n_batch):
    n_row = n_batch * N_SLAB
    assert n_row == SUBLANES

    @pl.when(pl.program_id(0) == 0)
    def _():
        x_re_ref[...] = jnp.zeros_like(x_re_ref)
        x_im_ref[...] = jnp.zeros_like(x_im_ref)

    for j in range(N_SLAB):
        cols = slice(j * SLAB_IN, (j + 1) * SLAB_IN)
        uj = jnp.concatenate([u_ref[b, :, cols] for b in range(n_batch)], axis=0)
        bu_re = jnp.dot(uj, wb_re_ref[j], preferred_element_type=F32)
        bu_im = jnp.dot(uj, wb_im_ref[j], preferred_element_type=F32)
        for b in range(n_batch):
            rows = slice(b * t_blk, (b + 1) * t_blk)
            k = b * N_SLAB + j
            for c in range(N_CHUNK):
                lanes = slice(c * LANES, (c + 1) * LANES)
                s_re_ref[c, pl.ds(k, t_blk, stride=TILE_PITCH), :] = bu_re[rows, lanes]
                s_im_ref[c, pl.ds(k, t_blk, stride=TILE_PITCH), :] = bu_im[rows, lanes]

    a_re = [a_re_ref[:, c * LANES:(c + 1) * LANES] for c in range(N_CHUNK)]
    a_im = [a_im_ref[:, c * LANES:(c + 1) * LANES] for c in range(N_CHUNK)]
    x_re0 = tuple(x_re_ref[:, c * LANES:(c + 1) * LANES] for c in range(N_CHUNK))
    x_im0 = tuple(x_im_ref[:, c * LANES:(c + 1) * LANES] for c in range(N_CHUNK))

    def step(t, state):
        x_re, x_im = state
        rows = pl.ds(t * TILE_PITCH, n_row)
        new_re, new_im = [], []
        for c in range(N_CHUNK):
            nr = a_re[c] * x_re[c] - a_im[c] * x_im[c] + s_re_ref[c, rows, :]
            ni = a_re[c] * x_im[c] + a_im[c] * x_re[c] + s_im_ref[c, rows, :]
            s_re_ref[c, rows, :] = nr
            s_im_ref[c, rows, :] = ni
            new_re.append(nr)
            new_im.append(ni)
        return tuple(new_re), tuple(new_im)

    x_re, x_im = lax.fori_loop(0, t_blk, step, (x_re0, x_im0), unroll=4)
    for c in range(N_CHUNK):
        x_re_ref[:, c * LANES:(c + 1) * LANES] = x_re[c]
        x_im_ref[:, c * LANES:(c + 1) * LANES] = x_im[c]

    def states(s_ref, j):
        return jnp.concatenate(
            [jnp.concatenate([s_ref[c, pl.ds(b * N_SLAB + j, t_blk, stride=TILE_PITCH), :] for c in range(N_CHUNK)],
                             axis=1) for b in range(n_batch)], axis=0).astype(BF16)

    ys = [jnp.dot(states(s_re_ref, j), wc_re_ref[j], preferred_element_type=F32)
          - jnp.dot(states(s_im_ref, j), wc_im_ref[j], preferred_element_type=F32) for j in range(N_SLAB)]
    u_all = jnp.concatenate([u_ref[b] for b in range(n_batch)], axis=0).astype(F32)
    y = jnp.concatenate(ys, axis=1) + dskip_ref[...] * u_all
    h = jax.nn.gelu(y)
    glu = jnp.dot(h.astype(BF16), wglu_ref[...], preferred_element_type=F32) + bglu_ref[...]
    out = glu[:, :D_SSM] * jax.nn.sigmoid(glu[:, D_SSM:])
    out = out * lax.rsqrt(jnp.mean(out * out, axis=-1, keepdims=True) + EPS) * g_ref[...]
    for b in range(n_batch):
        rows = slice(b * t_blk, (b + 1) * t_blk)
        o_ref[b] = (out[rows] * _silu(gate_ref[b].astype(F32))).astype(o_ref.dtype)


def _s5_branch(proj3, a_re, a_im, wb_re, wb_im, wc_re, wc_im, d_skip, w_glu, b_glu, g_ssm, t_blk=256):
    b, l, _ = proj3.shape
    n_row = b * N_SLAB
    a_re_t = jnp.tile(a_re, (b, 1))
    a_im_t = jnp.tile(a_im, (b, 1))
    kern = functools.partial(_s5_kernel, t_blk=t_blk, n_batch=b)
    weights = 4 * N_SLAB * SLAB_IN * SLAB_STATES * 2 + D_SSM * 2 * D_SSM * 2
    est = (weights + 2 * N_CHUNK * t_blk * TILE_PITCH * LANES * 4 + 2 * 3 * b * t_blk * D_SSM * 2
           + 6 * b * t_blk * 2 * D_SSM * 4)
    return pl.pallas_call(
        kern,
        grid=(l // t_blk,),
        in_specs=[
            pl.BlockSpec((b, t_blk, D_SSM), lambda i: (0, i, COL_SU_1024)),
            pl.BlockSpec((b, t_blk, D_SSM), lambda i: (0, i, COL_SGATE_1024)),
            _resident(n_row, SLAB_STATES), _resident(n_row, SLAB_STATES),
            _resident(N_SLAB, SLAB_IN, SLAB_STATES), _resident(N_SLAB, SLAB_IN, SLAB_STATES),
            _resident(N_SLAB, SLAB_STATES, SLAB_IN), _resident(N_SLAB, SLAB_STATES, SLAB_IN),
            _resident(1, D_SSM), _resident(D_SSM, 2 * D_SSM), _resident(1, 2 * D_SSM), _resident(1, D_SSM),
        ],
        out_specs=pl.BlockSpec((b, t_blk, D_SSM), lambda i: (0, i, 0)),
        out_shape=jax.ShapeDtypeStruct((b, l, D_SSM), BF16),
        scratch_shapes=[pltpu.VMEM((N_CHUNK, t_blk * TILE_PITCH, LANES), F32),
                        pltpu.VMEM((N_CHUNK, t_blk * TILE_PITCH, LANES), F32),
                        pltpu.VMEM((n_row, SLAB_STATES), F32),
                        pltpu.VMEM((n_row, SLAB_STATES), F32)],
        compiler_params=pltpu.CompilerParams(
            dimension_semantics=("arbitrary",),
            vmem_limit_bytes=_vmem_limit(est)),
        name="s5_branch",
    )(proj3, proj3, a_re_t, a_im_t, wb_re, wb_im, wc_re, wc_im,
      d_skip.reshape(1, D_SSM), w_glu, b_glu.reshape(1, 2 * D_SSM), g_ssm.reshape(1, D_SSM))


def _mixers(proj, bsz, w_pool, pool_scale, lam_re, lam_im, log_dt,
            b_re, b_im, c_re, c_im, d_skip, w_glu, b_glu, branch_g):
    m = proj.shape[0]
    proj3 = proj.reshape(bsz, m // bsz, D_IN)
    g_pool = branch_g[:D_POOL]
    g_attn = branch_g[D_POOL:D_POOL + D_ATTN]
    g_ssm = branch_g[D_POOL + D_ATTN:]
    y_pool = _pool_branch(proj3, w_pool.astype(BF16), pool_scale, g_pool)
    y_attn = _attn_branch(proj3, g_attn)
    a_re, a_im, wb_re, wb_im = _discretize(lam_re, lam_im, log_dt, b_re, b_im)
    y_ssm = _s5_branch(proj3, a_re, a_im, wb_re, wb_im,
                       _block_diag_out(c_re).astype(BF16), _block_diag_out(c_im).astype(BF16),
                       d_skip, w_glu.astype(BF16), b_glu, g_ssm)
    return y_pool.reshape(m, D_POOL), y_attn.reshape(m, D_ATTN), y_ssm.reshape(m, D_SSM)


def kernel(x, ln_g, w_in, w_pool, pool_scale, lam_re, lam_im, log_dt, b_re, b_im, c_re, c_im,
           d_skip, w_glu, b_glu, branch_g, w_out, final_g):
    bsz, l, d = x.shape
    assert ln_g.shape[0] == 2
    mixer_params = (w_pool, pool_scale, lam_re, lam_im, log_dt, b_re, b_im, c_re, c_im,
                    d_skip, w_glu, b_glu, branch_g)
    x0 = x.reshape(bsz * l, d)

    h = _rmsnorm(x0, ln_g[0], BF16)
    proj, w_in1 = _in_proj_f32w(h, w_in, 0)
    ys = _mixers(proj, bsz, *[p[0] for p in mixer_params])
    x1, xg, ssq, w_out1 = _out_proj_f32w(*ys, w_out, 0, x0, ln_g[1])

    proj = _in_proj_scaled(xg, ssq, w_in1)
    ys = _mixers(proj, bsz, *[p[1] for p in mixer_params])
    x2 = _out_proj(*ys, w_out1, x1)
    return _rmsnorm(x2, final_g, x.dtype).reshape(bsz, l, d)
```

```python
import functools
import math

import jax
import jax.numpy as jnp
from jax import lax
from jax.experimental import pallas as pl
from jax.experimental.pallas import tpu as pltpu

D_MODEL = 4096
D_POOL = 1024
D_ATTN = 2048
D_SSM = 1024
POOL_WINDOWS = (2, 4, 8, 16)
POOL_GROUP = 256
POOL_HALO = 16
HEAD_DIM = 128
N_HEADS = 16
SSM_GROUP = 16
SSM_STATE = 64
N_SSM_GROUPS = 64
D_IN = 12288
EPS = 1e-6

COL_PX_1024 = 0
COL_PGATE_1024 = 1
COL_Q_2048 = 1
COL_K_2048 = 2
COL_V_2048 = 3
COL_AGATE_2048 = 4
COL_SU_1024 = 10
COL_SGATE_1024 = 11

V7X_VMEM_BYTES = 64 * 1024 * 1024
SUBLANES = 8
LANES = 128

LOG2_E = 1.0 / math.log(2.0)
F32_EXP2_ZERO_ABOVE = 150.0
NO_BLOCK_CARRY = 1e30

BF16 = jnp.bfloat16
F32 = jnp.float32


def _vmem_limit(nbytes):
    return int(min(nbytes * 3 // 2 + (4 << 20), V7X_VMEM_BYTES - (6 << 20)))


def _silu(x):
    return x * jax.nn.sigmoid(x)


def _resident(*shape):
    return pl.BlockSpec(shape, lambda *_: (0,) * len(shape), pipeline_mode=pl.Buffered(1))


def _rmsnorm_kernel(x_ref, g_ref, o_ref):
    x = x_ref[...]
    y = x * lax.rsqrt(jnp.mean(x * x, axis=-1, keepdims=True) + EPS)
    o_ref[...] = (y * g_ref[...]).astype(o_ref.dtype)


def _rmsnorm(x2d, g, out_dtype, tm=512):
    m, d = x2d.shape
    return pl.pallas_call(
        _rmsnorm_kernel,
        grid=(m // tm,),
        in_specs=[pl.BlockSpec((tm, d), lambda i: (i, 0)),
                  pl.BlockSpec((1, d), lambda i: (0, 0))],
        out_specs=pl.BlockSpec((tm, d), lambda i: (i, 0)),
        out_shape=jax.ShapeDtypeStruct((m, d), out_dtype),
        compiler_params=pltpu.CompilerParams(
            dimension_semantics=("parallel",),
            vmem_limit_bytes=_vmem_limit(2 * tm * d * 8)),
        name="rmsnorm",
    )(x2d, g.reshape(1, d))


CAST_ROWS = 256


def _in_proj_f32w_kernel(a_ref, w_ref, wnext_ref, o_ref, wnext_bf_ref, wbf_ref):
    @pl.when(pl.program_id(1) == 0)
    def _():
        wbf_ref[...] = w_ref[...].astype(BF16)

    o_ref[...] = jnp.dot(a_ref[...], wbf_ref[...], preferred_element_type=F32).astype(o_ref.dtype)
    wnext_bf_ref[...] = wnext_ref[...].astype(BF16)


def _in_proj_f32w(h, w_all, layer, tm=512, tn=1024):
    m, k = h.shape
    n = w_all.shape[2]
    assert (m // tm) * CAST_ROWS == k
    est = 2 * (tm * k * 2 + k * tn * 4 + tm * tn * 2 + CAST_ROWS * tn * 6) + k * tn * 2
    return pl.pallas_call(
        _in_proj_f32w_kernel,
        grid=(n // tn, m // tm),
        in_specs=[pl.BlockSpec((tm, k), lambda j, i: (i, 0)),
                  pl.BlockSpec((None, k, tn), lambda j, i: (layer, 0, j)),
                  pl.BlockSpec((None, CAST_ROWS, tn), lambda j, i: (layer + 1, i, j))],
        out_specs=(pl.BlockSpec((tm, tn), lambda j, i: (i, j)),
                   pl.BlockSpec((CAST_ROWS, tn), lambda j, i: (i, j))),
        out_shape=(jax.ShapeDtypeStruct((m, n), BF16), jax.ShapeDtypeStruct((k, n), BF16)),
        scratch_shapes=[pltpu.VMEM((k, tn), BF16)],
        compiler_params=pltpu.CompilerParams(
            dimension_semantics=("parallel", "arbitrary"),
            vmem_limit_bytes=_vmem_limit(est)),
        name="in_proj_f32w",
    )(h, w_all, w_all)


def _in_proj_scaled_kernel(a_ref, ssq_ref, w_ref, o_ref, r_ref):
    @pl.when(pl.program_id(1) == 0)
    def _():
        ssq = jnp.sum(jnp.sum(ssq_ref[...], axis=0), axis=-1, keepdims=True)
        r_ref[...] = lax.rsqrt(ssq / D_MODEL + EPS)

    acc = jnp.dot(a_ref[...], w_ref[...], preferred_element_type=F32)
    o_ref[...] = (acc * r_ref[...]).astype(o_ref.dtype)


def _in_proj_scaled(xg, ssq, w, tm=1024, tn=1024):
    m, k = xg.shape
    n = w.shape[1]
    parts = ssq.shape[0]
    est = 2 * (tm * k * 2 + k * tn * 2 + tm * tn * 2 + parts * tm * LANES * 4) + tm * LANES * 4
    return pl.pallas_call(
        _in_proj_scaled_kernel,
        grid=(m // tm, n // tn),
        in_specs=[pl.BlockSpec((tm, k), lambda i, j: (i, 0)),
                  pl.BlockSpec((parts, tm, LANES), lambda i, j: (0, i, 0)),
                  pl.BlockSpec((k, tn), lambda i, j: (0, j))],
        out_specs=pl.BlockSpec((tm, tn), lambda i, j: (i, j)),
        out_shape=jax.ShapeDtypeStruct((m, n), BF16),
        scratch_shapes=[pltpu.VMEM((tm, 1), F32)],
        compiler_params=pltpu.CompilerParams(
            dimension_semantics=("parallel", "arbitrary"),
            vmem_limit_bytes=_vmem_limit(est)),
        name="in_proj_scaled",
    )(xg, ssq, w)


def _branch_matmul(yp_ref, ya_ref, ys_ref, w_ref):
    acc = jnp.dot(yp_ref[...], w_ref[0:D_POOL, :], preferred_element_type=F32)
    acc += jnp.dot(ya_ref[...], w_ref[D_POOL:D_POOL + D_ATTN, :], preferred_element_type=F32)
    return acc + jnp.dot(ys_ref[...], w_ref[D_POOL + D_ATTN:, :], preferred_element_type=F32)


def _out_proj_f32w_kernel(yp_ref, ya_ref, ys_ref, w_ref, x_ref, g_ref, wnext_ref,
                          xo_ref, xg_ref, ssq_ref, wnext_bf_ref, wbf_ref):
    @pl.when(pl.program_id(1) == 0)
    def _():
        wbf_ref[...] = w_ref[...].astype(BF16)

    x_new = x_ref[...] + _branch_matmul(yp_ref, ya_ref, ys_ref, wbf_ref)
    xo_ref[...] = x_new
    xg_ref[...] = (x_new * g_ref[...]).astype(xg_ref.dtype)
    sq = x_new * x_new
    ssq_ref[...] = functools.reduce(
        lambda a, b: a + b, [sq[:, c * LANES:(c + 1) * LANES] for c in range(sq.shape[1] // LANES)])
    wnext_bf_ref[...] = wnext_ref[...].astype(BF16)


def _out_proj_f32w(y_pool, y_attn, y_ssm, w_all, layer, x2d, next_g, tm=512, tn=1024):
    m = x2d.shape[0]
    k, n = w_all.shape[1:]
    assert (m // tm) * CAST_ROWS == k
    est = (2 * (tm * k * 2 + tm * tn * (4 + 4 + 2) + tm * LANES * 4 + CAST_ROWS * tn * 6) + k * tn * (4 + 2))
    tile = pl.BlockSpec((tm, tn), lambda j, i: (i, j))
    return pl.pallas_call(
        _out_proj_f32w_kernel,
        grid=(n // tn, m // tm),
        in_specs=[pl.BlockSpec((tm, D_POOL), lambda j, i: (i, 0)),
                  pl.BlockSpec((tm, D_ATTN), lambda j, i: (i, 0)),
                  pl.BlockSpec((tm, D_SSM), lambda j, i: (i, 0)),
                  pl.BlockSpec((None, k, tn), lambda j, i: (layer, 0, j), pipeline_mode=pl.Buffered(1)),
                  tile,
                  pl.BlockSpec((1, tn), lambda j, i: (0, j)),
                  pl.BlockSpec((None, CAST_ROWS, tn), lambda j, i: (layer + 1, i, j))],
        out_specs=(tile, tile,
                   pl.BlockSpec((None, tm, LANES), lambda j, i: (j, i, 0)),
                   pl.BlockSpec((CAST_ROWS, tn), lambda j, i: (i, j))),
        out_shape=(jax.ShapeDtypeStruct((m, n), F32), jax.ShapeDtypeStruct((m, n), BF16),
                   jax.ShapeDtypeStruct((n // tn, m, LANES), F32), jax.ShapeDtypeStruct((k, n), BF16)),
        scratch_shapes=[pltpu.VMEM((k, tn), BF16)],
        compiler_params=pltpu.CompilerParams(
            dimension_semantics=("parallel", "arbitrary"),
            vmem_limit_bytes=_vmem_limit(est)),
        name="out_proj_f32w",
    )(y_pool, y_attn, y_ssm, w_all, x2d, next_g.reshape(1, n), w_all)


def _out_proj_kernel(yp_ref, ya_ref, ys_ref, w_ref, x_ref, o_ref):
    o_ref[...] = x_ref[...] + _branch_matmul(yp_ref, ya_ref, ys_ref, w_ref)


def _out_proj(y_pool, y_attn, y_ssm, w, x2d, tm=1024, tn=512):
    m = x2d.shape[0]
    k, n = w.shape
    est = 2 * (tm * k * 2 + k * tn * 2 + 2 * tm * tn * 4)
    return pl.pallas_call(
        _out_proj_kernel,
        grid=(m // tm, n // tn),
        in_specs=[pl.BlockSpec((tm, D_POOL), lambda i, j: (i, 0)),
                  pl.BlockSpec((tm, D_ATTN), lambda i, j: (i, 0)),
                  pl.BlockSpec((tm, D_SSM), lambda i, j: (i, 0)),
                  pl.BlockSpec((k, tn), lambda i, j: (0, j)),
                  pl.BlockSpec((tm, tn), lambda i, j: (i, j))],
        out_specs=pl.BlockSpec((tm, tn), lambda i, j: (i, j)),
        out_shape=jax.ShapeDtypeStruct((m, n), F32),
        compiler_params=pltpu.CompilerParams(
            dimension_semantics=("parallel", "arbitrary"),
            vmem_limit_bytes=_vmem_limit(est)),
        name="out_proj",
    )(y_pool, y_attn, y_ssm, w, x2d)


def _pool_kernel(px_ref, halo_ref, gate_ref, w_ref, scale_ref, g_ref, o_ref, xs_ref, *, t_blk):
    i = pl.program_id(1)
    x = px_ref[0].astype(F32)
    xs_ref[0:POOL_HALO, :] = jnp.where(i > 0, halo_ref[0].astype(F32), 0.0)
    xs_ref[POOL_HALO:POOL_HALO + t_blk, :] = x
    pos = (i * t_blk + 1 + lax.broadcasted_iota(jnp.int32, (t_blk, 1), 0)).astype(F32)
    mixed = []
    for g, w in enumerate(POOL_WINDOWS):
        cols = slice(g * POOL_GROUP, (g + 1) * POOL_GROUP)
        s = x[:, cols]
        for j in range(1, w):
            s = s + xs_ref[POOL_HALO - j:POOL_HALO - j + t_blk, cols]
        pooled = s / jnp.minimum(pos, float(w)) - x[:, cols]
        mixed.append(jnp.dot(pooled.astype(BF16), w_ref[g], preferred_element_type=F32))
    y = jnp.concatenate(mixed, axis=1) * scale_ref[...]
    y = y * lax.rsqrt(jnp.mean(y * y, axis=-1, keepdims=True) + EPS) * g_ref[...]
    o_ref[0] = (y * _silu(gate_ref[0].astype(F32))).astype(o_ref.dtype)


def _pool_branch(proj3, w_pool, pool_scale, g_pool, t_blk=512):
    b, l, _ = proj3.shape
    halo_per_blk = t_blk // POOL_HALO
    kern = functools.partial(_pool_kernel, t_blk=t_blk)
    return pl.pallas_call(
        kern,
        grid=(b, l // t_blk),
        in_specs=[
            pl.BlockSpec((1, t_blk, D_POOL), lambda bi, i: (bi, i, COL_PX_1024)),
            pl.BlockSpec((1, POOL_HALO, D_POOL),
                         lambda bi, i: (bi, jnp.maximum(i * halo_per_blk - 1, 0), COL_PX_1024)),
            pl.BlockSpec((1, t_blk, D_POOL), lambda bi, i: (bi, i, COL_PGATE_1024)),
            pl.BlockSpec((len(POOL_WINDOWS), POOL_GROUP, POOL_GROUP), lambda bi, i: (0, 0, 0)),
            pl.BlockSpec((1, D_POOL), lambda bi, i: (0, 0)),
            pl.BlockSpec((1, D_POOL), lambda bi, i: (0, 0)),
        ],
        out_specs=pl.BlockSpec((1, t_blk, D_POOL), lambda bi, i: (bi, i, 0)),
        out_shape=jax.ShapeDtypeStruct((b, l, D_POOL), BF16),
        scratch_shapes=[pltpu.VMEM((POOL_HALO + t_blk, D_POOL), F32)],
        compiler_params=pltpu.CompilerParams(
            dimension_semantics=("parallel", "parallel"),
            vmem_limit_bytes=_vmem_limit(8 * t_blk * D_POOL * 4)),
        name="pool_branch",
    )(proj3, proj3, proj3, w_pool, pool_scale.reshape(1, D_POOL), g_pool.reshape(1, D_POOL))


def _attn_kernel(q_ref, k_ref, v_ref, gate_ref, g_ref, o_ref, *, t_blk, n_before):
    n_head = N_HEADS
    i = pl.program_id(1)
    row = lax.broadcasted_iota(jnp.int32, (t_blk, t_blk), 0)
    col = lax.broadcasted_iota(jnp.int32, (t_blk, t_blk), 1)
    later = jnp.where(row > col, 1.0, 0.0).astype(BF16)
    causal = col < row
    qs = [(q_ref[0, :, h * HEAD_DIM:(h + 1) * HEAD_DIM].astype(F32) * (HEAD_DIM ** -0.5 * LOG2_E)).astype(BF16)
          for h in range(n_head)]

    def tile(h, j, carry, diagonal):
        lanes = slice(h * HEAD_DIM, (h + 1) * HEAD_DIM)
        start = pl.multiple_of(j * t_blk, t_blk)
        kj = k_ref[0, pl.ds(start, t_blk), lanes]
        vj = v_ref[0, pl.ds(start, t_blk), lanes]
        z = lax.dot_general(qs[h], kj, (((1,), (1,)), ((), ())), preferred_element_type=F32)
        softplus = jnp.maximum(z, 0.0) + jnp.log(1.0 + jnp.exp2(-jnp.abs(z))) * LOG2_E
        if diagonal:
            softplus = jnp.where(causal, softplus, 0.0)
        suffix = jnp.dot(softplus.astype(BF16), later, preferred_element_type=F32)
        wts = jnp.exp2(z - softplus - suffix - carry)
        if diagonal:
            wts = jnp.where(causal, wts, 0.0)
        pv = jnp.dot(wts.astype(BF16), vj, preferred_element_type=F32)
        return pv, carry + jnp.sum(softplus, axis=1, keepdims=True)

    accs, carries = [], []
    for h in range(n_head):
        acc, carry = tile(h, i, jnp.zeros((t_blk, 1), F32), True)
        for d in range(1, n_before + 1):
            carry = jnp.where(i >= d, carry, NO_BLOCK_CARRY)
            pv, carry = tile(h, jnp.maximum(i - d, 0), carry, False)
            acc = acc + pv
        accs.append(acc)
        carries.append(carry)

    def alive_of(carries):
        return jnp.min(functools.reduce(jnp.minimum, carries)) < F32_EXP2_ZERO_ABOVE

    def cond(state):
        j, _, _, alive = state
        return jnp.logical_and(j >= 0, alive)

    def body(state):
        j, accs, carries, _ = state
        new_accs, new_carries = [], []
        for h in range(n_head):
            pv, carry = tile(h, j, carries[h], False)
            new_accs.append(accs[h] + pv)
            new_carries.append(carry)
        return j - 1, tuple(new_accs), tuple(new_carries), alive_of(new_carries)

    def finish(accs):
        y = jnp.concatenate(accs, axis=1)
        y = y * lax.rsqrt(jnp.mean(y * y, axis=-1, keepdims=True) + EPS) * g_ref[...]
        o_ref[0] = (y * _silu(gate_ref[0].astype(F32))).astype(o_ref.dtype)

    finish(accs)
    first = i - n_before - 1

    @pl.when(jnp.logical_and(first >= 0, alive_of(carries)))
    def _():
        _, more, _, _ = lax.while_loop(cond, body, (first, tuple(accs), tuple(carries), True))
        finish(more)


def _attn_branch(proj3, g_attn, t_blk=256, n_before=1):
    b, l, _ = proj3.shape
    kern = functools.partial(_attn_kernel, t_blk=t_blk, n_before=n_before)
    est = 2 * l * D_ATTN * 2 + 2 * 3 * t_blk * D_ATTN * 2 + N_HEADS * 12 * t_blk * t_blk * 4
    whole_seq = lambda col: pl.BlockSpec((1, l, D_ATTN), lambda bi, i: (bi, 0, col), pipeline_mode=pl.Buffered(1))
    q_rows = lambda col: pl.BlockSpec((1, t_blk, D_ATTN), lambda bi, i: (bi, i, col))
    return pl.pallas_call(
        kern,
        grid=(b, l // t_blk),
        in_specs=[q_rows(COL_Q_2048), whole_seq(COL_K_2048), whole_seq(COL_V_2048), q_rows(COL_AGATE_2048),
                  pl.BlockSpec((1, D_ATTN), lambda bi, i: (0, 0))],
        out_specs=pl.BlockSpec((1, t_blk, D_ATTN), lambda bi, i: (bi, i, 0)),
        out_shape=jax.ShapeDtypeStruct((b, l, D_ATTN), BF16),
        compiler_params=pltpu.CompilerParams(
            dimension_semantics=("parallel", "arbitrary"),
            vmem_limit_bytes=_vmem_limit(est)),
        name="attn_branch",
    )(proj3, proj3, proj3, proj3, g_attn.reshape(1, D_ATTN))


N_SLAB = 4
SLAB_IN = D_SSM // N_SLAB
SLAB_STATES = 1024
GROUPS_PER_SLAB = N_SSM_GROUPS // N_SLAB
N_CHUNK = SLAB_STATES // LANES
TILE_PITCH = SUBLANES + 1


def _block_diag(ref, rows_per_group, cols_per_group):
    n_rows = ref.shape[1]
    n_cols = GROUPS_PER_SLAB * cols_per_group
    row_group = lax.broadcasted_iota(jnp.int32, (n_rows, n_cols), 0) // rows_per_group
    col_group = lax.broadcasted_iota(jnp.int32, (n_rows, n_cols), 1) // cols_per_group
    return jnp.where((row_group == col_group)[None], jnp.concatenate([ref[...]] * GROUPS_PER_SLAB, axis=-1), 0.0)


def _discretize_kernel(lam_re_ref, lam_im_ref, log_dt_ref, b_re_ref, b_im_ref, c_re_ref, c_im_ref,
                       a_re_ref, a_im_ref, wb_re_ref, wb_im_ref, wc_re_ref, wc_im_ref):
    lam_re = lam_re_ref[...]
    lam_im = lam_im_ref[...]
    dt = jnp.exp(log_dt_ref[...])
    mag = jnp.exp(lam_re * dt)
    a_re = mag * jnp.cos(lam_im * dt)
    a_im = mag * jnp.sin(lam_im * dt)
    a_re_ref[...] = a_re
    a_im_ref[...] = a_im
    num_re = a_re - 1.0
    den = lam_re * lam_re + lam_im * lam_im
    c_re = ((num_re * lam_re + a_im * lam_im) / den)[:, None, :]
    c_im = ((a_im * lam_re - num_re * lam_im) / den)[:, None, :]
    b_re = _block_diag(b_re_ref, SSM_GROUP, SSM_STATE)
    b_im = _block_diag(b_im_ref, SSM_GROUP, SSM_STATE)
    wb_re_ref[...] = (c_re * b_re - c_im * b_im).astype(wb_re_ref.dtype)
    wb_im_ref[...] = (c_re * b_im + c_im * b_re).astype(wb_im_ref.dtype)
    wc_re_ref[...] = _block_diag(c_re_ref, SSM_STATE, SSM_GROUP).astype(wc_re_ref.dtype)
    wc_im_ref[...] = _block_diag(c_im_ref, SSM_STATE, SSM_GROUP).astype(wc_im_ref.dtype)


def _discretize(lam_re, lam_im, log_dt, b_re, b_im, c_re, c_im):
    shape4 = (N_SLAB, SLAB_STATES)
    log_dt_full = jnp.broadcast_to(log_dt[:, None], (N_SSM_GROUPS, SSM_STATE)).reshape(shape4)
    b_rows = lambda b: jnp.swapaxes(b, 1, 2).reshape(N_SLAB, SLAB_IN, SSM_STATE)
    c_rows = lambda c: jnp.swapaxes(c, 1, 2).reshape(N_SLAB, SLAB_STATES, SSM_GROUP)
    out_shape = (jax.ShapeDtypeStruct(shape4, F32), jax.ShapeDtypeStruct(shape4, F32),
                 jax.ShapeDtypeStruct((N_SLAB, SLAB_IN, SLAB_STATES), BF16),
                 jax.ShapeDtypeStruct((N_SLAB, SLAB_IN, SLAB_STATES), BF16),
                 jax.ShapeDtypeStruct((N_SLAB, SLAB_STATES, SLAB_IN), BF16),
                 jax.ShapeDtypeStruct((N_SLAB, SLAB_STATES, SLAB_IN), BF16))
    return pl.pallas_call(
        _discretize_kernel, out_shape=out_shape, name="s5_discretize",
        compiler_params=pltpu.CompilerParams(vmem_limit_bytes=_vmem_limit(32 << 20)),
    )(lam_re.reshape(shape4), lam_im.reshape(shape4), log_dt_full,
      b_rows(b_re), b_rows(b_im), c_rows(c_re), c_rows(c_im))


def _s5_kernel(u_ref, gate_ref, a_re_ref, a_im_ref, wb_re_ref, wb_im_ref, wc_re_ref, wc_im_ref,
               dskip_ref, wglu_ref, bglu_ref, g_ref, o_ref,
               s_re_ref, s_im_ref, x_re_ref, x_im_ref, *, t_blk, n_batch):
    n_row = n_batch * N_SLAB
    assert n_row == SUBLANES

    @pl.when(pl.program_id(0) == 0)
    def _():
        x_re_ref[...] = jnp.zeros_like(x_re_ref)
        x_im_ref[...] = jnp.zeros_like(x_im_ref)

    for j in range(N_SLAB):
        cols = slice(j * SLAB_IN, (j + 1) * SLAB_IN)
        uj = jnp.concatenate([u_ref[b, :, cols] for b in range(n_batch)], axis=0)
        bu_re = jnp.dot(uj, wb_re_ref[j], preferred_element_type=F32)
        bu_im = jnp.dot(uj, wb_im_ref[j], preferred_element_type=F32)
        for b in range(n_batch):
            rows = slice(b * t_blk, (b + 1) * t_blk)
            k = b * N_SLAB + j
            for c in range(N_CHUNK):
                lanes = slice(c * LANES, (c + 1) * LANES)
                s_re_ref[c, pl.ds(k, t_blk, stride=TILE_PITCH), :] = bu_re[rows, lanes]
                s_im_ref[c, pl.ds(k, t_blk, stride=TILE_PITCH), :] = bu_im[rows, lanes]

    a_re = [a_re_ref[:, c * LANES:(c + 1) * LANES] for c in range(N_CHUNK)]
    a_im = [a_im_ref[:, c * LANES:(c + 1) * LANES] for c in range(N_CHUNK)]
    x_re0 = tuple(x_re_ref[:, c * LANES:(c + 1) * LANES] for c in range(N_CHUNK))
    x_im0 = tuple(x_im_ref[:, c * LANES:(c + 1) * LANES] for c in range(N_CHUNK))

    def step(t, state):
        x_re, x_im = state
        rows = pl.ds(t * TILE_PITCH, n_row)
        new_re, new_im = [], []
        for c in range(N_CHUNK):
            nr = a_re[c] * x_re[c] - a_im[c] * x_im[c] + s_re_ref[c, rows, :]
            ni = a_re[c] * x_im[c] + a_im[c] * x_re[c] + s_im_ref[c, rows, :]
            s_re_ref[c, rows, :] = nr
            s_im_ref[c, rows, :] = ni
            new_re.append(nr)
            new_im.append(ni)
        return tuple(new_re), tuple(new_im)

    x_re, x_im = lax.fori_loop(0, t_blk, step, (x_re0, x_im0), unroll=4)
    for c in range(N_CHUNK):
        x_re_ref[:, c * LANES:(c + 1) * LANES] = x_re[c]
        x_im_ref[:, c * LANES:(c + 1) * LANES] = x_im[c]

    def states(s_ref, j):
        return jnp.concatenate(
            [jnp.concatenate([s_ref[c, pl.ds(b * N_SLAB + j, t_blk, stride=TILE_PITCH), :] for c in range(N_CHUNK)],
                             axis=1) for b in range(n_batch)], axis=0).astype(BF16)

    ys = [jnp.dot(states(s_re_ref, j), wc_re_ref[j], preferred_element_type=F32)
          - jnp.dot(states(s_im_ref, j), wc_im_ref[j], preferred_element_type=F32) for j in range(N_SLAB)]
    u_all = jnp.concatenate([u_ref[b] for b in range(n_batch)], axis=0).astype(F32)
    y = jnp.concatenate(ys, axis=1) + dskip_ref[...] * u_all
    h = jax.nn.gelu(y)
    glu = jnp.dot(h.astype(BF16), wglu_ref[...], preferred_element_type=F32) + bglu_ref[...]
    out = glu[:, :D_SSM] * jax.nn.sigmoid(glu[:, D_SSM:])
    out = out * lax.rsqrt(jnp.mean(out * out, axis=-1, keepdims=True) + EPS) * g_ref[...]
    for b in range(n_batch):
        rows = slice(b * t_blk, (b + 1) * t_blk)
        o_ref[b] = (out[rows] * _silu(gate_ref[b].astype(F32))).astype(o_ref.dtype)


def _s5_branch(proj3, a_re, a_im, wb_re, wb_im, wc_re, wc_im, d_skip, w_glu, b_glu, g_ssm, t_blk=256):
    b, l, _ = proj3.shape
    n_row = b * N_SLAB
    a_re_t = jnp.tile(a_re, (b, 1))
    a_im_t = jnp.tile(a_im, (b, 1))
    kern = functools.partial(_s5_kernel, t_blk=t_blk, n_batch=b)
    weights = 4 * N_SLAB * SLAB_IN * SLAB_STATES * 2 + D_SSM * 2 * D_SSM * 2
    est = (weights + 2 * N_CHUNK * t_blk * TILE_PITCH * LANES * 4 + 2 * 3 * b * t_blk * D_SSM * 2
           + 6 * b * t_blk * 2 * D_SSM * 4)
    return pl.pallas_call(
        kern,
        grid=(l // t_blk,),
        in_specs=[
            pl.BlockSpec((b, t_blk, D_SSM), lambda i: (0, i, COL_SU_1024)),
            pl.BlockSpec((b, t_blk, D_SSM), lambda i: (0, i, COL_SGATE_1024)),
            _resident(n_row, SLAB_STATES), _resident(n_row, SLAB_STATES),
            _resident(N_SLAB, SLAB_IN, SLAB_STATES), _resident(N_SLAB, SLAB_IN, SLAB_STATES),
            _resident(N_SLAB, SLAB_STATES, SLAB_IN), _resident(N_SLAB, SLAB_STATES, SLAB_IN),
            _resident(1, D_SSM), _resident(D_SSM, 2 * D_SSM), _resident(1, 2 * D_SSM), _resident(1, D_SSM),
        ],
        out_specs=pl.BlockSpec((b, t_blk, D_SSM), lambda i: (0, i, 0)),
        out_shape=jax.ShapeDtypeStruct((b, l, D_SSM), BF16),
        scratch_shapes=[pltpu.VMEM((N_CHUNK, t_blk * TILE_PITCH, LANES), F32),
                        pltpu.VMEM((N_CHUNK, t_blk * TILE_PITCH, LANES), F32),
                        pltpu.VMEM((n_row, SLAB_STATES), F32),
                        pltpu.VMEM((n_row, SLAB_STATES), F32)],
        compiler_params=pltpu.CompilerParams(
            dimension_semantics=("arbitrary",),
            vmem_limit_bytes=_vmem_limit(est)),
        name="s5_branch",
    )(proj3, proj3, a_re_t, a_im_t, wb_re, wb_im, wc_re, wc_im,
      d_skip.reshape(1, D_SSM), w_glu, b_glu.reshape(1, 2 * D_SSM), g_ssm.reshape(1, D_SSM))


def _mixers(proj, bsz, w_pool, pool_scale, lam_re, lam_im, log_dt,
            b_re, b_im, c_re, c_im, d_skip, w_glu, b_glu, branch_g):
    m = proj.shape[0]
    proj3 = proj.reshape(bsz, m // bsz, D_IN)
    g_pool = branch_g[:D_POOL]
    g_attn = branch_g[D_POOL:D_POOL + D_ATTN]
    g_ssm = branch_g[D_POOL + D_ATTN:]
    y_pool = _pool_branch(proj3, w_pool.astype(BF16), pool_scale, g_pool)
    y_attn = _attn_branch(proj3, g_attn)
    a_re, a_im, wb_re, wb_im, wc_re, wc_im = _discretize(lam_re, lam_im, log_dt, b_re, b_im, c_re, c_im)
    y_ssm = _s5_branch(proj3, a_re, a_im, wb_re, wb_im, wc_re, wc_im, d_skip, w_glu.astype(BF16), b_glu, g_ssm)
    return y_pool.reshape(m, D_POOL), y_attn.reshape(m, D_ATTN), y_ssm.reshape(m, D_SSM)


def kernel(x, ln_g, w_in, w_pool, pool_scale, lam_re, lam_im, log_dt, b_re, b_im, c_re, c_im,
           d_skip, w_glu, b_glu, branch_g, w_out, final_g):
    bsz, l, d = x.shape
    assert ln_g.shape[0] == 2
    mixer_params = (w_pool, pool_scale, lam_re, lam_im, log_dt, b_re, b_im, c_re, c_im,
                    d_skip, w_glu, b_glu, branch_g)
    x0 = x.reshape(bsz * l, d)

    h = _rmsnorm(x0, ln_g[0], BF16)
    proj, w_in1 = _in_proj_f32w(h, w_in, 0)
    ys = _mixers(proj, bsz, *[p[0] for p in mixer_params])
    x1, xg, ssq, w_out1 = _out_proj_f32w(*ys, w_out, 0, x0, ln_g[1])

    proj = _in_proj_scaled(xg, ssq, w_in1)
    ys = _mixers(proj, bsz, *[p[1] for p in mixer_params])
    x2 = _out_proj(*ys, w_out1, x1)
    return _rmsnorm(x2, final_g, x.dtype).reshape(bsz, l, d)
```

```python
import functools
import math

import jax
import jax.numpy as jnp
from jax import lax
from jax.experimental import pallas as pl
from jax.experimental.pallas import tpu as pltpu

D_MODEL = 4096
D_POOL = 1024
D_ATTN = 2048
D_SSM = 1024
POOL_WINDOWS = (2, 4, 8, 16)
POOL_GROUP = 256
POOL_HALO = 16
HEAD_DIM = 128
N_HEADS = 16
SSM_GROUP = 16
SSM_STATE = 64
N_SSM_GROUPS = 64
D_IN = 12288
EPS = 1e-6

COL_PX_1024 = 0
COL_PGATE_1024 = 1
COL_Q_2048 = 1
COL_K_2048 = 2
COL_V_2048 = 3
COL_AGATE_2048 = 4
COL_SU_1024 = 10
COL_SGATE_1024 = 11

V7X_VMEM_BYTES = 64 * 1024 * 1024
SUBLANES = 8
LANES = 128

LOG2_E = 1.0 / math.log(2.0)
F32_EXP2_ZERO_ABOVE = 150.0
NO_BLOCK_CARRY = 1e30
MASKED_LOGIT = -1e30

BF16 = jnp.bfloat16
F32 = jnp.float32


def _vmem_limit(nbytes):
    return int(min(nbytes * 3 // 2 + (4 << 20), V7X_VMEM_BYTES - (6 << 20)))


def _silu(x):
    return x * jax.nn.sigmoid(x)


def _resident(*shape):
    return pl.BlockSpec(shape, lambda *_: (0,) * len(shape), pipeline_mode=pl.Buffered(1))


def _rmsnorm_kernel(x_ref, g_ref, o_ref):
    x = x_ref[...]
    y = x * lax.rsqrt(jnp.mean(x * x, axis=-1, keepdims=True) + EPS)
    o_ref[...] = (y * g_ref[...]).astype(o_ref.dtype)


def _rmsnorm(x2d, g, out_dtype, tm=512):
    m, d = x2d.shape
    return pl.pallas_call(
        _rmsnorm_kernel,
        grid=(m // tm,),
        in_specs=[pl.BlockSpec((tm, d), lambda i: (i, 0)),
                  pl.BlockSpec((1, d), lambda i: (0, 0))],
        out_specs=pl.BlockSpec((tm, d), lambda i: (i, 0)),
        out_shape=jax.ShapeDtypeStruct((m, d), out_dtype),
        compiler_params=pltpu.CompilerParams(
            dimension_semantics=("parallel",),
            vmem_limit_bytes=_vmem_limit(2 * tm * d * 8)),
        name="rmsnorm",
    )(x2d, g.reshape(1, d))


W_CHUNKS = 4
CAST_ROWS = 256


def _in_proj_f32w_kernel(a_ref, w_ref, wnext_ref, o_ref, wnext_bf_ref, wbf_ref, *, n_col, steps_per_chunk):
    c = pl.program_id(0)
    i = pl.program_id(1)
    chunk_rows = w_ref.shape[0]

    @pl.when(jnp.logical_and(c < n_col, i % steps_per_chunk == 0))
    def _():
        rows = pl.ds(pl.multiple_of((i // steps_per_chunk) * chunk_rows, chunk_rows), chunk_rows)
        wbf_ref[c % 2, rows, :] = w_ref[...].astype(BF16)

    @pl.when(c > 0)
    def _():
        o_ref[...] = jnp.dot(a_ref[...], wbf_ref[(c + 1) % 2], preferred_element_type=F32).astype(o_ref.dtype)
        wnext_bf_ref[...] = wnext_ref[...].astype(BF16)


def _in_proj_f32w(h, w_all, layer, tm=1024, tn=1024):
    m, k = h.shape
    n = w_all.shape[2]
    n_col, n_row = n // tn, m // tm
    steps_per_chunk = n_row // W_CHUNKS
    chunk_rows = k // W_CHUNKS
    side_rows = k // n_row
    col = lambda c: jnp.maximum(c - 1, 0)
    row = lambda c, i: jnp.where(c == 0, 0, i)
    est = (2 * (tm * k * 2 + chunk_rows * tn * 4 + tm * tn * 2 + side_rows * tn * 6) + 2 * k * tn * 2)
    kern = functools.partial(_in_proj_f32w_kernel, n_col=n_col, steps_per_chunk=steps_per_chunk)
    return pl.pallas_call(
        kern,
        grid=(n_col + 1, n_row),
        in_specs=[pl.BlockSpec((tm, k), lambda c, i: (row(c, i), 0)),
                  pl.BlockSpec((None, chunk_rows, tn),
                               lambda c, i: (layer, i // steps_per_chunk, jnp.minimum(c, n_col - 1))),
                  pl.BlockSpec((None, side_rows, tn), lambda c, i: (layer + 1, row(c, i), col(c)))],
        out_specs=(pl.BlockSpec((tm, tn), lambda c, i: (row(c, i), col(c))),
                   pl.BlockSpec((side_rows, tn), lambda c, i: (row(c, i), col(c)))),
        out_shape=(jax.ShapeDtypeStruct((m, n), BF16), jax.ShapeDtypeStruct((k, n), BF16)),
        scratch_shapes=[pltpu.VMEM((2, k, tn), BF16)],
        compiler_params=pltpu.CompilerParams(
            dimension_semantics=("arbitrary", "arbitrary"),
            vmem_limit_bytes=_vmem_limit(est)),
        name="in_proj_f32w",
    )(h, w_all, w_all)


def _in_proj_scaled_kernel(a_ref, ssq_ref, w_ref, o_ref, r_ref):
    @pl.when(pl.program_id(1) == 0)
    def _():
        ssq = jnp.sum(jnp.sum(ssq_ref[...], axis=0), axis=-1, keepdims=True)
        r_ref[...] = lax.rsqrt(ssq / D_MODEL + EPS)

    acc = jnp.dot(a_ref[...], w_ref[...], preferred_element_type=F32)
    o_ref[...] = (acc * r_ref[...]).astype(o_ref.dtype)


def _in_proj_scaled(xg, ssq, w, tm=1024, tn=1024):
    m, k = xg.shape
    n = w.shape[1]
    parts = ssq.shape[0]
    est = 2 * (tm * k * 2 + k * tn * 2 + tm * tn * 2 + parts * tm * LANES * 4) + tm * LANES * 4
    return pl.pallas_call(
        _in_proj_scaled_kernel,
        grid=(m // tm, n // tn),
        in_specs=[pl.BlockSpec((tm, k), lambda i, j: (i, 0)),
                  pl.BlockSpec((parts, tm, LANES), lambda i, j: (0, i, 0)),
                  pl.BlockSpec((k, tn), lambda i, j: (0, j))],
        out_specs=pl.BlockSpec((tm, tn), lambda i, j: (i, j)),
        out_shape=jax.ShapeDtypeStruct((m, n), BF16),
        scratch_shapes=[pltpu.VMEM((tm, 1), F32)],
        compiler_params=pltpu.CompilerParams(
            dimension_semantics=("parallel", "arbitrary"),
            vmem_limit_bytes=_vmem_limit(est)),
        name="in_proj_scaled",
    )(xg, ssq, w)


def _branch_matmul(yp_ref, ya_ref, ys_ref, w_ref):
    acc = jnp.dot(yp_ref[...], w_ref[0:D_POOL, :], preferred_element_type=F32)
    acc += jnp.dot(ya_ref[...], w_ref[D_POOL:D_POOL + D_ATTN, :], preferred_element_type=F32)
    return acc + jnp.dot(ys_ref[...], w_ref[D_POOL + D_ATTN:, :], preferred_element_type=F32)


def _out_proj_f32w_kernel(yp_ref, ya_ref, ys_ref, w_ref, x_ref, g_ref, wnext_ref,
                          xo_ref, xg_ref, ssq_ref, wnext_bf_ref, wbf_ref):
    @pl.when(pl.program_id(1) == 0)
    def _():
        wbf_ref[...] = w_ref[...].astype(BF16)

    x_new = x_ref[...] + _branch_matmul(yp_ref, ya_ref, ys_ref, wbf_ref)
    xo_ref[...] = x_new
    xg_ref[...] = (x_new * g_ref[...]).astype(xg_ref.dtype)
    sq = x_new * x_new
    ssq_ref[...] = functools.reduce(
        lambda a, b: a + b, [sq[:, c * LANES:(c + 1) * LANES] for c in range(sq.shape[1] // LANES)])
    wnext_bf_ref[...] = wnext_ref[...].astype(BF16)


def _out_proj_f32w(y_pool, y_attn, y_ssm, w_all, layer, x2d, next_g, tm=512, tn=1024):
    m = x2d.shape[0]
    k, n = w_all.shape[1:]
    assert (m // tm) * CAST_ROWS == k
    est = (2 * (tm * k * 2 + tm * tn * (4 + 4 + 2) + tm * LANES * 4 + CAST_ROWS * tn * 6) + k * tn * (4 + 2))
    tile = pl.BlockSpec((tm, tn), lambda j, i: (i, j))
    return pl.pallas_call(
        _out_proj_f32w_kernel,
        grid=(n // tn, m // tm),
        in_specs=[pl.BlockSpec((tm, D_POOL), lambda j, i: (i, 0)),
                  pl.BlockSpec((tm, D_ATTN), lambda j, i: (i, 0)),
                  pl.BlockSpec((tm, D_SSM), lambda j, i: (i, 0)),
                  pl.BlockSpec((None, k, tn), lambda j, i: (layer, 0, j), pipeline_mode=pl.Buffered(1)),
                  tile,
                  pl.BlockSpec((1, tn), lambda j, i: (0, j)),
                  pl.BlockSpec((None, CAST_ROWS, tn), lambda j, i: (layer + 1, i, j))],
        out_specs=(tile, tile,
                   pl.BlockSpec((None, tm, LANES), lambda j, i: (j, i, 0)),
                   pl.BlockSpec((CAST_ROWS, tn), lambda j, i: (i, j))),
        out_shape=(jax.ShapeDtypeStruct((m, n), F32), jax.ShapeDtypeStruct((m, n), BF16),
                   jax.ShapeDtypeStruct((n // tn, m, LANES), F32), jax.ShapeDtypeStruct((k, n), BF16)),
        scratch_shapes=[pltpu.VMEM((k, tn), BF16)],
        compiler_params=pltpu.CompilerParams(
            dimension_semantics=("parallel", "arbitrary"),
            vmem_limit_bytes=_vmem_limit(est)),
        name="out_proj_f32w",
    )(y_pool, y_attn, y_ssm, w_all, x2d, next_g.reshape(1, n), w_all)


def _out_proj_kernel(yp_ref, ya_ref, ys_ref, w_ref, x_ref, o_ref):
    o_ref[...] = x_ref[...] + _branch_matmul(yp_ref, ya_ref, ys_ref, w_ref)


def _out_proj(y_pool, y_attn, y_ssm, w, x2d, tm=1024, tn=512):
    m = x2d.shape[0]
    k, n = w.shape
    est = 2 * (tm * k * 2 + k * tn * 2 + 2 * tm * tn * 4)
    return pl.pallas_call(
        _out_proj_kernel,
        grid=(m // tm, n // tn),
        in_specs=[pl.BlockSpec((tm, D_POOL), lambda i, j: (i, 0)),
                  pl.BlockSpec((tm, D_ATTN), lambda i, j: (i, 0)),
                  pl.BlockSpec((tm, D_SSM), lambda i, j: (i, 0)),
                  pl.BlockSpec((k, tn), lambda i, j: (0, j)),
                  pl.BlockSpec((tm, tn), lambda i, j: (i, j))],
        out_specs=pl.BlockSpec((tm, tn), lambda i, j: (i, j)),
        out_shape=jax.ShapeDtypeStruct((m, n), F32),
        compiler_params=pltpu.CompilerParams(
            dimension_semantics=("parallel", "arbitrary"),
            vmem_limit_bytes=_vmem_limit(est)),
        name="out_proj",
    )(y_pool, y_attn, y_ssm, w, x2d)


def _pool_kernel(px_ref, halo_ref, gate_ref, w_ref, scale_ref, g_ref, o_ref, xs_ref, *, t_blk):
    i = pl.program_id(1)
    x = px_ref[0].astype(F32)
    xs_ref[0:POOL_HALO, :] = jnp.where(i > 0, halo_ref[0].astype(F32), 0.0)
    xs_ref[POOL_HALO:POOL_HALO + t_blk, :] = x
    pos = (i * t_blk + 1 + lax.broadcasted_iota(jnp.int32, (t_blk, 1), 0)).astype(F32)
    mixed = []
    for g, w in enumerate(POOL_WINDOWS):
        cols = slice(g * POOL_GROUP, (g + 1) * POOL_GROUP)
        s = x[:, cols]
        for j in range(1, w):
            s = s + xs_ref[POOL_HALO - j:POOL_HALO - j + t_blk, cols]
        pooled = s / jnp.minimum(pos, float(w)) - x[:, cols]
        mixed.append(jnp.dot(pooled.astype(BF16), w_ref[g], preferred_element_type=F32))
    y = jnp.concatenate(mixed, axis=1) * scale_ref[...]
    y = y * lax.rsqrt(jnp.mean(y * y, axis=-1, keepdims=True) + EPS) * g_ref[...]
    o_ref[0] = (y * _silu(gate_ref[0].astype(F32))).astype(o_ref.dtype)


def _pool_branch(proj3, w_pool, pool_scale, g_pool, t_blk=512):
    b, l, _ = proj3.shape
    halo_per_blk = t_blk // POOL_HALO
    kern = functools.partial(_pool_kernel, t_blk=t_blk)
    return pl.pallas_call(
        kern,
        grid=(b, l // t_blk),
        in_specs=[
            pl.BlockSpec((1, t_blk, D_POOL), lambda bi, i: (bi, i, COL_PX_1024)),
            pl.BlockSpec((1, POOL_HALO, D_POOL),
                         lambda bi, i: (bi, jnp.maximum(i * halo_per_blk - 1, 0), COL_PX_1024)),
            pl.BlockSpec((1, t_blk, D_POOL), lambda bi, i: (bi, i, COL_PGATE_1024)),
            pl.BlockSpec((len(POOL_WINDOWS), POOL_GROUP, POOL_GROUP), lambda bi, i: (0, 0, 0)),
            pl.BlockSpec((1, D_POOL), lambda bi, i: (0, 0)),
            pl.BlockSpec((1, D_POOL), lambda bi, i: (0, 0)),
        ],
        out_specs=pl.BlockSpec((1, t_blk, D_POOL), lambda bi, i: (bi, i, 0)),
        out_shape=jax.ShapeDtypeStruct((b, l, D_POOL), BF16),
        scratch_shapes=[pltpu.VMEM((POOL_HALO + t_blk, D_POOL), F32)],
        compiler_params=pltpu.CompilerParams(
            dimension_semantics=("parallel", "parallel"),
            vmem_limit_bytes=_vmem_limit(8 * t_blk * D_POOL * 4)),
        name="pool_branch",
    )(proj3, proj3, proj3, w_pool, pool_scale.reshape(1, D_POOL), g_pool.reshape(1, D_POOL))


def _attn_kernel(q_ref, k_ref, v_ref, gate_ref, g_ref, o_ref, *, t_blk, n_before):
    n_head = N_HEADS
    i = pl.program_id(1)
    row = lax.broadcasted_iota(jnp.int32, (t_blk, t_blk), 0)
    col = lax.broadcasted_iota(jnp.int32, (t_blk, t_blk), 1)
    later = jnp.where(row > col, 1.0, 0.0).astype(BF16)
    causal = col < row
    qs = [(q_ref[0, :, h * HEAD_DIM:(h + 1) * HEAD_DIM].astype(F32) * (HEAD_DIM ** -0.5 * LOG2_E)).astype(BF16)
          for h in range(n_head)]

    def tile(h, j, carry, diagonal):
        lanes = slice(h * HEAD_DIM, (h + 1) * HEAD_DIM)
        start = pl.multiple_of(j * t_blk, t_blk)
        kj = k_ref[0, pl.ds(start, t_blk), lanes]
        vj = v_ref[0, pl.ds(start, t_blk), lanes]
        z = lax.dot_general(qs[h], kj, (((1,), (1,)), ((), ())), preferred_element_type=F32)
        if diagonal:
            z = jnp.where(causal, z, MASKED_LOGIT)
        softplus = jnp.maximum(z, 0.0) + jnp.log(1.0 + jnp.exp2(-jnp.abs(z))) * LOG2_E
        suffix = jnp.dot(softplus.astype(BF16), later, preferred_element_type=F32)
        wts = jnp.exp2(z - softplus - suffix - carry)
        pv = jnp.dot(wts.astype(BF16), vj, preferred_element_type=F32)
        return pv, carry + jnp.sum(softplus, axis=1, keepdims=True)

    accs, carries = [], []
    for h in range(n_head):
        acc, carry = tile(h, i, jnp.zeros((t_blk, 1), F32), True)
        for d in range(1, n_before + 1):
            carry = jnp.where(i >= d, carry, NO_BLOCK_CARRY)
            pv, carry = tile(h, jnp.maximum(i - d, 0), carry, False)
            acc = acc + pv
        accs.append(acc)
        carries.append(carry)

    def alive_of(carries):
        return jnp.min(functools.reduce(jnp.minimum, carries)) < F32_EXP2_ZERO_ABOVE

    def cond(state):
        j, _, _, alive = state
        return jnp.logical_and(j >= 0, alive)

    def body(state):
        j, accs, carries, _ = state
        new_accs, new_carries = [], []
        for h in range(n_head):
            pv, carry = tile(h, j, carries[h], False)
            new_accs.append(accs[h] + pv)
            new_carries.append(carry)
        return j - 1, tuple(new_accs), tuple(new_carries), alive_of(new_carries)

    def finish(accs):
        y = jnp.concatenate(accs, axis=1)
        y = y * lax.rsqrt(jnp.mean(y * y, axis=-1, keepdims=True) + EPS) * g_ref[...]
        o_ref[0] = (y * _silu(gate_ref[0].astype(F32))).astype(o_ref.dtype)

    finish(accs)
    first = i - n_before - 1

    @pl.when(jnp.logical_and(first >= 0, alive_of(carries)))
    def _():
        _, more, _, _ = lax.while_loop(cond, body, (first, tuple(accs), tuple(carries), True))
        finish(more)


def _attn_branch(proj3, g_attn, t_blk=256, n_before=1):
    b, l, _ = proj3.shape
    kern = functools.partial(_attn_kernel, t_blk=t_blk, n_before=n_before)
    est = 2 * l * D_ATTN * 2 + 2 * 3 * t_blk * D_ATTN * 2 + N_HEADS * 12 * t_blk * t_blk * 4
    whole_seq = lambda col: pl.BlockSpec((1, l, D_ATTN), lambda bi, i: (bi, 0, col), pipeline_mode=pl.Buffered(1))
    q_rows = lambda col: pl.BlockSpec((1, t_blk, D_ATTN), lambda bi, i: (bi, i, col))
    return pl.pallas_call(
        kern,
        grid=(b, l // t_blk),
        in_specs=[q_rows(COL_Q_2048), whole_seq(COL_K_2048), whole_seq(COL_V_2048), q_rows(COL_AGATE_2048),
                  pl.BlockSpec((1, D_ATTN), lambda bi, i: (0, 0))],
        out_specs=pl.BlockSpec((1, t_blk, D_ATTN), lambda bi, i: (bi, i, 0)),
        out_shape=jax.ShapeDtypeStruct((b, l, D_ATTN), BF16),
        compiler_params=pltpu.CompilerParams(
            dimension_semantics=("parallel", "arbitrary"),
            vmem_limit_bytes=_vmem_limit(est)),
        name="attn_branch",
    )(proj3, proj3, proj3, proj3, g_attn.reshape(1, D_ATTN))


N_SLAB = 4
SLAB_IN = D_SSM // N_SLAB
SLAB_STATES = 1024
GROUPS_PER_SLAB = N_SSM_GROUPS // N_SLAB
N_CHUNK = SLAB_STATES // LANES
TILE_PITCH = SUBLANES + 1


def _block_diag(ref, rows_per_group, cols_per_group):
    n_rows = ref.shape[1]
    n_cols = GROUPS_PER_SLAB * cols_per_group
    row_group = lax.broadcasted_iota(jnp.int32, (n_rows, n_cols), 0) // rows_per_group
    col_group = lax.broadcasted_iota(jnp.int32, (n_rows, n_cols), 1) // cols_per_group
    return jnp.where((row_group == col_group)[None], jnp.concatenate([ref[...]] * GROUPS_PER_SLAB, axis=-1), 0.0)


def _discretize_kernel(lam_re_ref, lam_im_ref, log_dt_ref, b_re_ref, b_im_ref, c_re_ref, c_im_ref,
                       a_re_ref, a_im_ref, wb_re_ref, wb_im_ref, wc_re_ref, wc_im_ref):
    lam_re = lam_re_ref[...]
    lam_im = lam_im_ref[...]
    dt = jnp.exp(log_dt_ref[...])
    mag = jnp.exp(lam_re * dt)
    a_re = mag * jnp.cos(lam_im * dt)
    a_im = mag * jnp.sin(lam_im * dt)
    a_re_ref[...] = a_re
    a_im_ref[...] = a_im
    num_re = a_re - 1.0
    den = lam_re * lam_re + lam_im * lam_im
    c_re = ((num_re * lam_re + a_im * lam_im) / den)[:, None, :]
    c_im = ((a_im * lam_re - num_re * lam_im) / den)[:, None, :]
    b_re = _block_diag(b_re_ref, SSM_GROUP, SSM_STATE)
    b_im = _block_diag(b_im_ref, SSM_GROUP, SSM_STATE)
    wb_re_ref[...] = (c_re * b_re - c_im * b_im).astype(wb_re_ref.dtype)
    wb_im_ref[...] = (c_re * b_im + c_im * b_re).astype(wb_im_ref.dtype)
    wc_re_ref[...] = _block_diag(c_re_ref, SSM_STATE, SSM_GROUP).astype(wc_re_ref.dtype)
    wc_im_ref[...] = _block_diag(c_im_ref, SSM_STATE, SSM_GROUP).astype(wc_im_ref.dtype)


def _discretize(lam_re, lam_im, log_dt, b_re, b_im, c_re, c_im):
    shape4 = (N_SLAB, SLAB_STATES)
    log_dt_full = jnp.broadcast_to(log_dt[:, None], (N_SSM_GROUPS, SSM_STATE)).reshape(shape4)
    b_rows = lambda b: jnp.swapaxes(b, 1, 2).reshape(N_SLAB, SLAB_IN, SSM_STATE)
    c_rows = lambda c: jnp.swapaxes(c, 1, 2).reshape(N_SLAB, SLAB_STATES, SSM_GROUP)
    out_shape = (jax.ShapeDtypeStruct(shape4, F32), jax.ShapeDtypeStruct(shape4, F32),
                 jax.ShapeDtypeStruct((N_SLAB, SLAB_IN, SLAB_STATES), BF16),
                 jax.ShapeDtypeStruct((N_SLAB, SLAB_IN, SLAB_STATES), BF16),
                 jax.ShapeDtypeStruct((N_SLAB, SLAB_STATES, SLAB_IN), BF16),
                 jax.ShapeDtypeStruct((N_SLAB, SLAB_STATES, SLAB_IN), BF16))
    return pl.pallas_call(
        _discretize_kernel, out_shape=out_shape, name="s5_discretize",
        compiler_params=pltpu.CompilerParams(vmem_limit_bytes=_vmem_limit(32 << 20)),
    )(lam_re.reshape(shape4), lam_im.reshape(shape4), log_dt_full,
      b_rows(b_re), b_rows(b_im), c_rows(c_re), c_rows(c_im))


def _s5_kernel(u_ref, gate_ref, a_re_ref, a_im_ref, wb_re_ref, wb_im_ref, wc_re_ref, wc_im_ref,
               dskip_ref, wglu_ref, bglu_ref, g_ref, o_ref,
               s_re_ref, s_im_ref, x_re_ref, x_im_ref, *, t_blk, n_batch):
    n_row = n_batch * N_SLAB
    assert n_row == SUBLANES

    @pl.when(pl.program_id(0) == 0)
    def _():
        x_re_ref[...] = jnp.zeros_like(x_re_ref)
        x_im_ref[...] = jnp.zeros_like(x_im_ref)

    for j in range(N_SLAB):
        cols = slice(j * SLAB_IN, (j + 1) * SLAB_IN)
        uj = jnp.concatenate([u_ref[b, :, cols] for b in range(n_batch)], axis=0)
        bu_re = jnp.dot(uj, wb_re_ref[j], preferred_element_type=F32)
        bu_im = jnp.dot(uj, wb_im_ref[j], preferred_element_type=F32)
        for b in range(n_batch):
            rows = slice(b * t_blk, (b + 1) * t_blk)
            k = b * N_SLAB + j
            for c in range(N_CHUNK):
                lanes = slice(c * LANES, (c + 1) * LANES)
                s_re_ref[c, pl.ds(k, t_blk, stride=TILE_PITCH), :] = bu_re[rows, lanes]
                s_im_ref[c, pl.ds(k, t_blk, stride=TILE_PITCH), :] = bu_im[rows, lanes]

    a_re = [a_re_ref[:, c * LANES:(c + 1) * LANES] for c in range(N_CHUNK)]
    a_im = [a_im_ref[:, c * LANES:(c + 1) * LANES] for c in range(N_CHUNK)]
    x_re0 = tuple(x_re_ref[:, c * LANES:(c + 1) * LANES] for c in range(N_CHUNK))
    x_im0 = tuple(x_im_ref[:, c * LANES:(c + 1) * LANES] for c in range(N_CHUNK))

    def step(t, state):
        x_re, x_im = state
        rows = pl.ds(t * TILE_PITCH, n_row)
        new_re, new_im = [], []
        for c in range(N_CHUNK):
            nr = a_re[c] * x_re[c] - a_im[c] * x_im[c] + s_re_ref[c, rows, :]
            ni = a_re[c] * x_im[c] + a_im[c] * x_re[c] + s_im_ref[c, rows, :]
            s_re_ref[c, rows, :] = nr
            s_im_ref[c, rows, :] = ni
            new_re.append(nr)
            new_im.append(ni)
        return tuple(new_re), tuple(new_im)

    x_re, x_im = lax.fori_loop(0, t_blk, step, (x_re0, x_im0), unroll=4)
    for c in range(N_CHUNK):
        x_re_ref[:, c * LANES:(c + 1) * LANES] = x_re[c]
        x_im_ref[:, c * LANES:(c + 1) * LANES] = x_im[c]

    def states(s_ref, j):
        return jnp.concatenate(
            [jnp.concatenate([s_ref[c, pl.ds(b * N_SLAB + j, t_blk, stride=TILE_PITCH), :] for c in range(N_CHUNK)],
                             axis=1) for b in range(n_batch)], axis=0).astype(BF16)

    ys = [jnp.dot(states(s_re_ref, j), wc_re_ref[j], preferred_element_type=F32)
          - jnp.dot(states(s_im_ref, j), wc_im_ref[j], preferred_element_type=F32) for j in range(N_SLAB)]
    u_all = jnp.concatenate([u_ref[b] for b in range(n_batch)], axis=0).astype(F32)
    y = jnp.concatenate(ys, axis=1) + dskip_ref[...] * u_all
    h = jax.nn.gelu(y)
    glu = jnp.dot(h.astype(BF16), wglu_ref[...], preferred_element_type=F32) + bglu_ref[...]
    out = glu[:, :D_SSM] * jax.nn.sigmoid(glu[:, D_SSM:])
    out = out * lax.rsqrt(jnp.mean(out * out, axis=-1, keepdims=True) + EPS) * g_ref[...]
    for b in range(n_batch):
        rows = slice(b * t_blk, (b + 1) * t_blk)
        o_ref[b] = (out[rows] * _silu(gate_ref[b].astype(F32))).astype(o_ref.dtype)


def _s5_branch(proj3, a_re, a_im, wb_re, wb_im, wc_re, wc_im, d_skip, w_glu, b_glu, g_ssm, t_blk=256):
    b, l, _ = proj3.shape
    n_row = b * N_SLAB
    a_re_t = jnp.tile(a_re, (b, 1))
    a_im_t = jnp.tile(a_im, (b, 1))
    kern = functools.partial(_s5_kernel, t_blk=t_blk, n_batch=b)
    weights = 4 * N_SLAB * SLAB_IN * SLAB_STATES * 2 + D_SSM * 2 * D_SSM * 2
    est = (weights + 2 * N_CHUNK * t_blk * TILE_PITCH * LANES * 4 + 2 * 3 * b * t_blk * D_SSM * 2
           + 6 * b * t_blk * 2 * D_SSM * 4)
    return pl.pallas_call(
        kern,
        grid=(l // t_blk,),
        in_specs=[
            pl.BlockSpec((b, t_blk, D_SSM), lambda i: (0, i, COL_SU_1024)),
            pl.BlockSpec((b, t_blk, D_SSM), lambda i: (0, i, COL_SGATE_1024)),
            _resident(n_row, SLAB_STATES), _resident(n_row, SLAB_STATES),
            _resident(N_SLAB, SLAB_IN, SLAB_STATES), _resident(N_SLAB, SLAB_IN, SLAB_STATES),
            _resident(N_SLAB, SLAB_STATES, SLAB_IN), _resident(N_SLAB, SLAB_STATES, SLAB_IN),
            _resident(1, D_SSM), _resident(D_SSM, 2 * D_SSM), _resident(1, 2 * D_SSM), _resident(1, D_SSM),
        ],
        out_specs=pl.BlockSpec((b, t_blk, D_SSM), lambda i: (0, i, 0)),
        out_shape=jax.ShapeDtypeStruct((b, l, D_SSM), BF16),
        scratch_shapes=[pltpu.VMEM((N_CHUNK, t_blk * TILE_PITCH, LANES), F32),
                        pltpu.VMEM((N_CHUNK, t_blk * TILE_PITCH, LANES), F32),
                        pltpu.VMEM((n_row, SLAB_STATES), F32),
                        pltpu.VMEM((n_row, SLAB_STATES), F32)],
        compiler_params=pltpu.CompilerParams(
            dimension_semantics=("arbitrary",),
            vmem_limit_bytes=_vmem_limit(est)),
        name="s5_branch",
    )(proj3, proj3, a_re_t, a_im_t, wb_re, wb_im, wc_re, wc_im,
      d_skip.reshape(1, D_SSM), w_glu, b_glu.reshape(1, 2 * D_SSM), g_ssm.reshape(1, D_SSM))


def _mixers(proj, bsz, w_pool, pool_scale, lam_re, lam_im, log_dt,
            b_re, b_im, c_re, c_im, d_skip, w_glu, b_glu, branch_g):
    m = proj.shape[0]
    proj3 = proj.reshape(bsz, m // bsz, D_IN)
    g_pool = branch_g[:D_POOL]
    g_attn = branch_g[D_POOL:D_POOL + D_ATTN]
    g_ssm = branch_g[D_POOL + D_ATTN:]
    y_pool = _pool_branch(proj3, w_pool.astype(BF16), pool_scale, g_pool)
    y_attn = _attn_branch(proj3, g_attn)
    a_re, a_im, wb_re, wb_im, wc_re, wc_im = _discretize(lam_re, lam_im, log_dt, b_re, b_im, c_re, c_im)
    y_ssm = _s5_branch(proj3, a_re, a_im, wb_re, wb_im, wc_re, wc_im, d_skip, w_glu.astype(BF16), b_glu, g_ssm)
    return y_pool.reshape(m, D_POOL), y_attn.reshape(m, D_ATTN), y_ssm.reshape(m, D_SSM)


def kernel(x, ln_g, w_in, w_pool, pool_scale, lam_re, lam_im, log_dt, b_re, b_im, c_re, c_im,
           d_skip, w_glu, b_glu, branch_g, w_out, final_g):
    bsz, l, d = x.shape
    assert ln_g.shape[0] == 2
    mixer_params = (w_pool, pool_scale, lam_re, lam_im, log_dt, b_re, b_im, c_re, c_im,
                    d_skip, w_glu, b_glu, branch_g)
    x0 = x.reshape(bsz * l, d)

    h = _rmsnorm(x0, ln_g[0], BF16)
    proj, w_in1 = _in_proj_f32w(h, w_in, 0)
    ys = _mixers(proj, bsz, *[p[0] for p in mixer_params])
    x1, xg, ssq, w_out1 = _out_proj_f32w(*ys, w_out, 0, x0, ln_g[1])

    proj = _in_proj_scaled(xg, ssq, w_in1)
    ys = _mixers(proj, bsz, *[p[1] for p in mixer_params])
    x2 = _out_proj(*ys, w_out1, x1)
    return _rmsnorm(x2, final_g, x.dtype).reshape(bsz, l, d)
```

```python
import functools
import math

import jax
import jax.numpy as jnp
from jax import lax
from jax.experimental import pallas as pl
from jax.experimental.pallas import tpu as pltpu

D_MODEL = 4096
D_POOL = 1024
D_ATTN = 2048
D_SSM = 1024
POOL_WINDOWS = (2, 4, 8, 16)
POOL_GROUP = 256
POOL_SUB = 256
HEAD_DIM = 128
N_HEADS = 16
SSM_GROUP = 16
SSM_STATE = 64
N_SSM_GROUPS = 64
D_IN = 12288
EPS = 1e-6

COL_PX_1024 = 0
COL_PGATE_1024 = 1
COL_Q_2048 = 1
COL_K_2048 = 2
COL_V_2048 = 3
COL_AGATE_2048 = 4
COL_SU_1024 = 10
COL_SGATE_1024 = 11

V7X_VMEM_BYTES = 64 * 1024 * 1024
SUBLANES = 8
LANES = 128

LOG2_E = 1.0 / math.log(2.0)
F32_EXP2_ZERO_ABOVE = 150.0
NO_BLOCK_CARRY = 1e30
MASKED_LOGIT = -1e30

BF16 = jnp.bfloat16
F32 = jnp.float32


def _vmem_limit(nbytes):
    return int(min(nbytes * 3 // 2 + (4 << 20), V7X_VMEM_BYTES - (6 << 20)))


def _silu(x):
    return x * jax.nn.sigmoid(x)


def _resident(*shape):
    return pl.BlockSpec(shape, lambda *_: (0,) * len(shape), pipeline_mode=pl.Buffered(1))


def _rmsnorm_kernel(x_ref, g_ref, o_ref):
    x = x_ref[...]
    y = x * lax.rsqrt(jnp.mean(x * x, axis=-1, keepdims=True) + EPS)
    o_ref[...] = (y * g_ref[...]).astype(o_ref.dtype)


def _rmsnorm(x2d, g, out_dtype, tm=512):
    m, d = x2d.shape
    return pl.pallas_call(
        _rmsnorm_kernel,
        grid=(m // tm,),
        in_specs=[pl.BlockSpec((tm, d), lambda i: (i, 0)),
                  pl.BlockSpec((1, d), lambda i: (0, 0))],
        out_specs=pl.BlockSpec((tm, d), lambda i: (i, 0)),
        out_shape=jax.ShapeDtypeStruct((m, d), out_dtype),
        compiler_params=pltpu.CompilerParams(
            dimension_semantics=("parallel",),
            vmem_limit_bytes=_vmem_limit(2 * tm * d * 8)),
        name="rmsnorm",
    )(x2d, g.reshape(1, d))


W_CHUNKS = 4
CAST_ROWS = 256


def _in_proj_f32w_kernel(a_ref, w_ref, wnext_ref, o_ref, wnext_bf_ref, wbf_ref, *, n_col, steps_per_chunk):
    c = pl.program_id(0)
    i = pl.program_id(1)
    chunk_rows = w_ref.shape[0]

    @pl.when(jnp.logical_and(c < n_col, i % steps_per_chunk == 0))
    def _():
        rows = pl.ds(pl.multiple_of((i // steps_per_chunk) * chunk_rows, chunk_rows), chunk_rows)
        wbf_ref[c % 2, rows, :] = w_ref[...].astype(BF16)

    @pl.when(c > 0)
    def _():
        o_ref[...] = jnp.dot(a_ref[...], wbf_ref[(c + 1) % 2], preferred_element_type=F32).astype(o_ref.dtype)
        wnext_bf_ref[...] = wnext_ref[...].astype(BF16)


def _in_proj_f32w(h, w_all, layer, tm=1024, tn=1024):
    m, k = h.shape
    n = w_all.shape[2]
    n_col, n_row = n // tn, m // tm
    steps_per_chunk = n_row // W_CHUNKS
    chunk_rows = k // W_CHUNKS
    side_rows = k // n_row
    col = lambda c: jnp.maximum(c - 1, 0)
    row = lambda c, i: jnp.where(c == 0, 0, i)
    est = (2 * (tm * k * 2 + chunk_rows * tn * 4 + tm * tn * 2 + side_rows * tn * 6) + 2 * k * tn * 2)
    kern = functools.partial(_in_proj_f32w_kernel, n_col=n_col, steps_per_chunk=steps_per_chunk)
    return pl.pallas_call(
        kern,
        grid=(n_col + 1, n_row),
        in_specs=[pl.BlockSpec((tm, k), lambda c, i: (row(c, i), 0)),
                  pl.BlockSpec((None, chunk_rows, tn),
                               lambda c, i: (layer, i // steps_per_chunk, jnp.minimum(c, n_col - 1))),
                  pl.BlockSpec((None, side_rows, tn), lambda c, i: (layer + 1, row(c, i), col(c)))],
        out_specs=(pl.BlockSpec((tm, tn), lambda c, i: (row(c, i), col(c))),
                   pl.BlockSpec((side_rows, tn), lambda c, i: (row(c, i), col(c)))),
        out_shape=(jax.ShapeDtypeStruct((m, n), BF16), jax.ShapeDtypeStruct((k, n), BF16)),
        scratch_shapes=[pltpu.VMEM((2, k, tn), BF16)],
        compiler_params=pltpu.CompilerParams(
            dimension_semantics=("arbitrary", "arbitrary"),
            vmem_limit_bytes=_vmem_limit(est)),
        name="in_proj_f32w",
    )(h, w_all, w_all)


def _in_proj_scaled_kernel(a_ref, ssq_ref, w_ref, o_ref, r_ref):
    @pl.when(pl.program_id(1) == 0)
    def _():
        ssq = jnp.sum(jnp.sum(ssq_ref[...], axis=0), axis=-1, keepdims=True)
        r_ref[...] = lax.rsqrt(ssq / D_MODEL + EPS)

    acc = jnp.dot(a_ref[...], w_ref[...], preferred_element_type=F32)
    o_ref[...] = (acc * r_ref[...]).astype(o_ref.dtype)


def _in_proj_scaled(xg, ssq, w, tm=1024, tn=1024):
    m, k = xg.shape
    n = w.shape[1]
    parts = ssq.shape[0]
    est = 2 * (tm * k * 2 + k * tn * 2 + tm * tn * 2 + parts * tm * LANES * 4) + tm * LANES * 4
    return pl.pallas_call(
        _in_proj_scaled_kernel,
        grid=(m // tm, n // tn),
        in_specs=[pl.BlockSpec((tm, k), lambda i, j: (i, 0)),
                  pl.BlockSpec((parts, tm, LANES), lambda i, j: (0, i, 0)),
                  pl.BlockSpec((k, tn), lambda i, j: (0, j))],
        out_specs=pl.BlockSpec((tm, tn), lambda i, j: (i, j)),
        out_shape=jax.ShapeDtypeStruct((m, n), BF16),
        scratch_shapes=[pltpu.VMEM((tm, 1), F32)],
        compiler_params=pltpu.CompilerParams(
            dimension_semantics=("parallel", "arbitrary"),
            vmem_limit_bytes=_vmem_limit(est)),
        name="in_proj_scaled",
    )(xg, ssq, w)


def _branch_matmul(yp_ref, ya_ref, ys_ref, w_ref):
    acc = jnp.dot(yp_ref[...], w_ref[0:D_POOL, :], preferred_element_type=F32)
    acc += jnp.dot(ya_ref[...], w_ref[D_POOL:D_POOL + D_ATTN, :], preferred_element_type=F32)
    return acc + jnp.dot(ys_ref[...], w_ref[D_POOL + D_ATTN:, :], preferred_element_type=F32)


def _out_proj_f32w_kernel(yp_ref, ya_ref, ys_ref, w_ref, x_ref, g_ref, wnext_ref,
                          xo_ref, xg_ref, ssq_ref, wnext_bf_ref, wbf_ref):
    @pl.when(pl.program_id(1) == 0)
    def _():
        wbf_ref[...] = w_ref[...].astype(BF16)

    x_new = x_ref[...] + _branch_matmul(yp_ref, ya_ref, ys_ref, wbf_ref)
    xo_ref[...] = x_new
    xg_ref[...] = (x_new * g_ref[...]).astype(xg_ref.dtype)
    sq = x_new * x_new
    ssq_ref[...] = functools.reduce(
        lambda a, b: a + b, [sq[:, c * LANES:(c + 1) * LANES] for c in range(sq.shape[1] // LANES)])
    wnext_bf_ref[...] = wnext_ref[...].astype(BF16)


def _out_proj_f32w(y_pool, y_attn, y_ssm, w_all, layer, x2d, next_g, tm=512, tn=1024):
    m = x2d.shape[0]
    k, n = w_all.shape[1:]
    assert (m // tm) * CAST_ROWS == k
    est = (2 * (tm * k * 2 + tm * tn * (4 + 4 + 2) + tm * LANES * 4 + CAST_ROWS * tn * 6) + k * tn * (4 + 2))
    tile = pl.BlockSpec((tm, tn), lambda j, i: (i, j))
    return pl.pallas_call(
        _out_proj_f32w_kernel,
        grid=(n // tn, m // tm),
        in_specs=[pl.BlockSpec((tm, D_POOL), lambda j, i: (i, 0)),
                  pl.BlockSpec((tm, D_ATTN), lambda j, i: (i, 0)),
                  pl.BlockSpec((tm, D_SSM), lambda j, i: (i, 0)),
                  pl.BlockSpec((None, k, tn), lambda j, i: (layer, 0, j), pipeline_mode=pl.Buffered(1)),
                  tile,
                  pl.BlockSpec((1, tn), lambda j, i: (0, j)),
                  pl.BlockSpec((None, CAST_ROWS, tn), lambda j, i: (layer + 1, i, j))],
        out_specs=(tile, tile,
                   pl.BlockSpec((None, tm, LANES), lambda j, i: (j, i, 0)),
                   pl.BlockSpec((CAST_ROWS, tn), lambda j, i: (i, j))),
        out_shape=(jax.ShapeDtypeStruct((m, n), F32), jax.ShapeDtypeStruct((m, n), BF16),
                   jax.ShapeDtypeStruct((n // tn, m, LANES), F32), jax.ShapeDtypeStruct((k, n), BF16)),
        scratch_shapes=[pltpu.VMEM((k, tn), BF16)],
        compiler_params=pltpu.CompilerParams(
            dimension_semantics=("parallel", "arbitrary"),
            vmem_limit_bytes=_vmem_limit(est)),
        name="out_proj_f32w",
    )(y_pool, y_attn, y_ssm, w_all, x2d, next_g.reshape(1, n), w_all)


def _out_proj_kernel(yp_ref, ya_ref, ys_ref, w_ref, x_ref, o_ref):
    o_ref[...] = x_ref[...] + _branch_matmul(yp_ref, ya_ref, ys_ref, w_ref)


def _out_proj(y_pool, y_attn, y_ssm, w, x2d, tm=1024, tn=1024):
    m = x2d.shape[0]
    k, n = w.shape
    est = 2 * (tm * k * 2 + k * tn * 2 + 2 * tm * tn * 4)
    return pl.pallas_call(
        _out_proj_kernel,
        grid=(m // tm, n // tn),
        in_specs=[pl.BlockSpec((tm, D_POOL), lambda i, j: (i, 0)),
                  pl.BlockSpec((tm, D_ATTN), lambda i, j: (i, 0)),
                  pl.BlockSpec((tm, D_SSM), lambda i, j: (i, 0)),
                  pl.BlockSpec((k, tn), lambda i, j: (0, j)),
                  pl.BlockSpec((tm, tn), lambda i, j: (i, j))],
        out_specs=pl.BlockSpec((tm, tn), lambda i, j: (i, j)),
        out_shape=jax.ShapeDtypeStruct((m, n), F32),
        compiler_params=pltpu.CompilerParams(
            dimension_semantics=("parallel", "arbitrary"),
            vmem_limit_bytes=_vmem_limit(est)),
        name="out_proj",
    )(y_pool, y_attn, y_ssm, w, x2d)


def _pool_kernel(px_ref, prev_ref, gate_ref, w_ref, scale_ref, g_ref, o_ref, *, t_blk):
    i = pl.program_id(1)
    x_bf = px_ref[0]
    x = x_bf.astype(F32)
    x_ext = jnp.concatenate([jnp.where(i > 0, prev_ref[0], jnp.zeros_like(prev_ref[0])), x_bf], axis=0)
    pos = (i * t_blk + 1 + lax.broadcasted_iota(jnp.int32, (t_blk, 1), 0)).astype(F32)
    out_row = lax.broadcasted_iota(jnp.int32, (POOL_SUB, 2 * POOL_SUB), 0)
    in_row = lax.broadcasted_iota(jnp.int32, (POOL_SUB, 2 * POOL_SUB), 1)
    lag = out_row + POOL_SUB - in_row
    mixed = []
    for g, w in enumerate(POOL_WINDOWS):
        cols = slice(g * POOL_GROUP, (g + 1) * POOL_GROUP)
        band = jnp.where(lag >= 0, jnp.where(lag < w, 1.0, 0.0), 0.0).astype(BF16)
        s = jnp.concatenate(
            [jnp.dot(band, x_ext[r:r + 2 * POOL_SUB, cols], preferred_element_type=F32)
             for r in range(0, t_blk, POOL_SUB)], axis=0)
        pooled = s / jnp.minimum(pos, float(w)) - x[:, cols]
        mixed.append(jnp.dot(pooled.astype(BF16), w_ref[g], preferred_element_type=F32))
    y = jnp.concatenate(mixed, axis=1) * scale_ref[...]
    y = y * lax.rsqrt(jnp.mean(y * y, axis=-1, keepdims=True) + EPS) * g_ref[...]
    o_ref[0] = (y * _silu(gate_ref[0].astype(F32))).astype(o_ref.dtype)


def _pool_branch(proj3, w_pool, pool_scale, g_pool, t_blk=512):
    b, l, _ = proj3.shape
    assert max(POOL_WINDOWS) <= POOL_SUB and t_blk % POOL_SUB == 0
    sub_per_blk = t_blk // POOL_SUB
    kern = functools.partial(_pool_kernel, t_blk=t_blk)
    return pl.pallas_call(
        kern,
        grid=(b, l // t_blk),
        in_specs=[
            pl.BlockSpec((1, t_blk, D_POOL), lambda bi, i: (bi, i, COL_PX_1024)),
            pl.BlockSpec((1, POOL_SUB, D_POOL),
                         lambda bi, i: (bi, jnp.maximum(i * sub_per_blk - 1, 0), COL_PX_1024)),
            pl.BlockSpec((1, t_blk, D_POOL), lambda bi, i: (bi, i, COL_PGATE_1024)),
            pl.BlockSpec((len(POOL_WINDOWS), POOL_GROUP, POOL_GROUP), lambda bi, i: (0, 0, 0)),
            pl.BlockSpec((1, D_POOL), lambda bi, i: (0, 0)),
            pl.BlockSpec((1, D_POOL), lambda bi, i: (0, 0)),
        ],
        out_specs=pl.BlockSpec((1, t_blk, D_POOL), lambda bi, i: (bi, i, 0)),
        out_shape=jax.ShapeDtypeStruct((b, l, D_POOL), BF16),
        compiler_params=pltpu.CompilerParams(
            dimension_semantics=("parallel", "parallel"),
            vmem_limit_bytes=_vmem_limit(8 * t_blk * D_POOL * 4)),
        name="pool_branch",
    )(proj3, proj3, proj3, w_pool, pool_scale.reshape(1, D_POOL), g_pool.reshape(1, D_POOL))


def _attn_kernel(q_ref, k_ref, v_ref, gate_ref, g_ref, o_ref, *, t_blk, n_before):
    n_head = N_HEADS
    i = pl.program_id(1)
    row = lax.broadcasted_iota(jnp.int32, (t_blk, t_blk), 0)
    col = lax.broadcasted_iota(jnp.int32, (t_blk, t_blk), 1)
    later = jnp.where(row > col, 1.0, 0.0).astype(BF16)
    causal = col < row
    qs = [(q_ref[0, :, h * HEAD_DIM:(h + 1) * HEAD_DIM].astype(F32) * (HEAD_DIM ** -0.5 * LOG2_E)).astype(BF16)
          for h in range(n_head)]

    def tile(h, j, carry, diagonal):
        lanes = slice(h * HEAD_DIM, (h + 1) * HEAD_DIM)
        start = pl.multiple_of(j * t_blk, t_blk)
        kj = k_ref[0, pl.ds(start, t_blk), lanes]
        vj = v_ref[0, pl.ds(start, t_blk), lanes]
        z = lax.dot_general(qs[h], kj, (((1,), (1,)), ((), ())), preferred_element_type=F32)
        if diagonal:
            z = jnp.where(causal, z, MASKED_LOGIT)
        softplus = jnp.maximum(z, 0.0) + jnp.log(1.0 + jnp.exp2(-jnp.abs(z))) * LOG2_E
        suffix = jnp.dot(softplus.astype(BF16), later, preferred_element_type=F32)
        wts = jnp.exp2(z - softplus - suffix - carry)
        pv = jnp.dot(wts.astype(BF16), vj, preferred_element_type=F32)
        return pv, carry + jnp.sum(softplus, axis=1, keepdims=True)

    accs, carries = [], []
    for h in range(n_head):
        acc, carry = tile(h, i, jnp.zeros((t_blk, 1), F32), True)
        for d in range(1, n_before + 1):
            carry = jnp.where(i >= d, carry, NO_BLOCK_CARRY)
            pv, carry = tile(h, jnp.maximum(i - d, 0), carry, False)
            acc = acc + pv
        accs.append(acc)
        carries.append(carry)

    def alive_of(carries):
        return jnp.min(functools.reduce(jnp.minimum, carries)) < F32_EXP2_ZERO_ABOVE

    def cond(state):
        j, _, _, alive = state
        return jnp.logical_and(j >= 0, alive)

    def body(state):
        j, accs, carries, _ = state
        new_accs, new_carries = [], []
        for h in range(n_head):
            pv, carry = tile(h, j, carries[h], False)
            new_accs.append(accs[h] + pv)
            new_carries.append(carry)
        return j - 1, tuple(new_accs), tuple(new_carries), alive_of(new_carries)

    def finish(accs):
        y = jnp.concatenate(accs, axis=1)
        y = y * lax.rsqrt(jnp.mean(y * y, axis=-1, keepdims=True) + EPS) * g_ref[...]
        o_ref[0] = (y * _silu(gate_ref[0].astype(F32))).astype(o_ref.dtype)

    finish(accs)
    first = i - n_before - 1

    @pl.when(jnp.logical_and(first >= 0, alive_of(carries)))
    def _():
        _, more, _, _ = lax.while_loop(cond, body, (first, tuple(accs), tuple(carries), True))
        finish(more)


def _attn_branch(proj3, g_attn, t_blk=256, n_before=1):
    b, l, _ = proj3.shape
    kern = functools.partial(_attn_kernel, t_blk=t_blk, n_before=n_before)
    est = 2 * l * D_ATTN * 2 + 2 * 3 * t_blk * D_ATTN * 2 + N_HEADS * 12 * t_blk * t_blk * 4
    whole_seq = lambda col: pl.BlockSpec((1, l, D_ATTN), lambda bi, i: (bi, 0, col), pipeline_mode=pl.Buffered(1))
    q_rows = lambda col: pl.BlockSpec((1, t_blk, D_ATTN), lambda bi, i: (bi, i, col))
    return pl.pallas_call(
        kern,
        grid=(b, l // t_blk),
        in_specs=[q_rows(COL_Q_2048), whole_seq(COL_K_2048), whole_seq(COL_V_2048), q_rows(COL_AGATE_2048),
                  pl.BlockSpec((1, D_ATTN), lambda bi, i: (0, 0))],
        out_specs=pl.BlockSpec((1, t_blk, D_ATTN), lambda bi, i: (bi, i, 0)),
        out_shape=jax.ShapeDtypeStruct((b, l, D_ATTN), BF16),
        compiler_params=pltpu.CompilerParams(
            dimension_semantics=("parallel", "arbitrary"),
            vmem_limit_bytes=_vmem_limit(est)),
        name="attn_branch",
    )(proj3, proj3, proj3, proj3, g_attn.reshape(1, D_ATTN))


N_SLAB = 4
SLAB_IN = D_SSM // N_SLAB
SLAB_STATES = 1024
GROUPS_PER_SLAB = N_SSM_GROUPS // N_SLAB
N_CHUNK = SLAB_STATES // LANES
TILE_PITCH = SUBLANES + 1


def _block_diag(ref, rows_per_group, cols_per_group):
    n_rows = ref.shape[1]
    n_cols = GROUPS_PER_SLAB * cols_per_group
    row_group = lax.broadcasted_iota(jnp.int32, (n_rows, n_cols), 0) // rows_per_group
    col_group = lax.broadcasted_iota(jnp.int32, (n_rows, n_cols), 1) // cols_per_group
    return jnp.where((row_group == col_group)[None], jnp.concatenate([ref[...]] * GROUPS_PER_SLAB, axis=-1), 0.0)


def _discretize_kernel(lam_re_ref, lam_im_ref, log_dt_ref, b_re_ref, b_im_ref, c_re_ref, c_im_ref,
                       a_re_ref, a_im_ref, wb_re_ref, wb_im_ref, wc_re_ref, wc_im_ref):
    lam_re = lam_re_ref[...]
    lam_im = lam_im_ref[...]
    dt = jnp.exp(log_dt_ref[...])
    mag = jnp.exp(lam_re * dt)
    a_re = mag * jnp.cos(lam_im * dt)
    a_im = mag * jnp.sin(lam_im * dt)
    a_re_ref[...] = a_re
    a_im_ref[...] = a_im
    num_re = a_re - 1.0
    den = lam_re * lam_re + lam_im * lam_im
    c_re = ((num_re * lam_re + a_im * lam_im) / den)[:, None, :]
    c_im = ((a_im * lam_re - num_re * lam_im) / den)[:, None, :]
    b_re = _block_diag(b_re_ref, SSM_GROUP, SSM_STATE)
    b_im = _block_diag(b_im_ref, SSM_GROUP, SSM_STATE)
    wb_re_ref[...] = (c_re * b_re - c_im * b_im).astype(wb_re_ref.dtype)
    wb_im_ref[...] = (c_re * b_im + c_im * b_re).astype(wb_im_ref.dtype)
    wc_re_ref[...] = _block_diag(c_re_ref, SSM_STATE, SSM_GROUP).astype(wc_re_ref.dtype)
    wc_im_ref[...] = _block_diag(c_im_ref, SSM_STATE, SSM_GROUP).astype(wc_im_ref.dtype)


def _discretize(lam_re, lam_im, log_dt, b_re, b_im, c_re, c_im):
    shape4 = (N_SLAB, SLAB_STATES)
    log_dt_full = jnp.broadcast_to(log_dt[:, None], (N_SSM_GROUPS, SSM_STATE)).reshape(shape4)
    b_rows = lambda b: jnp.swapaxes(b, 1, 2).reshape(N_SLAB, SLAB_IN, SSM_STATE)
    c_rows = lambda c: jnp.swapaxes(c, 1, 2).reshape(N_SLAB, SLAB_STATES, SSM_GROUP)
    out_shape = (jax.ShapeDtypeStruct(shape4, F32), jax.ShapeDtypeStruct(shape4, F32),
                 jax.ShapeDtypeStruct((N_SLAB, SLAB_IN, SLAB_STATES), BF16),
                 jax.ShapeDtypeStruct((N_SLAB, SLAB_IN, SLAB_STATES), BF16),
                 jax.ShapeDtypeStruct((N_SLAB, SLAB_STATES, SLAB_IN), BF16),
                 jax.ShapeDtypeStruct((N_SLAB, SLAB_STATES, SLAB_IN), BF16))
    return pl.pallas_call(
        _discretize_kernel, out_shape=out_shape, name="s5_discretize",
        compiler_params=pltpu.CompilerParams(vmem_limit_bytes=_vmem_limit(32 << 20)),
    )(lam_re.reshape(shape4), lam_im.reshape(shape4), log_dt_full,
      b_rows(b_re), b_rows(b_im), c_rows(c_re), c_rows(c_im))


def _s5_kernel(u_ref, gate_ref, a_re_ref, a_im_ref, wb_re_ref, wb_im_ref, wc_re_ref, wc_im_ref,
               dskip_ref, wglu_ref, bglu_ref, g_ref, o_ref,
               s_re_ref, s_im_ref, x_re_ref, x_im_ref, *, t_blk, n_batch):
    n_row = n_batch * N_SLAB
    assert n_row == SUBLANES

    @pl.when(pl.program_id(0) == 0)
    def _():
        x_re_ref[...] = jnp.zeros_like(x_re_ref)
        x_im_ref[...] = jnp.zeros_like(x_im_ref)

    for j in range(N_SLAB):
        cols = slice(j * SLAB_IN, (j + 1) * SLAB_IN)
        uj = jnp.concatenate([u_ref[b, :, cols] for b in range(n_batch)], axis=0)
        bu_re = jnp.dot(uj, wb_re_ref[j], preferred_element_type=F32)
        bu_im = jnp.dot(uj, wb_im_ref[j], preferred_element_type=F32)
        for b in range(n_batch):
            rows = slice(b * t_blk, (b + 1) * t_blk)
            k = b * N_SLAB + j
            for c in range(N_CHUNK):
                lanes = slice(c * LANES, (c + 1) * LANES)
                s_re_ref[c, pl.ds(k, t_blk, stride=TILE_PITCH), :] = bu_re[rows, lanes]
                s_im_ref[c, pl.ds(k, t_blk, stride=TILE_PITCH), :] = bu_im[rows, lanes]

    a_re = [a_re_ref[:, c * LANES:(c + 1) * LANES] for c in range(N_CHUNK)]
    a_im = [a_im_ref[:, c * LANES:(c + 1) * LANES] for c in range(N_CHUNK)]
    x_re0 = tuple(x_re_ref[:, c * LANES:(c + 1) * LANES] for c in range(N_CHUNK))
    x_im0 = tuple(x_im_ref[:, c * LANES:(c + 1) * LANES] for c in range(N_CHUNK))

    def step(t, state):
        x_re, x_im = state
        rows = pl.ds(t * TILE_PITCH, n_row)
        new_re, new_im = [], []
        for c in range(N_CHUNK):
            nr = a_re[c] * x_re[c] - a_im[c] * x_im[c] + s_re_ref[c, rows, :]
            ni = a_re[c] * x_im[c] + a_im[c] * x_re[c] + s_im_ref[c, rows, :]
            s_re_ref[c, rows, :] = nr
            s_im_ref[c, rows, :] = ni
            new_re.append(nr)
            new_im.append(ni)
        return tuple(new_re), tuple(new_im)

    x_re, x_im = lax.fori_loop(0, t_blk, step, (x_re0, x_im0), unroll=4)
    for c in range(N_CHUNK):
        x_re_ref[:, c * LANES:(c + 1) * LANES] = x_re[c]
        x_im_ref[:, c * LANES:(c + 1) * LANES] = x_im[c]

    def states(s_ref, j):
        return jnp.concatenate(
            [jnp.concatenate([s_ref[c, pl.ds(b * N_SLAB + j, t_blk, stride=TILE_PITCH), :] for c in range(N_CHUNK)],
                             axis=1) for b in range(n_batch)], axis=0).astype(BF16)

    ys = [jnp.dot(states(s_re_ref, j), wc_re_ref[j], preferred_element_type=F32)
          - jnp.dot(states(s_im_ref, j), wc_im_ref[j], preferred_element_type=F32) for j in range(N_SLAB)]
    u_all = jnp.concatenate([u_ref[b] for b in range(n_batch)], axis=0).astype(F32)
    y = jnp.concatenate(ys, axis=1) + dskip_ref[...] * u_all
    h = jax.nn.gelu(y)
    glu = jnp.dot(h.astype(BF16), wglu_ref[...], preferred_element_type=F32) + bglu_ref[...]
    out = glu[:, :D_SSM] * jax.nn.sigmoid(glu[:, D_SSM:])
    out = out * lax.rsqrt(jnp.mean(out * out, axis=-1, keepdims=True) + EPS) * g_ref[...]
    for b in range(n_batch):
        rows = slice(b * t_blk, (b + 1) * t_blk)
        o_ref[b] = (out[rows] * _silu(gate_ref[b].astype(F32))).astype(o_ref.dtype)


def _s5_branch(proj3, a_re, a_im, wb_re, wb_im, wc_re, wc_im, d_skip, w_glu, b_glu, g_ssm, t_blk=256):
    b, l, _ = proj3.shape
    n_row = b * N_SLAB
    a_re_t = jnp.tile(a_re, (b, 1))
    a_im_t = jnp.tile(a_im, (b, 1))
    kern = functools.partial(_s5_kernel, t_blk=t_blk, n_batch=b)
    weights = 4 * N_SLAB * SLAB_IN * SLAB_STATES * 2 + D_SSM * 2 * D_SSM * 2
    est = (weights + 2 * N_CHUNK * t_blk * TILE_PITCH * LANES * 4 + 2 * 3 * b * t_blk * D_SSM * 2
           + 6 * b * t_blk * 2 * D_SSM * 4)
    return pl.pallas_call(
        kern,
        grid=(l // t_blk,),
        in_specs=[
            pl.BlockSpec((b, t_blk, D_SSM), lambda i: (0, i, COL_SU_1024)),
            pl.BlockSpec((b, t_blk, D_SSM), lambda i: (0, i, COL_SGATE_1024)),
            _resident(n_row, SLAB_STATES), _resident(n_row, SLAB_STATES),
            _resident(N_SLAB, SLAB_IN, SLAB_STATES), _resident(N_SLAB, SLAB_IN, SLAB_STATES),
            _resident(N_SLAB, SLAB_STATES, SLAB_IN), _resident(N_SLAB, SLAB_STATES, SLAB_IN),
            _resident(1, D_SSM), _resident(D_SSM, 2 * D_SSM), _resident(1, 2 * D_SSM), _resident(1, D_SSM),
        ],
        out_specs=pl.BlockSpec((b, t_blk, D_SSM), lambda i: (0, i, 0)),
        out_shape=jax.ShapeDtypeStruct((b, l, D_SSM), BF16),
        scratch_shapes=[pltpu.VMEM((N_CHUNK, t_blk * TILE_PITCH, LANES), F32),
                        pltpu.VMEM((N_CHUNK, t_blk * TILE_PITCH, LANES), F32),
                        pltpu.VMEM((n_row, SLAB_STATES), F32),
                        pltpu.VMEM((n_row, SLAB_STATES), F32)],
        compiler_params=pltpu.CompilerParams(
            dimension_semantics=("arbitrary",),
            vmem_limit_bytes=_vmem_limit(est)),
        name="s5_branch",
    )(proj3, proj3, a_re_t, a_im_t, wb_re, wb_im, wc_re, wc_im,
      d_skip.reshape(1, D_SSM), w_glu, b_glu.reshape(1, 2 * D_SSM), g_ssm.reshape(1, D_SSM))


def _mixers(proj, bsz, w_pool, pool_scale, lam_re, lam_im, log_dt,
            b_re, b_im, c_re, c_im, d_skip, w_glu, b_glu, branch_g):
    m = proj.shape[0]
    proj3 = proj.reshape(bsz, m // bsz, D_IN)
    g_pool = branch_g[:D_POOL]
    g_attn = branch_g[D_POOL:D_POOL + D_ATTN]
    g_ssm = branch_g[D_POOL + D_ATTN:]
    y_pool = _pool_branch(proj3, w_pool.astype(BF16), pool_scale, g_pool)
    y_attn = _attn_branch(proj3, g_attn)
    a_re, a_im, wb_re, wb_im, wc_re, wc_im = _discretize(lam_re, lam_im, log_dt, b_re, b_im, c_re, c_im)
    y_ssm = _s5_branch(proj3, a_re, a_im, wb_re, wb_im, wc_re, wc_im, d_skip, w_glu.astype(BF16), b_glu, g_ssm)
    return y_pool.reshape(m, D_POOL), y_attn.reshape(m, D_ATTN), y_ssm.reshape(m, D_SSM)


def kernel(x, ln_g, w_in, w_pool, pool_scale, lam_re, lam_im, log_dt, b_re, b_im, c_re, c_im,
           d_skip, w_glu, b_glu, branch_g, w_out, final_g):
    bsz, l, d = x.shape
    assert ln_g.shape[0] == 2
    mixer_params = (w_pool, pool_scale, lam_re, lam_im, log_dt, b_re, b_im, c_re, c_im,
                    d_skip, w_glu, b_glu, branch_g)
    x0 = x.reshape(bsz * l, d)

    h = _rmsnorm(x0, ln_g[0], BF16)
    proj, w_in1 = _in_proj_f32w(h, w_in, 0)
    ys = _mixers(proj, bsz, *[p[0] for p in mixer_params])
    x1, xg, ssq, w_out1 = _out_proj_f32w(*ys, w_out, 0, x0, ln_g[1])

    proj = _in_proj_scaled(xg, ssq, w_in1)
    ys = _mixers(proj, bsz, *[p[1] for p in mixer_params])
    x2 = _out_proj(*ys, w_out1, x1)
    return _rmsnorm(x2, final_g, x.dtype).reshape(bsz, l, d)
```

```python
import functools
import math

import jax
import jax.numpy as jnp
from jax import lax
from jax.experimental import pallas as pl
from jax.experimental.pallas import tpu as pltpu

D_MODEL = 4096
D_POOL = 1024
D_ATTN = 2048
D_SSM = 1024
POOL_WINDOWS = (2, 4, 8, 16)
POOL_GROUP = 256
POOL_SUB = 256
HEAD_DIM = 128
N_HEADS = 16
SSM_GROUP = 16
SSM_STATE = 64
N_SSM_GROUPS = 64
D_IN = 12288
EPS = 1e-6

COL_PX_1024 = 0
COL_PGATE_1024 = 1
COL_Q_2048 = 1
COL_K_2048 = 2
COL_V_2048 = 3
COL_AGATE_2048 = 4
COL_SU_1024 = 10
COL_SGATE_1024 = 11

V7X_VMEM_BYTES = 64 * 1024 * 1024
VMEM_RESERVE_BYTES = 6 * 1024 * 1024
SUBLANES = 8
LANES = 128

LOG2_E = 1.0 / math.log(2.0)
F32_EXP2_ZERO_ABOVE = 150.0
NO_BLOCK_CARRY = 1e30
MASKED_LOGIT = -1e30

BF16 = jnp.bfloat16
F32 = jnp.float32


def _vmem_limit(nbytes):
    return int(min(nbytes * 3 // 2 + (4 << 20), V7X_VMEM_BYTES - VMEM_RESERVE_BYTES))


def _silu(x):
    return x * jax.nn.sigmoid(x)


def _resident(*shape):
    return pl.BlockSpec(shape, lambda *_: (0,) * len(shape), pipeline_mode=pl.Buffered(1))


def _rmsnorm_kernel(x_ref, g_ref, o_ref):
    x = x_ref[...]
    y = x * lax.rsqrt(jnp.mean(x * x, axis=-1, keepdims=True) + EPS)
    o_ref[...] = (y * g_ref[...]).astype(o_ref.dtype)


def _rmsnorm(x2d, g, out_dtype, tm=512):
    m, d = x2d.shape
    return pl.pallas_call(
        _rmsnorm_kernel,
        grid=(m // tm,),
        in_specs=[pl.BlockSpec((tm, d), lambda i: (i, 0)),
                  pl.BlockSpec((1, d), lambda i: (0, 0))],
        out_specs=pl.BlockSpec((tm, d), lambda i: (i, 0)),
        out_shape=jax.ShapeDtypeStruct((m, d), out_dtype),
        compiler_params=pltpu.CompilerParams(
            dimension_semantics=("parallel",),
            vmem_limit_bytes=_vmem_limit(2 * tm * d * 8)),
        name="rmsnorm",
    )(x2d, g.reshape(1, d))


W_CHUNKS = 4
CAST_ROWS = 256


def _in_proj_f32w_kernel(a_ref, w_ref, wnext_ref, o_ref, wnext_bf_ref, wbf_ref, *, n_col, steps_per_chunk):
    c = pl.program_id(0)
    i = pl.program_id(1)
    chunk_rows = w_ref.shape[0]

    @pl.when(jnp.logical_and(c < n_col, i % steps_per_chunk == 0))
    def _():
        rows = pl.ds(pl.multiple_of((i // steps_per_chunk) * chunk_rows, chunk_rows), chunk_rows)
        wbf_ref[c % 2, rows, :] = w_ref[...].astype(BF16)

    @pl.when(c > 0)
    def _():
        o_ref[...] = jnp.dot(a_ref[...], wbf_ref[(c + 1) % 2], preferred_element_type=F32).astype(o_ref.dtype)
        wnext_bf_ref[...] = wnext_ref[...].astype(BF16)


def _in_proj_f32w(h, w_all, layer, tm=1024, tn=1024):
    m, k = h.shape
    n = w_all.shape[2]
    n_col, n_row = n // tn, m // tm
    steps_per_chunk = n_row // W_CHUNKS
    chunk_rows = k // W_CHUNKS
    side_rows = k // n_row
    col = lambda c: jnp.maximum(c - 1, 0)
    row = lambda c, i: jnp.where(c == 0, 0, i)
    est = (2 * (tm * k * 2 + chunk_rows * tn * 4 + tm * tn * 2 + side_rows * tn * 6) + 2 * k * tn * 2)
    kern = functools.partial(_in_proj_f32w_kernel, n_col=n_col, steps_per_chunk=steps_per_chunk)
    return pl.pallas_call(
        kern,
        grid=(n_col + 1, n_row),
        in_specs=[pl.BlockSpec((tm, k), lambda c, i: (row(c, i), 0)),
                  pl.BlockSpec((None, chunk_rows, tn),
                               lambda c, i: (layer, i // steps_per_chunk, jnp.minimum(c, n_col - 1))),
                  pl.BlockSpec((None, side_rows, tn), lambda c, i: (layer + 1, row(c, i), col(c)))],
        out_specs=(pl.BlockSpec((tm, tn), lambda c, i: (row(c, i), col(c))),
                   pl.BlockSpec((side_rows, tn), lambda c, i: (row(c, i), col(c)))),
        out_shape=(jax.ShapeDtypeStruct((m, n), BF16), jax.ShapeDtypeStruct((k, n), BF16)),
        scratch_shapes=[pltpu.VMEM((2, k, tn), BF16)],
        compiler_params=pltpu.CompilerParams(
            dimension_semantics=("arbitrary", "arbitrary"),
            vmem_limit_bytes=_vmem_limit(est)),
        name="in_proj_f32w",
    )(h, w_all, w_all)


def _in_proj_scaled_kernel(a_ref, ssq_ref, w_ref, o_ref, r_ref):
    @pl.when(pl.program_id(1) == 0)
    def _():
        ssq = jnp.sum(jnp.sum(ssq_ref[...], axis=0), axis=-1, keepdims=True)
        r_ref[...] = lax.rsqrt(ssq / D_MODEL + EPS)

    acc = jnp.dot(a_ref[...], w_ref[...], preferred_element_type=F32)
    o_ref[...] = (acc * r_ref[...]).astype(o_ref.dtype)


def _in_proj_scaled(xg, ssq, w, tm=1024, tn=1024):
    m, k = xg.shape
    n = w.shape[1]
    parts = ssq.shape[0]
    est = 2 * (tm * k * 2 + k * tn * 2 + tm * tn * 2 + parts * tm * LANES * 4) + tm * LANES * 4
    return pl.pallas_call(
        _in_proj_scaled_kernel,
        grid=(m // tm, n // tn),
        in_specs=[pl.BlockSpec((tm, k), lambda i, j: (i, 0)),
                  pl.BlockSpec((parts, tm, LANES), lambda i, j: (0, i, 0)),
                  pl.BlockSpec((k, tn), lambda i, j: (0, j))],
        out_specs=pl.BlockSpec((tm, tn), lambda i, j: (i, j)),
        out_shape=jax.ShapeDtypeStruct((m, n), BF16),
        scratch_shapes=[pltpu.VMEM((tm, 1), F32)],
        compiler_params=pltpu.CompilerParams(
            dimension_semantics=("parallel", "arbitrary"),
            vmem_limit_bytes=_vmem_limit(est)),
        name="in_proj_scaled",
    )(xg, ssq, w)


def _branch_matmul(yp_ref, ya_ref, ys_ref, w_ref):
    acc = jnp.dot(yp_ref[...], w_ref[0:D_POOL, :], preferred_element_type=F32)
    acc += jnp.dot(ya_ref[...], w_ref[D_POOL:D_POOL + D_ATTN, :], preferred_element_type=F32)
    return acc + jnp.dot(ys_ref[...], w_ref[D_POOL + D_ATTN:, :], preferred_element_type=F32)


def _out_proj_f32w_kernel(yp_ref, ya_ref, ys_ref, w_ref, x_ref, g_ref, wnext_ref,
                          xo_ref, xg_ref, ssq_ref, wnext_bf_ref, wbf_ref):
    @pl.when(pl.program_id(1) == 0)
    def _():
        wbf_ref[...] = w_ref[...].astype(BF16)

    x_new = x_ref[...] + _branch_matmul(yp_ref, ya_ref, ys_ref, wbf_ref)
    xo_ref[...] = x_new
    xg_ref[...] = (x_new * g_ref[...]).astype(xg_ref.dtype)
    sq = x_new * x_new
    ssq_ref[...] = functools.reduce(
        lambda a, b: a + b, [sq[:, c * LANES:(c + 1) * LANES] for c in range(sq.shape[1] // LANES)])
    wnext_bf_ref[...] = wnext_ref[...].astype(BF16)


def _out_proj_f32w(y_pool, y_attn, y_ssm, w_all, layer, x2d, next_g, tm=512, tn=1024):
    m = x2d.shape[0]
    k, n = w_all.shape[1:]
    assert (m // tm) * CAST_ROWS == k
    est = (2 * (tm * k * 2 + tm * tn * (4 + 4 + 2) + tm * LANES * 4 + CAST_ROWS * tn * 6) + k * tn * (4 + 2))
    tile = pl.BlockSpec((tm, tn), lambda j, i: (i, j))
    return pl.pallas_call(
        _out_proj_f32w_kernel,
        grid=(n // tn, m // tm),
        in_specs=[pl.BlockSpec((tm, D_POOL), lambda j, i: (i, 0)),
                  pl.BlockSpec((tm, D_ATTN), lambda j, i: (i, 0)),
                  pl.BlockSpec((tm, D_SSM), lambda j, i: (i, 0)),
                  pl.BlockSpec((None, k, tn), lambda j, i: (layer, 0, j), pipeline_mode=pl.Buffered(1)),
                  tile,
                  pl.BlockSpec((1, tn), lambda j, i: (0, j)),
                  pl.BlockSpec((None, CAST_ROWS, tn), lambda j, i: (layer + 1, i, j))],
        out_specs=(tile, tile,
                   pl.BlockSpec((None, tm, LANES), lambda j, i: (j, i, 0)),
                   pl.BlockSpec((CAST_ROWS, tn), lambda j, i: (i, j))),
        out_shape=(jax.ShapeDtypeStruct((m, n), F32), jax.ShapeDtypeStruct((m, n), BF16),
                   jax.ShapeDtypeStruct((n // tn, m, LANES), F32), jax.ShapeDtypeStruct((k, n), BF16)),
        scratch_shapes=[pltpu.VMEM((k, tn), BF16)],
        compiler_params=pltpu.CompilerParams(
            dimension_semantics=("parallel", "arbitrary"),
            vmem_limit_bytes=_vmem_limit(est)),
        name="out_proj_f32w",
    )(y_pool, y_attn, y_ssm, w_all, x2d, next_g.reshape(1, n), w_all)


def _out_proj_kernel(yp_ref, ya_ref, ys_ref, w_ref, x_ref, o_ref):
    o_ref[...] = x_ref[...] + _branch_matmul(yp_ref, ya_ref, ys_ref, w_ref)


def _out_proj(y_pool, y_attn, y_ssm, w, x2d, tm=1024, tn=1024):
    m = x2d.shape[0]
    k, n = w.shape
    est = 2 * (tm * k * 2 + k * tn * 2 + 2 * tm * tn * 4)
    return pl.pallas_call(
        _out_proj_kernel,
        grid=(m // tm, n // tn),
        in_specs=[pl.BlockSpec((tm, D_POOL), lambda i, j: (i, 0)),
                  pl.BlockSpec((tm, D_ATTN), lambda i, j: (i, 0)),
                  pl.BlockSpec((tm, D_SSM), lambda i, j: (i, 0)),
                  pl.BlockSpec((k, tn), lambda i, j: (0, j)),
                  pl.BlockSpec((tm, tn), lambda i, j: (i, j))],
        out_specs=pl.BlockSpec((tm, tn), lambda i, j: (i, j)),
        out_shape=jax.ShapeDtypeStruct((m, n), F32),
        compiler_params=pltpu.CompilerParams(
            dimension_semantics=("parallel", "arbitrary"),
            vmem_limit_bytes=_vmem_limit(est)),
        name="out_proj",
    )(y_pool, y_attn, y_ssm, w, x2d)


def _pool_kernel(px_ref, prev_ref, gate_ref, w_ref, scale_ref, g_ref, o_ref, *, t_blk):
    i = pl.program_id(1)
    x_bf = px_ref[0]
    x = x_bf.astype(F32)
    x_ext = jnp.concatenate([jnp.where(i > 0, prev_ref[0], jnp.zeros_like(prev_ref[0])), x_bf], axis=0)
    pos = (i * t_blk + 1 + lax.broadcasted_iota(jnp.int32, (t_blk, 1), 0)).astype(F32)
    out_row = lax.broadcasted_iota(jnp.int32, (POOL_SUB, 2 * POOL_SUB), 0)
    in_row = lax.broadcasted_iota(jnp.int32, (POOL_SUB, 2 * POOL_SUB), 1)
    lag = out_row + POOL_SUB - in_row
    mixed = []
    for g, w in enumerate(POOL_WINDOWS):
        cols = slice(g * POOL_GROUP, (g + 1) * POOL_GROUP)
        band = jnp.where(lag >= 0, jnp.where(lag < w, 1.0, 0.0), 0.0).astype(BF16)
        s = jnp.concatenate(
            [jnp.dot(band, x_ext[r:r + 2 * POOL_SUB, cols], preferred_element_type=F32)
             for r in range(0, t_blk, POOL_SUB)], axis=0)
        pooled = s / jnp.minimum(pos, float(w)) - x[:, cols]
        mixed.append(jnp.dot(pooled.astype(BF16), w_ref[g], preferred_element_type=F32))
    y = jnp.concatenate(mixed, axis=1) * scale_ref[...]
    y = y * lax.rsqrt(jnp.mean(y * y, axis=-1, keepdims=True) + EPS) * g_ref[...]
    o_ref[0] = (y * _silu(gate_ref[0].astype(F32))).astype(o_ref.dtype)


def _pool_branch(proj3, w_pool, pool_scale, g_pool, t_blk=1024):
    b, l, _ = proj3.shape
    assert max(POOL_WINDOWS) <= POOL_SUB and t_blk % POOL_SUB == 0
    sub_per_blk = t_blk // POOL_SUB
    kern = functools.partial(_pool_kernel, t_blk=t_blk)
    return pl.pallas_call(
        kern,
        grid=(b, l // t_blk),
        in_specs=[
            pl.BlockSpec((1, t_blk, D_POOL), lambda bi, i: (bi, i, COL_PX_1024)),
            pl.BlockSpec((1, POOL_SUB, D_POOL),
                         lambda bi, i: (bi, jnp.maximum(i * sub_per_blk - 1, 0), COL_PX_1024)),
            pl.BlockSpec((1, t_blk, D_POOL), lambda bi, i: (bi, i, COL_PGATE_1024)),
            pl.BlockSpec((len(POOL_WINDOWS), POOL_GROUP, POOL_GROUP), lambda bi, i: (0, 0, 0)),
            pl.BlockSpec((1, D_POOL), lambda bi, i: (0, 0)),
            pl.BlockSpec((1, D_POOL), lambda bi, i: (0, 0)),
        ],
        out_specs=pl.BlockSpec((1, t_blk, D_POOL), lambda bi, i: (bi, i, 0)),
        out_shape=jax.ShapeDtypeStruct((b, l, D_POOL), BF16),
        compiler_params=pltpu.CompilerParams(
            dimension_semantics=("parallel", "parallel"),
            vmem_limit_bytes=_vmem_limit(8 * t_blk * D_POOL * 4)),
        name="pool_branch",
    )(proj3, proj3, proj3, w_pool, pool_scale.reshape(1, D_POOL), g_pool.reshape(1, D_POOL))


def _attn_kernel(q_ref, k_ref, v_ref, gate_ref, g_ref, o_ref, *, t_blk, n_before):
    n_head = N_HEADS
    i = pl.program_id(1)
    row = lax.broadcasted_iota(jnp.int32, (t_blk, t_blk), 0)
    col = lax.broadcasted_iota(jnp.int32, (t_blk, t_blk), 1)
    later = jnp.where(row > col, 1.0, 0.0).astype(BF16)
    causal = col < row
    qs = [(q_ref[0, :, h * HEAD_DIM:(h + 1) * HEAD_DIM].astype(F32) * (HEAD_DIM ** -0.5 * LOG2_E)).astype(BF16)
          for h in range(n_head)]

    def tile(h, j, carry, diagonal):
        lanes = slice(h * HEAD_DIM, (h + 1) * HEAD_DIM)
        start = pl.multiple_of(j * t_blk, t_blk)
        kj = k_ref[0, pl.ds(start, t_blk), lanes]
        vj = v_ref[0, pl.ds(start, t_blk), lanes]
        z = lax.dot_general(qs[h], kj, (((1,), (1,)), ((), ())), preferred_element_type=F32)
        if diagonal:
            z = jnp.where(causal, z, MASKED_LOGIT)
        softplus = jnp.maximum(z, 0.0) + jnp.log(1.0 + jnp.exp2(-jnp.abs(z))) * LOG2_E
        suffix = jnp.dot(softplus.astype(BF16), later, preferred_element_type=F32)
        wts = jnp.exp2(z - softplus - suffix - carry)
        pv = jnp.dot(wts.astype(BF16), vj, preferred_element_type=F32)
        return pv, carry + jnp.sum(softplus, axis=1, keepdims=True)

    accs, carries = [], []
    for h in range(n_head):
        acc, carry = tile(h, i, jnp.zeros((t_blk, 1), F32), True)
        for d in range(1, n_before + 1):
            carry = jnp.where(i >= d, carry, NO_BLOCK_CARRY)
            pv, carry = tile(h, jnp.maximum(i - d, 0), carry, False)
            acc = acc + pv
        accs.append(acc)
        carries.append(carry)

    def alive_of(carries):
        return jnp.min(functools.reduce(jnp.minimum, carries)) < F32_EXP2_ZERO_ABOVE

    def cond(state):
        j, _, _, alive = state
        return jnp.logical_and(j >= 0, alive)

    def body(state):
        j, accs, carries, _ = state
        new_accs, new_carries = [], []
        for h in range(n_head):
            pv, carry = tile(h, j, carries[h], False)
            new_accs.append(accs[h] + pv)
            new_carries.append(carry)
        return j - 1, tuple(new_accs), tuple(new_carries), alive_of(new_carries)

    def finish(accs):
        y = jnp.concatenate(accs, axis=1)
        y = y * lax.rsqrt(jnp.mean(y * y, axis=-1, keepdims=True) + EPS) * g_ref[...]
        o_ref[0] = (y * _silu(gate_ref[0].astype(F32))).astype(o_ref.dtype)

    finish(accs)
    first = i - n_before - 1

    @pl.when(jnp.logical_and(first >= 0, alive_of(carries)))
    def _():
        _, more, _, _ = lax.while_loop(cond, body, (first, tuple(accs), tuple(carries), True))
        finish(more)


def _attn_branch(proj3, g_attn, t_blk=256, n_before=1):
    b, l, _ = proj3.shape
    kern = functools.partial(_attn_kernel, t_blk=t_blk, n_before=n_before)
    est = 2 * l * D_ATTN * 2 + 2 * 3 * t_blk * D_ATTN * 2 + N_HEADS * 12 * t_blk * t_blk * 4
    whole_seq = lambda col: pl.BlockSpec((1, l, D_ATTN), lambda bi, i: (bi, 0, col), pipeline_mode=pl.Buffered(1))
    q_rows = lambda col: pl.BlockSpec((1, t_blk, D_ATTN), lambda bi, i: (bi, i, col))
    return pl.pallas_call(
        kern,
        grid=(b, l // t_blk),
        in_specs=[q_rows(COL_Q_2048), whole_seq(COL_K_2048), whole_seq(COL_V_2048), q_rows(COL_AGATE_2048),
                  pl.BlockSpec((1, D_ATTN), lambda bi, i: (0, 0))],
        out_specs=pl.BlockSpec((1, t_blk, D_ATTN), lambda bi, i: (bi, i, 0)),
        out_shape=jax.ShapeDtypeStruct((b, l, D_ATTN), BF16),
        compiler_params=pltpu.CompilerParams(
            dimension_semantics=("parallel", "arbitrary"),
            vmem_limit_bytes=_vmem_limit(est)),
        name="attn_branch",
    )(proj3, proj3, proj3, proj3, g_attn.reshape(1, D_ATTN))


N_SLAB = 4
SLAB_IN = D_SSM // N_SLAB
SLAB_STATES = 1024
GROUPS_PER_SLAB = N_SSM_GROUPS // N_SLAB
N_CHUNK = SLAB_STATES // LANES
TILE_PITCH = SUBLANES + 1


def _block_diag(ref, rows_per_group, cols_per_group):
    n_rows = ref.shape[1]
    n_cols = GROUPS_PER_SLAB * cols_per_group
    row_group = lax.broadcasted_iota(jnp.int32, (n_rows, n_cols), 0) // rows_per_group
    col_group = lax.broadcasted_iota(jnp.int32, (n_rows, n_cols), 1) // cols_per_group
    return jnp.where((row_group == col_group)[None], jnp.concatenate([ref[...]] * GROUPS_PER_SLAB, axis=-1), 0.0)


def _discretize_kernel(lam_re_ref, lam_im_ref, log_dt_ref, b_re_ref, b_im_ref, c_re_ref, c_im_ref,
                       a_re_ref, a_im_ref, wb_re_ref, wb_im_ref, wc_re_ref, wc_im_ref):
    lam_re = lam_re_ref[...]
    lam_im = lam_im_ref[...]
    dt = jnp.exp(log_dt_ref[...])
    mag = jnp.exp(lam_re * dt)
    a_re = mag * jnp.cos(lam_im * dt)
    a_im = mag * jnp.sin(lam_im * dt)
    a_re_ref[...] = a_re
    a_im_ref[...] = a_im
    num_re = a_re - 1.0
    den = lam_re * lam_re + lam_im * lam_im
    c_re = ((num_re * lam_re + a_im * lam_im) / den)[:, None, :]
    c_im = ((a_im * lam_re - num_re * lam_im) / den)[:, None, :]
    b_re = _block_diag(b_re_ref, SSM_GROUP, SSM_STATE)
    b_im = _block_diag(b_im_ref, SSM_GROUP, SSM_STATE)
    wb_re_ref[...] = (c_re * b_re - c_im * b_im).astype(wb_re_ref.dtype)
    wb_im_ref[...] = (c_re * b_im + c_im * b_re).astype(wb_im_ref.dtype)
    wc_re_ref[...] = _block_diag(c_re_ref, SSM_STATE, SSM_GROUP).astype(wc_re_ref.dtype)
    wc_im_ref[...] = _block_diag(c_im_ref, SSM_STATE, SSM_GROUP).astype(wc_im_ref.dtype)


def _discretize(lam_re, lam_im, log_dt, b_re, b_im, c_re, c_im):
    shape4 = (N_SLAB, SLAB_STATES)
    log_dt_full = jnp.broadcast_to(log_dt[:, None], (N_SSM_GROUPS, SSM_STATE)).reshape(shape4)
    b_rows = lambda b: jnp.swapaxes(b, 1, 2).reshape(N_SLAB, SLAB_IN, SSM_STATE)
    c_rows = lambda c: jnp.swapaxes(c, 1, 2).reshape(N_SLAB, SLAB_STATES, SSM_GROUP)
    out_shape = (jax.ShapeDtypeStruct(shape4, F32), jax.ShapeDtypeStruct(shape4, F32),
                 jax.ShapeDtypeStruct((N_SLAB, SLAB_IN, SLAB_STATES), BF16),
                 jax.ShapeDtypeStruct((N_SLAB, SLAB_IN, SLAB_STATES), BF16),
                 jax.ShapeDtypeStruct((N_SLAB, SLAB_STATES, SLAB_IN), BF16),
                 jax.ShapeDtypeStruct((N_SLAB, SLAB_STATES, SLAB_IN), BF16))
    return pl.pallas_call(
        _discretize_kernel, out_shape=out_shape, name="s5_discretize",
        compiler_params=pltpu.CompilerParams(vmem_limit_bytes=_vmem_limit(32 << 20)),
    )(lam_re.reshape(shape4), lam_im.reshape(shape4), log_dt_full,
      b_rows(b_re), b_rows(b_im), c_rows(c_re), c_rows(c_im))


def _s5_kernel(u_ref, gate_ref, a_re_ref, a_im_ref, wb_re_ref, wb_im_ref, wc_re_ref, wc_im_ref,
               dskip_ref, wglu_ref, bglu_ref, g_ref, o_ref,
               s_re_ref, s_im_ref, x_re_ref, x_im_ref, *, t_blk, n_batch):
    n_row = n_batch * N_SLAB
    assert n_row == SUBLANES

    @pl.when(pl.program_id(0) == 0)
    def _():
        x_re_ref[...] = jnp.zeros_like(x_re_ref)
        x_im_ref[...] = jnp.zeros_like(x_im_ref)

    for j in range(N_SLAB):
        cols = slice(j * SLAB_IN, (j + 1) * SLAB_IN)
        uj = jnp.concatenate([u_ref[b, :, cols] for b in range(n_batch)], axis=0)
        bu_re = jnp.dot(uj, wb_re_ref[j], preferred_element_type=F32)
        bu_im = jnp.dot(uj, wb_im_ref[j], preferred_element_type=F32)
        for b in range(n_batch):
            rows = slice(b * t_blk, (b + 1) * t_blk)
            k = b * N_SLAB + j
            for c in range(N_CHUNK):
                lanes = slice(c * LANES, (c + 1) * LANES)
                s_re_ref[c, pl.ds(k, t_blk, stride=TILE_PITCH), :] = bu_re[rows, lanes]
                s_im_ref[c, pl.ds(k, t_blk, stride=TILE_PITCH), :] = bu_im[rows, lanes]

    a_re = [a_re_ref[:, c * LANES:(c + 1) * LANES] for c in range(N_CHUNK)]
    a_im = [a_im_ref[:, c * LANES:(c + 1) * LANES] for c in range(N_CHUNK)]
    x_re0 = tuple(x_re_ref[:, c * LANES:(c + 1) * LANES] for c in range(N_CHUNK))
    x_im0 = tuple(x_im_ref[:, c * LANES:(c + 1) * LANES] for c in range(N_CHUNK))

    def step(t, state):
        x_re, x_im = state
        rows = pl.ds(t * TILE_PITCH, n_row)
        new_re, new_im = [], []
        for c in range(N_CHUNK):
            nr = a_re[c] * x_re[c] - a_im[c] * x_im[c] + s_re_ref[c, rows, :]
            ni = a_re[c] * x_im[c] + a_im[c] * x_re[c] + s_im_ref[c, rows, :]
            s_re_ref[c, rows, :] = nr
            s_im_ref[c, rows, :] = ni
            new_re.append(nr)
            new_im.append(ni)
        return tuple(new_re), tuple(new_im)

    x_re, x_im = lax.fori_loop(0, t_blk, step, (x_re0, x_im0), unroll=4)
    for c in range(N_CHUNK):
        x_re_ref[:, c * LANES:(c + 1) * LANES] = x_re[c]
        x_im_ref[:, c * LANES:(c + 1) * LANES] = x_im[c]

    def states(s_ref, j):
        return jnp.concatenate(
            [jnp.concatenate([s_ref[c, pl.ds(b * N_SLAB + j, t_blk, stride=TILE_PITCH), :] for c in range(N_CHUNK)],
                             axis=1) for b in range(n_batch)], axis=0).astype(BF16)

    ys = [jnp.dot(states(s_re_ref, j), wc_re_ref[j], preferred_element_type=F32)
          - jnp.dot(states(s_im_ref, j), wc_im_ref[j], preferred_element_type=F32) for j in range(N_SLAB)]
    u_all = jnp.concatenate([u_ref[b] for b in range(n_batch)], axis=0).astype(F32)
    y = jnp.concatenate(ys, axis=1) + dskip_ref[...] * u_all
    h = jax.nn.gelu(y)
    glu = jnp.dot(h.astype(BF16), wglu_ref[...], preferred_element_type=F32) + bglu_ref[...]
    out = glu[:, :D_SSM] * jax.nn.sigmoid(glu[:, D_SSM:])
    out = out * lax.rsqrt(jnp.mean(out * out, axis=-1, keepdims=True) + EPS) * g_ref[...]
    for b in range(n_batch):
        rows = slice(b * t_blk, (b + 1) * t_blk)
        o_ref[b] = (out[rows] * _silu(gate_ref[b].astype(F32))).astype(o_ref.dtype)


def _s5_branch(proj3, a_re, a_im, wb_re, wb_im, wc_re, wc_im, d_skip, w_glu, b_glu, g_ssm, t_blk=256):
    b, l, _ = proj3.shape
    n_row = b * N_SLAB
    a_re_t = jnp.tile(a_re, (b, 1))
    a_im_t = jnp.tile(a_im, (b, 1))
    kern = functools.partial(_s5_kernel, t_blk=t_blk, n_batch=b)
    weights = 4 * N_SLAB * SLAB_IN * SLAB_STATES * 2 + D_SSM * 2 * D_SSM * 2
    est = (weights + 2 * N_CHUNK * t_blk * TILE_PITCH * LANES * 4 + 2 * 3 * b * t_blk * D_SSM * 2
           + 6 * b * t_blk * 2 * D_SSM * 4)
    return pl.pallas_call(
        kern,
        grid=(l // t_blk,),
        in_specs=[
            pl.BlockSpec((b, t_blk, D_SSM), lambda i: (0, i, COL_SU_1024)),
            pl.BlockSpec((b, t_blk, D_SSM), lambda i: (0, i, COL_SGATE_1024)),
            _resident(n_row, SLAB_STATES), _resident(n_row, SLAB_STATES),
            _resident(N_SLAB, SLAB_IN, SLAB_STATES), _resident(N_SLAB, SLAB_IN, SLAB_STATES),
            _resident(N_SLAB, SLAB_STATES, SLAB_IN), _resident(N_SLAB, SLAB_STATES, SLAB_IN),
            _resident(1, D_SSM), _resident(D_SSM, 2 * D_SSM), _resident(1, 2 * D_SSM), _resident(1, D_SSM),
        ],
        out_specs=pl.BlockSpec((b, t_blk, D_SSM), lambda i: (0, i, 0)),
        out_shape=jax.ShapeDtypeStruct((b, l, D_SSM), BF16),
        scratch_shapes=[pltpu.VMEM((N_CHUNK, t_blk * TILE_PITCH, LANES), F32),
                        pltpu.VMEM((N_CHUNK, t_blk * TILE_PITCH, LANES), F32),
                        pltpu.VMEM((n_row, SLAB_STATES), F32),
                        pltpu.VMEM((n_row, SLAB_STATES), F32)],
        compiler_params=pltpu.CompilerParams(
            dimension_semantics=("arbitrary",),
            vmem_limit_bytes=_vmem_limit(est)),
        name="s5_branch",
    )(proj3, proj3, a_re_t, a_im_t, wb_re, wb_im, wc_re, wc_im,
      d_skip.reshape(1, D_SSM), w_glu, b_glu.reshape(1, 2 * D_SSM), g_ssm.reshape(1, D_SSM))


def _mixers(proj, bsz, w_pool, pool_scale, lam_re, lam_im, log_dt,
            b_re, b_im, c_re, c_im, d_skip, w_glu, b_glu, branch_g):
    m = proj.shape[0]
    proj3 = proj.reshape(bsz, m // bsz, D_IN)
    g_pool = branch_g[:D_POOL]
    g_attn = branch_g[D_POOL:D_POOL + D_ATTN]
    g_ssm = branch_g[D_POOL + D_ATTN:]
    y_pool = _pool_branch(proj3, w_pool.astype(BF16), pool_scale, g_pool)
    y_attn = _attn_branch(proj3, g_attn)
    a_re, a_im, wb_re, wb_im, wc_re, wc_im = _discretize(lam_re, lam_im, log_dt, b_re, b_im, c_re, c_im)
    y_ssm = _s5_branch(proj3, a_re, a_im, wb_re, wb_im, wc_re, wc_im, d_skip, w_glu.astype(BF16), b_glu, g_ssm)
    return y_pool.reshape(m, D_POOL), y_attn.reshape(m, D_ATTN), y_ssm.reshape(m, D_SSM)


def kernel(x, ln_g, w_in, w_pool, pool_scale, lam_re, lam_im, log_dt, b_re, b_im, c_re, c_im,
           d_skip, w_glu, b_glu, branch_g, w_out, final_g):
    bsz, l, d = x.shape
    assert ln_g.shape[0] == 2
    mixer_params = (w_pool, pool_scale, lam_re, lam_im, log_dt, b_re, b_im, c_re, c_im,
                    d_skip, w_glu, b_glu, branch_g)
    x0 = x.reshape(bsz * l, d)

    h = _rmsnorm(x0, ln_g[0], BF16)
    proj, w_in1 = _in_proj_f32w(h, w_in, 0)
    ys = _mixers(proj, bsz, *[p[0] for p in mixer_params])
    x1, xg, ssq, w_out1 = _out_proj_f32w(*ys, w_out, 0, x0, ln_g[1])

    proj = _in_proj_scaled(xg, ssq, w_in1)
    ys = _mixers(proj, bsz, *[p[1] for p in mixer_params])
    x2 = _out_proj(*ys, w_out1, x1)
    return _rmsnorm(x2, final_g, x.dtype).reshape(bsz, l, d)
```

```python
import functools
import math

import jax
import jax.numpy as jnp
from jax import lax
from jax.experimental import pallas as pl
from jax.experimental.pallas import tpu as pltpu

D_MODEL = 4096
D_POOL = 1024
D_ATTN = 2048
D_SSM = 1024
POOL_WINDOWS = (2, 4, 8, 16)
POOL_GROUP = 256
POOL_SUB = 256
HEAD_DIM = 128
N_HEADS = 16
SSM_GROUP = 16
SSM_STATE = 64
N_SSM_GROUPS = 64
D_IN = 12288
EPS = 1e-6

COL_PX_1024 = 0
COL_PGATE_1024 = 1
COL_Q_2048 = 1
COL_K_2048 = 2
COL_V_2048 = 3
COL_AGATE_2048 = 4
COL_SU_1024 = 10
COL_SGATE_1024 = 11

V7X_VMEM_BYTES = 64 * 1024 * 1024
VMEM_RESERVE_BYTES = 6 * 1024 * 1024
SUBLANES = 8
LANES = 128

LOG2_E = 1.0 / math.log(2.0)
F32_EXP2_ZERO_ABOVE = 150.0
NO_BLOCK_CARRY = 1e30
MASKED_LOGIT = -1e30

BF16 = jnp.bfloat16
F32 = jnp.float32


def _vmem_limit(nbytes):
    return int(min(nbytes * 3 // 2 + (4 << 20), V7X_VMEM_BYTES - VMEM_RESERVE_BYTES))


def _silu(x):
    return x * jax.nn.sigmoid(x)


def _resident(*shape):
    return pl.BlockSpec(shape, lambda *_: (0,) * len(shape), pipeline_mode=pl.Buffered(1))


def _rmsnorm_kernel(x_ref, g_ref, o_ref):
    x = x_ref[...]
    y = x * lax.rsqrt(jnp.mean(x * x, axis=-1, keepdims=True) + EPS)
    o_ref[...] = (y * g_ref[...]).astype(o_ref.dtype)


def _rmsnorm(x2d, g, out_dtype, tm=512):
    m, d = x2d.shape
    return pl.pallas_call(
        _rmsnorm_kernel,
        grid=(m // tm,),
        in_specs=[pl.BlockSpec((tm, d), lambda i: (i, 0)),
                  pl.BlockSpec((1, d), lambda i: (0, 0))],
        out_specs=pl.BlockSpec((tm, d), lambda i: (i, 0)),
        out_shape=jax.ShapeDtypeStruct((m, d), out_dtype),
        compiler_params=pltpu.CompilerParams(
            dimension_semantics=("parallel",),
            vmem_limit_bytes=_vmem_limit(2 * tm * d * 8)),
        name="rmsnorm",
    )(x2d, g.reshape(1, d))


W_CHUNKS = 4


def _in_proj_f32w_kernel(a_ref, w_ref, wnext_ref, o_ref, wnext_bf_ref, wbf_ref, *, n_col, steps_per_chunk):
    c = pl.program_id(0)
    i = pl.program_id(1)
    chunk_rows = w_ref.shape[0]

    @pl.when(jnp.logical_and(c < n_col, i % steps_per_chunk == 0))
    def _():
        rows = pl.ds(pl.multiple_of((i // steps_per_chunk) * chunk_rows, chunk_rows), chunk_rows)
        wbf_ref[c % 2, rows, :] = w_ref[...].astype(BF16)

    @pl.when(c > 0)
    def _():
        o_ref[...] = jnp.dot(a_ref[...], wbf_ref[(c + 1) % 2], preferred_element_type=F32).astype(o_ref.dtype)
        wnext_bf_ref[...] = wnext_ref[...].astype(BF16)


def _in_proj_f32w(h, w_all, layer, tm=1024, tn=1024):
    m, k = h.shape
    n = w_all.shape[2]
    n_col, n_row = n // tn, m // tm
    steps_per_chunk = n_row // W_CHUNKS
    chunk_rows = k // W_CHUNKS
    side_rows = k // n_row
    col = lambda c: jnp.maximum(c - 1, 0)
    row = lambda c, i: jnp.where(c == 0, 0, i)
    est = (2 * (tm * k * 2 + chunk_rows * tn * 4 + tm * tn * 2 + side_rows * tn * 6) + 2 * k * tn * 2)
    kern = functools.partial(_in_proj_f32w_kernel, n_col=n_col, steps_per_chunk=steps_per_chunk)
    return pl.pallas_call(
        kern,
        grid=(n_col + 1, n_row),
        in_specs=[pl.BlockSpec((tm, k), lambda c, i: (row(c, i), 0)),
                  pl.BlockSpec((None, chunk_rows, tn),
                               lambda c, i: (layer, i // steps_per_chunk, jnp.minimum(c, n_col - 1))),
                  pl.BlockSpec((None, side_rows, tn), lambda c, i: (layer + 1, row(c, i), col(c)))],
        out_specs=(pl.BlockSpec((tm, tn), lambda c, i: (row(c, i), col(c))),
                   pl.BlockSpec((side_rows, tn), lambda c, i: (row(c, i), col(c)))),
        out_shape=(jax.ShapeDtypeStruct((m, n), BF16), jax.ShapeDtypeStruct((k, n), BF16)),
        scratch_shapes=[pltpu.VMEM((2, k, tn), BF16)],
        compiler_params=pltpu.CompilerParams(
            dimension_semantics=("arbitrary", "arbitrary"),
            vmem_limit_bytes=_vmem_limit(est)),
        name="in_proj_f32w",
    )(h, w_all, w_all)


def _in_proj_scaled_kernel(a_ref, ssq_ref, w_ref, wout_ref, o_ref, wout_bf_ref, r_ref, *, n_cast_cols):
    j = pl.program_id(1)

    @pl.when(j == 0)
    def _():
        ssq = jnp.sum(jnp.sum(ssq_ref[...], axis=0), axis=-1, keepdims=True)
        r_ref[...] = lax.rsqrt(ssq / D_MODEL + EPS)

    acc = jnp.dot(a_ref[...], w_ref[...], preferred_element_type=F32)
    o_ref[...] = (acc * r_ref[...]).astype(o_ref.dtype)

    @pl.when(j < n_cast_cols)
    def _():
        wout_bf_ref[...] = wout_ref[...].astype(BF16)


def _in_proj_scaled(xg, ssq, w, w_out_all, layer, tm=1024, tn=1024, cast_cols=512):
    m, k = xg.shape
    n = w.shape[1]
    parts = ssq.shape[0]
    n_row, n_col = m // tm, n // tn
    ko, no = w_out_all.shape[1:]
    cast_rows = ko // n_row
    n_cast_cols = no // cast_cols
    assert n_cast_cols <= n_col
    cast_col = lambda j: jnp.minimum(j, n_cast_cols - 1)
    est = (2 * (tm * k * 2 + k * tn * 2 + tm * tn * 2 + parts * tm * LANES * 4 + cast_rows * cast_cols * 6)
           + tm * LANES * 4)
    return pl.pallas_call(
        functools.partial(_in_proj_scaled_kernel, n_cast_cols=n_cast_cols),
        grid=(n_row, n_col),
        in_specs=[pl.BlockSpec((tm, k), lambda i, j: (i, 0)),
                  pl.BlockSpec((parts, tm, LANES), lambda i, j: (0, i, 0)),
                  pl.BlockSpec((k, tn), lambda i, j: (0, j)),
                  pl.BlockSpec((None, cast_rows, cast_cols), lambda i, j: (layer, i, cast_col(j)))],
        out_specs=(pl.BlockSpec((tm, tn), lambda i, j: (i, j)),
                   pl.BlockSpec((cast_rows, cast_cols), lambda i, j: (i, cast_col(j)))),
        out_shape=(jax.ShapeDtypeStruct((m, n), BF16), jax.ShapeDtypeStruct((ko, no), BF16)),
        scratch_shapes=[pltpu.VMEM((tm, 1), F32)],
        compiler_params=pltpu.CompilerParams(
            dimension_semantics=("parallel", "arbitrary"),
            vmem_limit_bytes=_vmem_limit(est)),
        name="in_proj_scaled",
    )(xg, ssq, w, w_out_all)


def _branch_matmul(yp_ref, ya_ref, ys_ref, w_ref):
    acc = jnp.dot(yp_ref[...], w_ref[0:D_POOL, :], preferred_element_type=F32)
    acc += jnp.dot(ya_ref[...], w_ref[D_POOL:D_POOL + D_ATTN, :], preferred_element_type=F32)
    return acc + jnp.dot(ys_ref[...], w_ref[D_POOL + D_ATTN:, :], preferred_element_type=F32)


def _out_proj_f32w_kernel(yp_ref, ya_ref, ys_ref, w_ref, x_ref, g_ref, xo_ref, xg_ref, ssq_ref, wbf_ref):
    @pl.when(pl.program_id(1) == 0)
    def _():
        wbf_ref[...] = w_ref[...].astype(BF16)

    x_new = x_ref[...] + _branch_matmul(yp_ref, ya_ref, ys_ref, wbf_ref)
    xo_ref[...] = x_new
    xg_ref[...] = (x_new * g_ref[...]).astype(xg_ref.dtype)
    sq = x_new * x_new
    ssq_ref[...] = functools.reduce(
        lambda a, b: a + b, [sq[:, c * LANES:(c + 1) * LANES] for c in range(sq.shape[1] // LANES)])


def _out_proj_f32w(y_pool, y_attn, y_ssm, w_all, layer, x2d, next_g, tm=512, tn=1024):
    m = x2d.shape[0]
    k, n = w_all.shape[1:]
    est = 2 * (tm * k * 2 + tm * tn * (4 + 4 + 2) + tm * LANES * 4) + k * tn * (4 + 2)
    tile = pl.BlockSpec((tm, tn), lambda j, i: (i, j))
    return pl.pallas_call(
        _out_proj_f32w_kernel,
        grid=(n // tn, m // tm),
        in_specs=[pl.BlockSpec((tm, D_POOL), lambda j, i: (i, 0)),
                  pl.BlockSpec((tm, D_ATTN), lambda j, i: (i, 0)),
                  pl.BlockSpec((tm, D_SSM), lambda j, i: (i, 0)),
                  pl.BlockSpec((None, k, tn), lambda j, i: (layer, 0, j), pipeline_mode=pl.Buffered(1)),
                  tile,
                  pl.BlockSpec((1, tn), lambda j, i: (0, j))],
        out_specs=(tile, tile, pl.BlockSpec((None, tm, LANES), lambda j, i: (j, i, 0))),
        out_shape=(jax.ShapeDtypeStruct((m, n), F32), jax.ShapeDtypeStruct((m, n), BF16),
                   jax.ShapeDtypeStruct((n // tn, m, LANES), F32)),
        scratch_shapes=[pltpu.VMEM((k, tn), BF16)],
        compiler_params=pltpu.CompilerParams(
            dimension_semantics=("parallel", "arbitrary"),
            vmem_limit_bytes=_vmem_limit(est)),
        name="out_proj_f32w",
    )(y_pool, y_attn, y_ssm, w_all, x2d, next_g.reshape(1, n))


def _out_proj_kernel(yp_ref, ya_ref, ys_ref, w_ref, x_ref, o_ref):
    o_ref[...] = x_ref[...] + _branch_matmul(yp_ref, ya_ref, ys_ref, w_ref)


def _out_proj(y_pool, y_attn, y_ssm, w, x2d, tm=1024, tn=1024):
    m = x2d.shape[0]
    k, n = w.shape
    est = 2 * (tm * k * 2 + k * tn * 2 + 2 * tm * tn * 4)
    return pl.pallas_call(
        _out_proj_kernel,
        grid=(m // tm, n // tn),
        in_specs=[pl.BlockSpec((tm, D_POOL), lambda i, j: (i, 0)),
                  pl.BlockSpec((tm, D_ATTN), lambda i, j: (i, 0)),
                  pl.BlockSpec((tm, D_SSM), lambda i, j: (i, 0)),
                  pl.BlockSpec((k, tn), lambda i, j: (0, j)),
                  pl.BlockSpec((tm, tn), lambda i, j: (i, j))],
        out_specs=pl.BlockSpec((tm, tn), lambda i, j: (i, j)),
        out_shape=jax.ShapeDtypeStruct((m, n), F32),
        compiler_params=pltpu.CompilerParams(
            dimension_semantics=("parallel", "arbitrary"),
            vmem_limit_bytes=_vmem_limit(est)),
        name="out_proj",
    )(y_pool, y_attn, y_ssm, w, x2d)


def _pool_kernel(px_ref, prev_ref, gate_ref, w_ref, scale_ref, g_ref, o_ref, *, t_blk):
    i = pl.program_id(1)
    x_bf = px_ref[0]
    x = x_bf.astype(F32)
    x_ext = jnp.concatenate([jnp.where(i > 0, prev_ref[0], jnp.zeros_like(prev_ref[0])), x_bf], axis=0)
    pos = (i * t_blk + 1 + lax.broadcasted_iota(jnp.int32, (t_blk, 1), 0)).astype(F32)
    out_row = lax.broadcasted_iota(jnp.int32, (POOL_SUB, 2 * POOL_SUB), 0)
    in_row = lax.broadcasted_iota(jnp.int32, (POOL_SUB, 2 * POOL_SUB), 1)
    lag = out_row + POOL_SUB - in_row
    mixed = []
    for g, w in enumerate(POOL_WINDOWS):
        cols = slice(g * POOL_GROUP, (g + 1) * POOL_GROUP)
        band = jnp.where(lag >= 0, jnp.where(lag < w, 1.0, 0.0), 0.0).astype(BF16)
        s = jnp.concatenate(
            [jnp.dot(band, x_ext[r:r + 2 * POOL_SUB, cols], preferred_element_type=F32)
             for r in range(0, t_blk, POOL_SUB)], axis=0)
        pooled = s / jnp.minimum(pos, float(w)) - x[:, cols]
        mixed.append(jnp.dot(pooled.astype(BF16), w_ref[g], preferred_element_type=F32))
    y = jnp.concatenate(mixed, axis=1) * scale_ref[...]
    y = y * lax.rsqrt(jnp.mean(y * y, axis=-1, keepdims=True) + EPS) * g_ref[...]
    o_ref[0] = (y * _silu(gate_ref[0].astype(F32))).astype(o_ref.dtype)


def _pool_branch(proj3, w_pool, pool_scale, g_pool, t_blk=1024):
    b, l, _ = proj3.shape
    assert max(POOL_WINDOWS) <= POOL_SUB and t_blk % POOL_SUB == 0
    sub_per_blk = t_blk // POOL_SUB
    kern = functools.partial(_pool_kernel, t_blk=t_blk)
    return pl.pallas_call(
        kern,
        grid=(b, l // t_blk),
        in_specs=[
            pl.BlockSpec((1, t_blk, D_POOL), lambda bi, i: (bi, i, COL_PX_1024)),
            pl.BlockSpec((1, POOL_SUB, D_POOL),
                         lambda bi, i: (bi, jnp.maximum(i * sub_per_blk - 1, 0), COL_PX_1024)),
            pl.BlockSpec((1, t_blk, D_POOL), lambda bi, i: (bi, i, COL_PGATE_1024)),
            pl.BlockSpec((len(POOL_WINDOWS), POOL_GROUP, POOL_GROUP), lambda bi, i: (0, 0, 0)),
            pl.BlockSpec((1, D_POOL), lambda bi, i: (0, 0)),
            pl.BlockSpec((1, D_POOL), lambda bi, i: (0, 0)),
        ],
        out_specs=pl.BlockSpec((1, t_blk, D_POOL), lambda bi, i: (bi, i, 0)),
        out_shape=jax.ShapeDtypeStruct((b, l, D_POOL), BF16),
        compiler_params=pltpu.CompilerParams(
            dimension_semantics=("parallel", "parallel"),
            vmem_limit_bytes=_vmem_limit(8 * t_blk * D_POOL * 4)),
        name="pool_branch",
    )(proj3, proj3, proj3, w_pool, pool_scale.reshape(1, D_POOL), g_pool.reshape(1, D_POOL))


def _attn_kernel(q_ref, k_ref, v_ref, gate_ref, g_ref, o_ref, *, t_blk, n_before):
    n_head = N_HEADS
    i = pl.program_id(1)
    row = lax.broadcasted_iota(jnp.int32, (t_blk, t_blk), 0)
    col = lax.broadcasted_iota(jnp.int32, (t_blk, t_blk), 1)
    later = jnp.where(row > col, 1.0, 0.0).astype(BF16)
    causal = col < row
    qs = [(q_ref[0, :, h * HEAD_DIM:(h + 1) * HEAD_DIM].astype(F32) * (HEAD_DIM ** -0.5 * LOG2_E)).astype(BF16)
          for h in range(n_head)]

    def tile(h, j, carry, diagonal):
        lanes = slice(h * HEAD_DIM, (h + 1) * HEAD_DIM)
        start = pl.multiple_of(j * t_blk, t_blk)
        kj = k_ref[0, pl.ds(start, t_blk), lanes]
        vj = v_ref[0, pl.ds(start, t_blk), lanes]
        z = lax.dot_general(qs[h], kj, (((1,), (1,)), ((), ())), preferred_element_type=F32)
        if diagonal:
            z = jnp.where(causal, z, MASKED_LOGIT)
        softplus = jnp.maximum(z, 0.0) + jnp.log(1.0 + jnp.exp2(-jnp.abs(z))) * LOG2_E
        suffix = jnp.dot(softplus.astype(BF16), later, preferred_element_type=F32)
        wts = jnp.exp2(z - softplus - suffix - carry)
        pv = jnp.dot(wts.astype(BF16), vj, preferred_element_type=F32)
        return pv, carry + jnp.sum(softplus, axis=1, keepdims=True)

    accs, carries = [], []
    for h in range(n_head):
        acc, carry = tile(h, i, jnp.zeros((t_blk, 1), F32), True)
        for d in range(1, n_before + 1):
            carry = jnp.where(i >= d, carry, NO_BLOCK_CARRY)
            pv, carry = tile(h, jnp.maximum(i - d, 0), carry, False)
            acc = acc + pv
        accs.append(acc)
        carries.append(carry)

    def alive_of(carries):
        return jnp.min(functools.reduce(jnp.minimum, carries)) < F32_EXP2_ZERO_ABOVE

    def cond(state):
        j, _, _, alive = state
        return jnp.logical_and(j >= 0, alive)

    def body(state):
        j, accs, carries, _ = state
        new_accs, new_carries = [], []
        for h in range(n_head):
            pv, carry = tile(h, j, carries[h], False)
            new_accs.append(accs[h] + pv)
            new_carries.append(carry)
        return j - 1, tuple(new_accs), tuple(new_carries), alive_of(new_carries)

    def finish(accs):
        y = jnp.concatenate(accs, axis=1)
        y = y * lax.rsqrt(jnp.mean(y * y, axis=-1, keepdims=True) + EPS) * g_ref[...]
        o_ref[0] = (y * _silu(gate_ref[0].astype(F32))).astype(o_ref.dtype)

    finish(accs)
    first = i - n_before - 1

    @pl.when(jnp.logical_and(first >= 0, alive_of(carries)))
    def _():
        _, more, _, _ = lax.while_loop(cond, body, (first, tuple(accs), tuple(carries), True))
        finish(more)


def _attn_branch(proj3, g_attn, t_blk=256, n_before=1):
    b, l, _ = proj3.shape
    kern = functools.partial(_attn_kernel, t_blk=t_blk, n_before=n_before)
    est = 2 * l * D_ATTN * 2 + 2 * 3 * t_blk * D_ATTN * 2 + N_HEADS * 12 * t_blk * t_blk * 4
    whole_seq = lambda col: pl.BlockSpec((1, l, D_ATTN), lambda bi, i: (bi, 0, col), pipeline_mode=pl.Buffered(1))
    q_rows = lambda col: pl.BlockSpec((1, t_blk, D_ATTN), lambda bi, i: (bi, i, col))
    return pl.pallas_call(
        kern,
        grid=(b, l // t_blk),
        in_specs=[q_rows(COL_Q_2048), whole_seq(COL_K_2048), whole_seq(COL_V_2048), q_rows(COL_AGATE_2048),
                  pl.BlockSpec((1, D_ATTN), lambda bi, i: (0, 0))],
        out_specs=pl.BlockSpec((1, t_blk, D_ATTN), lambda bi, i: (bi, i, 0)),
        out_shape=jax.ShapeDtypeStruct((b, l, D_ATTN), BF16),
        compiler_params=pltpu.CompilerParams(
            dimension_semantics=("parallel", "arbitrary"),
            vmem_limit_bytes=_vmem_limit(est)),
        name="attn_branch",
    )(proj3, proj3, proj3, proj3, g_attn.reshape(1, D_ATTN))


N_SLAB = 4
SLAB_IN = D_SSM // N_SLAB
SLAB_STATES = 1024
GROUPS_PER_SLAB = N_SSM_GROUPS // N_SLAB
N_CHUNK = SLAB_STATES // LANES
TILE_PITCH = SUBLANES + 1


def _block_diag(ref, rows_per_group, cols_per_group):
    n_rows = ref.shape[1]
    n_cols = GROUPS_PER_SLAB * cols_per_group
    row_group = lax.broadcasted_iota(jnp.int32, (n_rows, n_cols), 0) // rows_per_group
    col_group = lax.broadcasted_iota(jnp.int32, (n_rows, n_cols), 1) // cols_per_group
    return jnp.where((row_group == col_group)[None], jnp.concatenate([ref[...]] * GROUPS_PER_SLAB, axis=-1), 0.0)


def _discretize_kernel(lam_re_ref, lam_im_ref, log_dt_ref, b_re_ref, b_im_ref, c_re_ref, c_im_ref,
                       a_re_ref, a_im_ref, wb_re_ref, wb_im_ref, wc_re_ref, wc_im_ref):
    lam_re = lam_re_ref[...]
    lam_im = lam_im_ref[...]
    dt = jnp.exp(log_dt_ref[...])
    mag = jnp.exp(lam_re * dt)
    a_re = mag * jnp.cos(lam_im * dt)
    a_im = mag * jnp.sin(lam_im * dt)
    a_re_ref[...] = a_re
    a_im_ref[...] = a_im
    num_re = a_re - 1.0
    den = lam_re * lam_re + lam_im * lam_im
    c_re = ((num_re * lam_re + a_im * lam_im) / den)[:, None, :]
    c_im = ((a_im * lam_re - num_re * lam_im) / den)[:, None, :]
    b_re = _block_diag(b_re_ref, SSM_GROUP, SSM_STATE)
    b_im = _block_diag(b_im_ref, SSM_GROUP, SSM_STATE)
    wb_re_ref[...] = (c_re * b_re - c_im * b_im).astype(wb_re_ref.dtype)
    wb_im_ref[...] = (c_re * b_im + c_im * b_re).astype(wb_im_ref.dtype)
    wc_re_ref[...] = _block_diag(c_re_ref, SSM_STATE, SSM_GROUP).astype(wc_re_ref.dtype)
    wc_im_ref[...] = _block_diag(c_im_ref, SSM_STATE, SSM_GROUP).astype(wc_im_ref.dtype)


def _discretize(lam_re, lam_im, log_dt, b_re, b_im, c_re, c_im):
    shape4 = (N_SLAB, SLAB_STATES)
    log_dt_full = jnp.broadcast_to(log_dt[:, None], (N_SSM_GROUPS, SSM_STATE)).reshape(shape4)
    b_rows = lambda b: jnp.swapaxes(b, 1, 2).reshape(N_SLAB, SLAB_IN, SSM_STATE)
    c_rows = lambda c: jnp.swapaxes(c, 1, 2).reshape(N_SLAB, SLAB_STATES, SSM_GROUP)
    out_shape = (jax.ShapeDtypeStruct(shape4, F32), jax.ShapeDtypeStruct(shape4, F32),
                 jax.ShapeDtypeStruct((N_SLAB, SLAB_IN, SLAB_STATES), BF16),
                 jax.ShapeDtypeStruct((N_SLAB, SLAB_IN, SLAB_STATES), BF16),
                 jax.ShapeDtypeStruct((N_SLAB, SLAB_STATES, SLAB_IN), BF16),
                 jax.ShapeDtypeStruct((N_SLAB, SLAB_STATES, SLAB_IN), BF16))
    return pl.pallas_call(
        _discretize_kernel, out_shape=out_shape, name="s5_discretize",
        compiler_params=pltpu.CompilerParams(vmem_limit_bytes=_vmem_limit(32 << 20)),
    )(lam_re.reshape(shape4), lam_im.reshape(shape4), log_dt_full,
      b_rows(b_re), b_rows(b_im), c_rows(c_re), c_rows(c_im))


def _s5_kernel(u_ref, gate_ref, a_re_ref, a_im_ref, wb_re_ref, wb_im_ref, wc_re_ref, wc_im_ref,
               dskip_ref, wglu_ref, bglu_ref, g_ref, o_ref,
               s_re_ref, s_im_ref, x_re_ref, x_im_ref, *, t_blk, n_batch):
    n_row = n_batch * N_SLAB
    assert n_row == SUBLANES

    @pl.when(pl.program_id(0) == 0)
    def _():
        x_re_ref[...] = jnp.zeros_like(x_re_ref)
        x_im_ref[...] = jnp.zeros_like(x_im_ref)

    for j in range(N_SLAB):
        cols = slice(j * SLAB_IN, (j + 1) * SLAB_IN)
        uj = jnp.concatenate([u_ref[b, :, cols] for b in range(n_batch)], axis=0)
        bu_re = jnp.dot(uj, wb_re_ref[j], preferred_element_type=F32)
        bu_im = jnp.dot(uj, wb_im_ref[j], preferred_element_type=F32)
        for b in range(n_batch):
            rows = slice(b * t_blk, (b + 1) * t_blk)
            k = b * N_SLAB + j
            for c in range(N_CHUNK):
                lanes = slice(c * LANES, (c + 1) * LANES)
                s_re_ref[c, pl.ds(k, t_blk, stride=TILE_PITCH), :] = bu_re[rows, lanes]
                s_im_ref[c, pl.ds(k, t_blk, stride=TILE_PITCH), :] = bu_im[rows, lanes]

    a_re = [a_re_ref[:, c * LANES:(c + 1) * LANES] for c in range(N_CHUNK)]
    a_im = [a_im_ref[:, c * LANES:(c + 1) * LANES] for c in range(N_CHUNK)]
    x_re0 = tuple(x_re_ref[:, c * LANES:(c + 1) * LANES] for c in range(N_CHUNK))
    x_im0 = tuple(x_im_ref[:, c * LANES:(c + 1) * LANES] for c in range(N_CHUNK))

    def step(t, state):
        x_re, x_im = state
        rows = pl.ds(t * TILE_PITCH, n_row)
        new_re, new_im = [], []
        for c in range(N_CHUNK):
            nr = a_re[c] * x_re[c] - a_im[c] * x_im[c] + s_re_ref[c, rows, :]
            ni = a_re[c] * x_im[c] + a_im[c] * x_re[c] + s_im_ref[c, rows, :]
            s_re_ref[c, rows, :] = nr
            s_im_ref[c, rows, :] = ni
            new_re.append(nr)
            new_im.append(ni)
        return tuple(new_re), tuple(new_im)

    x_re, x_im = lax.fori_loop(0, t_blk, step, (x_re0, x_im0), unroll=4)
    for c in range(N_CHUNK):
        x_re_ref[:, c * LANES:(c + 1) * LANES] = x_re[c]
        x_im_ref[:, c * LANES:(c + 1) * LANES] = x_im[c]

    def states(s_ref, j):
        return jnp.concatenate(
            [jnp.concatenate([s_ref[c, pl.ds(b * N_SLAB + j, t_blk, stride=TILE_PITCH), :] for c in range(N_CHUNK)],
                             axis=1) for b in range(n_batch)], axis=0).astype(BF16)

    ys = [jnp.dot(states(s_re_ref, j), wc_re_ref[j], preferred_element_type=F32)
          - jnp.dot(states(s_im_ref, j), wc_im_ref[j], preferred_element_type=F32) for j in range(N_SLAB)]
    u_all = jnp.concatenate([u_ref[b] for b in range(n_batch)], axis=0).astype(F32)
    y = jnp.concatenate(ys, axis=1) + dskip_ref[...] * u_all
    h = jax.nn.gelu(y)
    glu = jnp.dot(h.astype(BF16), wglu_ref[...], preferred_element_type=F32) + bglu_ref[...]
    out = glu[:, :D_SSM] * jax.nn.sigmoid(glu[:, D_SSM:])
    out = out * lax.rsqrt(jnp.mean(out * out, axis=-1, keepdims=True) + EPS) * g_ref[...]
    for b in range(n_batch):
        rows = slice(b * t_blk, (b + 1) * t_blk)
        o_ref[b] = (out[rows] * _silu(gate_ref[b].astype(F32))).astype(o_ref.dtype)


def _s5_branch(proj3, a_re, a_im, wb_re, wb_im, wc_re, wc_im, d_skip, w_glu, b_glu, g_ssm, t_blk=256):
    b, l, _ = proj3.shape
    n_row = b * N_SLAB
    a_re_t = jnp.tile(a_re, (b, 1))
    a_im_t = jnp.tile(a_im, (b, 1))
    kern = functools.partial(_s5_kernel, t_blk=t_blk, n_batch=b)
    weights = 4 * N_SLAB * SLAB_IN * SLAB_STATES * 2 + D_SSM * 2 * D_SSM * 2
    est = (weights + 2 * N_CHUNK * t_blk * TILE_PITCH * LANES * 4 + 2 * 3 * b * t_blk * D_SSM * 2
           + 6 * b * t_blk * 2 * D_SSM * 4)
    return pl.pallas_call(
        kern,
        grid=(l // t_blk,),
        in_specs=[
            pl.BlockSpec((b, t_blk, D_SSM), lambda i: (0, i, COL_SU_1024)),
            pl.BlockSpec((b, t_blk, D_SSM), lambda i: (0, i, COL_SGATE_1024)),
            _resident(n_row, SLAB_STATES), _resident(n_row, SLAB_STATES),
            _resident(N_SLAB, SLAB_IN, SLAB_STATES), _resident(N_SLAB, SLAB_IN, SLAB_STATES),
            _resident(N_SLAB, SLAB_STATES, SLAB_IN), _resident(N_SLAB, SLAB_STATES, SLAB_IN),
            _resident(1, D_SSM), _resident(D_SSM, 2 * D_SSM), _resident(1, 2 * D_SSM), _resident(1, D_SSM),
        ],
        out_specs=pl.BlockSpec((b, t_blk, D_SSM), lambda i: (0, i, 0)),
        out_shape=jax.ShapeDtypeStruct((b, l, D_SSM), BF16),
        scratch_shapes=[pltpu.VMEM((N_CHUNK, t_blk * TILE_PITCH, LANES), F32),
                        pltpu.VMEM((N_CHUNK, t_blk * TILE_PITCH, LANES), F32),
                        pltpu.VMEM((n_row, SLAB_STATES), F32),
                        pltpu.VMEM((n_row, SLAB_STATES), F32)],
        compiler_params=pltpu.CompilerParams(
            dimension_semantics=("arbitrary",),
            vmem_limit_bytes=_vmem_limit(est)),
        name="s5_branch",
    )(proj3, proj3, a_re_t, a_im_t, wb_re, wb_im, wc_re, wc_im,
      d_skip.reshape(1, D_SSM), w_glu, b_glu.reshape(1, 2 * D_SSM), g_ssm.reshape(1, D_SSM))


def _mixers(proj, bsz, w_pool, pool_scale, lam_re, lam_im, log_dt,
            b_re, b_im, c_re, c_im, d_skip, w_glu, b_glu, branch_g):
    m = proj.shape[0]
    proj3 = proj.reshape(bsz, m // bsz, D_IN)
    g_pool = branch_g[:D_POOL]
    g_attn = branch_g[D_POOL:D_POOL + D_ATTN]
    g_ssm = branch_g[D_POOL + D_ATTN:]
    y_pool = _pool_branch(proj3, w_pool.astype(BF16), pool_scale, g_pool)
    y_attn = _attn_branch(proj3, g_attn)
    a_re, a_im, wb_re, wb_im, wc_re, wc_im = _discretize(lam_re, lam_im, log_dt, b_re, b_im, c_re, c_im)
    y_ssm = _s5_branch(proj3, a_re, a_im, wb_re, wb_im, wc_re, wc_im, d_skip, w_glu.astype(BF16), b_glu, g_ssm)
    return y_pool.reshape(m, D_POOL), y_attn.reshape(m, D_ATTN), y_ssm.reshape(m, D_SSM)


def kernel(x, ln_g, w_in, w_pool, pool_scale, lam_re, lam_im, log_dt, b_re, b_im, c_re, c_im,
           d_skip, w_glu, b_glu, branch_g, w_out, final_g):
    bsz, l, d = x.shape
    assert ln_g.shape[0] == 2
    mixer_params = (w_pool, pool_scale, lam_re, lam_im, log_dt, b_re, b_im, c_re, c_im,
                    d_skip, w_glu, b_glu, branch_g)
    x0 = x.reshape(bsz * l, d)

    h = _rmsnorm(x0, ln_g[0], BF16)
    proj, w_in1 = _in_proj_f32w(h, w_in, 0)
    ys = _mixers(proj, bsz, *[p[0] for p in mixer_params])
    x1, xg, ssq = _out_proj_f32w(*ys, w_out, 0, x0, ln_g[1])

    proj, w_out1 = _in_proj_scaled(xg, ssq, w_in1, w_out, 1)
    ys = _mixers(proj, bsz, *[p[1] for p in mixer_params])
    x2 = _out_proj(*ys, w_out1, x1)
    return _rmsnorm(x2, final_g, x.dtype).reshape(bsz, l, d)
```

```python
import functools
import math

import jax
import jax.numpy as jnp
from jax import lax
from jax.experimental import pallas as pl
from jax.experimental.pallas import tpu as pltpu

D_MODEL = 4096
D_POOL = 1024
D_ATTN = 2048
D_SSM = 1024
POOL_WINDOWS = (2, 4, 8, 16)
POOL_GROUP = 256
POOL_SUB = 256
HEAD_DIM = 128
N_HEADS = 16
SSM_GROUP = 16
SSM_STATE = 64
N_SSM_GROUPS = 64
D_IN = 12288
EPS = 1e-6

COL_PX_1024 = 0
COL_PGATE_1024 = 1
COL_Q_2048 = 1
COL_K_2048 = 2
COL_V_2048 = 3
COL_AGATE_2048 = 4
COL_SU_1024 = 10
COL_SGATE_1024 = 11

V7X_VMEM_BYTES = 64 * 1024 * 1024
VMEM_RESERVE_BYTES = 6 * 1024 * 1024
SUBLANES = 8
LANES = 128

LOG2_E = 1.0 / math.log(2.0)
F32_EXP2_ZERO_ABOVE = 150.0
NO_BLOCK_CARRY = 1e30
MASKED_LOGIT = -1e30

BF16 = jnp.bfloat16
F32 = jnp.float32


def _vmem_limit(nbytes):
    return int(min(nbytes * 3 // 2 + (4 << 20), V7X_VMEM_BYTES - VMEM_RESERVE_BYTES))


def _silu(x):
    return x * jax.nn.sigmoid(x)


def _resident(*shape):
    return pl.BlockSpec(shape, lambda *_: (0,) * len(shape), pipeline_mode=pl.Buffered(1))


def _rmsnorm_kernel(x_ref, g_ref, o_ref):
    x = x_ref[...]
    y = x * lax.rsqrt(jnp.mean(x * x, axis=-1, keepdims=True) + EPS)
    o_ref[...] = (y * g_ref[...]).astype(o_ref.dtype)


def _rmsnorm(x2d, g, out_dtype, tm=512):
    m, d = x2d.shape
    return pl.pallas_call(
        _rmsnorm_kernel,
        grid=(m // tm,),
        in_specs=[pl.BlockSpec((tm, d), lambda i: (i, 0)),
                  pl.BlockSpec((1, d), lambda i: (0, 0))],
        out_specs=pl.BlockSpec((tm, d), lambda i: (i, 0)),
        out_shape=jax.ShapeDtypeStruct((m, d), out_dtype),
        compiler_params=pltpu.CompilerParams(
            dimension_semantics=("parallel",),
            vmem_limit_bytes=_vmem_limit(2 * tm * d * 8)),
        name="rmsnorm",
    )(x2d, g.reshape(1, d))


W_CHUNKS = 4


def _in_proj_f32w_kernel(a_ref, w_ref, wnext_ref, o_ref, wnext_bf_ref, wbf_ref, *, n_col, steps_per_chunk):
    c = pl.program_id(0)
    i = pl.program_id(1)
    chunk_rows = w_ref.shape[0]

    @pl.when(jnp.logical_and(c < n_col, i % steps_per_chunk == 0))
    def _():
        rows = pl.ds(pl.multiple_of((i // steps_per_chunk) * chunk_rows, chunk_rows), chunk_rows)
        wbf_ref[c % 2, rows, :] = w_ref[...].astype(BF16)

    @pl.when(c > 0)
    def _():
        o_ref[...] = jnp.dot(a_ref[...], wbf_ref[(c + 1) % 2], preferred_element_type=F32).astype(o_ref.dtype)
        wnext_bf_ref[...] = wnext_ref[...].astype(BF16)


def _in_proj_f32w(h, w_all, layer, tm=1024, tn=1024):
    m, k = h.shape
    n = w_all.shape[2]
    n_col, n_row = n // tn, m // tm
    steps_per_chunk = n_row // W_CHUNKS
    chunk_rows = k // W_CHUNKS
    side_rows = k // n_row
    col = lambda c: jnp.maximum(c - 1, 0)
    row = lambda c, i: jnp.where(c == 0, 0, i)
    est = (2 * (tm * k * 2 + chunk_rows * tn * 4 + tm * tn * 2 + side_rows * tn * 6) + 2 * k * tn * 2)
    kern = functools.partial(_in_proj_f32w_kernel, n_col=n_col, steps_per_chunk=steps_per_chunk)
    return pl.pallas_call(
        kern,
        grid=(n_col + 1, n_row),
        in_specs=[pl.BlockSpec((tm, k), lambda c, i: (row(c, i), 0)),
                  pl.BlockSpec((None, chunk_rows, tn),
                               lambda c, i: (layer, i // steps_per_chunk, jnp.minimum(c, n_col - 1))),
                  pl.BlockSpec((None, side_rows, tn), lambda c, i: (layer + 1, row(c, i), col(c)))],
        out_specs=(pl.BlockSpec((tm, tn), lambda c, i: (row(c, i), col(c))),
                   pl.BlockSpec((side_rows, tn), lambda c, i: (row(c, i), col(c)))),
        out_shape=(jax.ShapeDtypeStruct((m, n), BF16), jax.ShapeDtypeStruct((k, n), BF16)),
        scratch_shapes=[pltpu.VMEM((2, k, tn), BF16)],
        compiler_params=pltpu.CompilerParams(
            dimension_semantics=("arbitrary", "arbitrary"),
            vmem_limit_bytes=_vmem_limit(est)),
        name="in_proj_f32w",
    )(h, w_all, w_all)


def _in_proj_scaled_kernel(a_ref, ssq_ref, w_ref, wout_ref, o_ref, wout_bf_ref, r_ref, *, n_cast_cols):
    j = pl.program_id(1)

    @pl.when(j == 0)
    def _():
        ssq = jnp.sum(jnp.sum(ssq_ref[...], axis=0), axis=-1, keepdims=True)
        r_ref[...] = lax.rsqrt(ssq / D_MODEL + EPS)

    acc = jnp.dot(a_ref[...], w_ref[...], preferred_element_type=F32)
    o_ref[...] = (acc * r_ref[...]).astype(o_ref.dtype)

    @pl.when(j < n_cast_cols)
    def _():
        wout_bf_ref[...] = wout_ref[...].astype(BF16)


def _in_proj_scaled(xg, ssq, w, w_out_all, layer, tm=1024, tn=1024, cast_cols=512):
    m, k = xg.shape
    n = w.shape[1]
    parts = ssq.shape[0]
    n_row, n_col = m // tm, n // tn
    ko, no = w_out_all.shape[1:]
    cast_rows = ko // n_row
    n_cast_cols = no // cast_cols
    assert n_cast_cols <= n_col
    cast_col = lambda j: jnp.minimum(j, n_cast_cols - 1)
    est = (2 * (tm * k * 2 + k * tn * 2 + tm * tn * 2 + parts * tm * LANES * 4 + cast_rows * cast_cols * 6)
           + tm * LANES * 4)
    return pl.pallas_call(
        functools.partial(_in_proj_scaled_kernel, n_cast_cols=n_cast_cols),
        grid=(n_row, n_col),
        in_specs=[pl.BlockSpec((tm, k), lambda i, j: (i, 0)),
                  pl.BlockSpec((parts, tm, LANES), lambda i, j: (0, i, 0)),
                  pl.BlockSpec((k, tn), lambda i, j: (0, j)),
                  pl.BlockSpec((None, cast_rows, cast_cols), lambda i, j: (layer, i, cast_col(j)))],
        out_specs=(pl.BlockSpec((tm, tn), lambda i, j: (i, j)),
                   pl.BlockSpec((cast_rows, cast_cols), lambda i, j: (i, cast_col(j)))),
        out_shape=(jax.ShapeDtypeStruct((m, n), BF16), jax.ShapeDtypeStruct((ko, no), BF16)),
        scratch_shapes=[pltpu.VMEM((tm, 1), F32)],
        compiler_params=pltpu.CompilerParams(
            dimension_semantics=("parallel", "arbitrary"),
            vmem_limit_bytes=_vmem_limit(est)),
        name="in_proj_scaled",
    )(xg, ssq, w, w_out_all)


def _branch_matmul(yp_ref, ya_ref, ys_ref, w_ref):
    acc = jnp.dot(yp_ref[...], w_ref[0:D_POOL, :], preferred_element_type=F32)
    acc += jnp.dot(ya_ref[...], w_ref[D_POOL:D_POOL + D_ATTN, :], preferred_element_type=F32)
    return acc + jnp.dot(ys_ref[...], w_ref[D_POOL + D_ATTN:, :], preferred_element_type=F32)


def _out_proj_f32w_kernel(yp_ref, ya_ref, ys_ref, w_ref, x_ref, g_ref, xo_ref, xg_ref, ssq_ref, wbf_ref):
    @pl.when(pl.program_id(1) == 0)
    def _():
        wbf_ref[...] = w_ref[...].astype(BF16)

    x_new = x_ref[...] + _branch_matmul(yp_ref, ya_ref, ys_ref, wbf_ref)
    xo_ref[...] = x_new
    xg_ref[...] = (x_new * g_ref[...]).astype(xg_ref.dtype)
    sq = x_new * x_new
    ssq_ref[...] = functools.reduce(
        lambda a, b: a + b, [sq[:, c * LANES:(c + 1) * LANES] for c in range(sq.shape[1] // LANES)])


def _out_proj_f32w(y_pool, y_attn, y_ssm, w_all, layer, x2d, next_g, tm=512, tn=1024):
    m = x2d.shape[0]
    k, n = w_all.shape[1:]
    est = 2 * (tm * k * 2 + tm * tn * (4 + 4 + 2) + tm * LANES * 4) + k * tn * (4 + 2)
    tile = pl.BlockSpec((tm, tn), lambda j, i: (i, j))
    return pl.pallas_call(
        _out_proj_f32w_kernel,
        grid=(n // tn, m // tm),
        in_specs=[pl.BlockSpec((tm, D_POOL), lambda j, i: (i, 0)),
                  pl.BlockSpec((tm, D_ATTN), lambda j, i: (i, 0)),
                  pl.BlockSpec((tm, D_SSM), lambda j, i: (i, 0)),
                  pl.BlockSpec((None, k, tn), lambda j, i: (layer, 0, j), pipeline_mode=pl.Buffered(1)),
                  tile,
                  pl.BlockSpec((1, tn), lambda j, i: (0, j))],
        out_specs=(tile, tile, pl.BlockSpec((None, tm, LANES), lambda j, i: (j, i, 0))),
        out_shape=(jax.ShapeDtypeStruct((m, n), F32), jax.ShapeDtypeStruct((m, n), BF16),
                   jax.ShapeDtypeStruct((n // tn, m, LANES), F32)),
        scratch_shapes=[pltpu.VMEM((k, tn), BF16)],
        compiler_params=pltpu.CompilerParams(
            dimension_semantics=("parallel", "arbitrary"),
            vmem_limit_bytes=_vmem_limit(est)),
        name="out_proj_f32w",
    )(y_pool, y_attn, y_ssm, w_all, x2d, next_g.reshape(1, n))


def _out_proj_kernel(yp_ref, ya_ref, ys_ref, w_ref, x_ref, o_ref):
    o_ref[...] = x_ref[...] + _branch_matmul(yp_ref, ya_ref, ys_ref, w_ref)


def _out_proj(y_pool, y_attn, y_ssm, w, x2d, tm=1024, tn=1024):
    m = x2d.shape[0]
    k, n = w.shape
    est = 2 * (tm * k * 2 + k * tn * 2 + 2 * tm * tn * 4)
    return pl.pallas_call(
        _out_proj_kernel,
        grid=(m // tm, n // tn),
        in_specs=[pl.BlockSpec((tm, D_POOL), lambda i, j: (i, 0)),
                  pl.BlockSpec((tm, D_ATTN), lambda i, j: (i, 0)),
                  pl.BlockSpec((tm, D_SSM), lambda i, j: (i, 0)),
                  pl.BlockSpec((k, tn), lambda i, j: (0, j)),
                  pl.BlockSpec((tm, tn), lambda i, j: (i, j))],
        out_specs=pl.BlockSpec((tm, tn), lambda i, j: (i, j)),
        out_shape=jax.ShapeDtypeStruct((m, n), F32),
        compiler_params=pltpu.CompilerParams(
            dimension_semantics=("parallel", "arbitrary"),
            vmem_limit_bytes=_vmem_limit(est)),
        name="out_proj",
    )(y_pool, y_attn, y_ssm, w, x2d)


def _pool_kernel(px_ref, prev_ref, gate_ref, w_ref, scale_ref, g_ref, o_ref, *, t_blk):
    i = pl.program_id(1)
    x_bf = px_ref[0]
    x = x_bf.astype(F32)
    x_ext = jnp.concatenate([jnp.where(i > 0, prev_ref[0], jnp.zeros_like(prev_ref[0])), x_bf], axis=0)
    pos = (i * t_blk + 1 + lax.broadcasted_iota(jnp.int32, (t_blk, 1), 0)).astype(F32)
    out_row = lax.broadcasted_iota(jnp.int32, (POOL_SUB, 2 * POOL_SUB), 0)
    in_row = lax.broadcasted_iota(jnp.int32, (POOL_SUB, 2 * POOL_SUB), 1)
    lag = out_row + POOL_SUB - in_row
    mixed = []
    for g, w in enumerate(POOL_WINDOWS):
        cols = slice(g * POOL_GROUP, (g + 1) * POOL_GROUP)
        band = jnp.where(lag >= 0, jnp.where(lag < w, 1.0, 0.0), 0.0).astype(BF16)
        s = jnp.concatenate(
            [jnp.dot(band, x_ext[r:r + 2 * POOL_SUB, cols], preferred_element_type=F32)
             for r in range(0, t_blk, POOL_SUB)], axis=0)
        pooled = s / jnp.minimum(pos, float(w)) - x[:, cols]
        mixed.append(jnp.dot(pooled.astype(BF16), w_ref[g], preferred_element_type=F32))
    y = jnp.concatenate(mixed, axis=1) * scale_ref[...]
    y = y * lax.rsqrt(jnp.mean(y * y, axis=-1, keepdims=True) + EPS) * g_ref[...]
    o_ref[0] = (y * _silu(gate_ref[0].astype(F32))).astype(o_ref.dtype)


def _pool_branch(proj3, w_pool, pool_scale, g_pool, t_blk=1024):
    b, l, _ = proj3.shape
    assert max(POOL_WINDOWS) <= POOL_SUB and t_blk % POOL_SUB == 0
    sub_per_blk = t_blk // POOL_SUB
    kern = functools.partial(_pool_kernel, t_blk=t_blk)
    return pl.pallas_call(
        kern,
        grid=(b, l // t_blk),
        in_specs=[
            pl.BlockSpec((1, t_blk, D_POOL), lambda bi, i: (bi, i, COL_PX_1024)),
            pl.BlockSpec((1, POOL_SUB, D_POOL),
                         lambda bi, i: (bi, jnp.maximum(i * sub_per_blk - 1, 0), COL_PX_1024)),
            pl.BlockSpec((1, t_blk, D_POOL), lambda bi, i: (bi, i, COL_PGATE_1024)),
            pl.BlockSpec((len(POOL_WINDOWS), POOL_GROUP, POOL_GROUP), lambda bi, i: (0, 0, 0)),
            pl.BlockSpec((1, D_POOL), lambda bi, i: (0, 0)),
            pl.BlockSpec((1, D_POOL), lambda bi, i: (0, 0)),
        ],
        out_specs=pl.BlockSpec((1, t_blk, D_POOL), lambda bi, i: (bi, i, 0)),
        out_shape=jax.ShapeDtypeStruct((b, l, D_POOL), BF16),
        compiler_params=pltpu.CompilerParams(
            dimension_semantics=("parallel", "parallel"),
            vmem_limit_bytes=_vmem_limit(8 * t_blk * D_POOL * 4)),
        name="pool_branch",
    )(proj3, proj3, proj3, w_pool, pool_scale.reshape(1, D_POOL), g_pool.reshape(1, D_POOL))


def _attn_kernel(q_ref, proj_hbm_ref, gate_ref, g_ref, o_ref, k_ref, v_ref, kv_sem, *, t_blk, n_before):
    n_head = N_HEADS
    bi = pl.program_id(0)
    i = pl.program_id(1)

    def kv_copy(which, blk):
        col0 = (COL_K_2048, COL_V_2048)[which] * D_ATTN
        rows = pl.ds(pl.multiple_of(blk * t_blk, t_blk), t_blk)
        return pltpu.make_async_copy(proj_hbm_ref.at[bi, rows, pl.ds(col0, D_ATTN)],
                                     (k_ref, v_ref)[which].at[rows, :], kv_sem.at[which, blk])

    @pl.when(i == 0)
    def _():
        for blk in range(k_ref.shape[0] // t_blk):
            kv_copy(0, blk).start()
            kv_copy(1, blk).start()

    kv_copy(0, i).wait()
    kv_copy(1, i).wait()

    row = lax.broadcasted_iota(jnp.int32, (t_blk, t_blk), 0)
    col = lax.broadcasted_iota(jnp.int32, (t_blk, t_blk), 1)
    later = jnp.where(row > col, 1.0, 0.0).astype(BF16)
    causal = col < row
    qs = [(q_ref[0, :, h * HEAD_DIM:(h + 1) * HEAD_DIM].astype(F32) * (HEAD_DIM ** -0.5 * LOG2_E)).astype(BF16)
          for h in range(n_head)]

    def tile(h, j, carry, diagonal):
        lanes = slice(h * HEAD_DIM, (h + 1) * HEAD_DIM)
        start = pl.multiple_of(j * t_blk, t_blk)
        kj = k_ref[pl.ds(start, t_blk), lanes]
        vj = v_ref[pl.ds(start, t_blk), lanes]
        z = lax.dot_general(qs[h], kj, (((1,), (1,)), ((), ())), preferred_element_type=F32)
        if diagonal:
            z = jnp.where(causal, z, MASKED_LOGIT)
        softplus = jnp.maximum(z, 0.0) + jnp.log(1.0 + jnp.exp2(-jnp.abs(z))) * LOG2_E
        suffix = jnp.dot(softplus.astype(BF16), later, preferred_element_type=F32)
        wts = jnp.exp2(z - softplus - suffix - carry)
        pv = jnp.dot(wts.astype(BF16), vj, preferred_element_type=F32)
        return pv, carry + jnp.sum(softplus, axis=1, keepdims=True)

    accs, carries = [], []
    for h in range(n_head):
        acc, carry = tile(h, i, jnp.zeros((t_blk, 1), F32), True)
        for d in range(1, n_before + 1):
            carry = jnp.where(i >= d, carry, NO_BLOCK_CARRY)
            pv, carry = tile(h, jnp.maximum(i - d, 0), carry, False)
            acc = acc + pv
        accs.append(acc)
        carries.append(carry)

    def alive_of(carries):
        return jnp.min(functools.reduce(jnp.minimum, carries)) < F32_EXP2_ZERO_ABOVE

    def cond(state):
        j, _, _, alive = state
        return jnp.logical_and(j >= 0, alive)

    def body(state):
        j, accs, carries, _ = state
        new_accs, new_carries = [], []
        for h in range(n_head):
            pv, carry = tile(h, j, carries[h], False)
            new_accs.append(accs[h] + pv)
            new_carries.append(carry)
        return j - 1, tuple(new_accs), tuple(new_carries), alive_of(new_carries)

    def finish(accs):
        y = jnp.concatenate(accs, axis=1)
        y = y * lax.rsqrt(jnp.mean(y * y, axis=-1, keepdims=True) + EPS) * g_ref[...]
        o_ref[0] = (y * _silu(gate_ref[0].astype(F32))).astype(o_ref.dtype)

    finish(accs)
    first = i - n_before - 1

    @pl.when(jnp.logical_and(first >= 0, alive_of(carries)))
    def _():
        _, more, _, _ = lax.while_loop(cond, body, (first, tuple(accs), tuple(carries), True))
        finish(more)


def _attn_branch(proj3, g_attn, t_blk=256, n_before=1):
    b, l, _ = proj3.shape
    kern = functools.partial(_attn_kernel, t_blk=t_blk, n_before=n_before)
    est = 2 * l * D_ATTN * 2 + 2 * 3 * t_blk * D_ATTN * 2 + N_HEADS * 12 * t_blk * t_blk * 4
    q_rows = lambda col: pl.BlockSpec((1, t_blk, D_ATTN), lambda bi, i: (bi, i, col))
    return pl.pallas_call(
        kern,
        grid=(b, l // t_blk),
        in_specs=[q_rows(COL_Q_2048), pl.BlockSpec(memory_space=pl.ANY), q_rows(COL_AGATE_2048),
                  pl.BlockSpec((1, D_ATTN), lambda bi, i: (0, 0))],
        out_specs=pl.BlockSpec((1, t_blk, D_ATTN), lambda bi, i: (bi, i, 0)),
        out_shape=jax.ShapeDtypeStruct((b, l, D_ATTN), BF16),
        scratch_shapes=[pltpu.VMEM((l, D_ATTN), BF16), pltpu.VMEM((l, D_ATTN), BF16),
                        pltpu.SemaphoreType.DMA((2, l // t_blk))],
        compiler_params=pltpu.CompilerParams(
            dimension_semantics=("arbitrary", "arbitrary"),
            vmem_limit_bytes=_vmem_limit(est)),
        name="attn_branch",
    )(proj3, proj3, proj3, g_attn.reshape(1, D_ATTN))


N_SLAB = 4
SLAB_IN = D_SSM // N_SLAB
SLAB_STATES = 1024
GROUPS_PER_SLAB = N_SSM_GROUPS // N_SLAB
N_CHUNK = SLAB_STATES // LANES
TILE_PITCH = SUBLANES + 1


def _block_diag(ref, rows_per_group, cols_per_group):
    n_rows = ref.shape[1]
    n_cols = GROUPS_PER_SLAB * cols_per_group
    row_group = lax.broadcasted_iota(jnp.int32, (n_rows, n_cols), 0) // rows_per_group
    col_group = lax.broadcasted_iota(jnp.int32, (n_rows, n_cols), 1) // cols_per_group
    return jnp.where((row_group == col_group)[None], jnp.concatenate([ref[...]] * GROUPS_PER_SLAB, axis=-1), 0.0)


def _discretize_kernel(lam_re_ref, lam_im_ref, log_dt_ref, b_re_ref, b_im_ref, c_re_ref, c_im_ref,
                       a_re_ref, a_im_ref, wb_re_ref, wb_im_ref, wc_re_ref, wc_im_ref):
    lam_re = lam_re_ref[...]
    lam_im = lam_im_ref[...]
    dt = jnp.exp(log_dt_ref[...])
    mag = jnp.exp(lam_re * dt)
    a_re = mag * jnp.cos(lam_im * dt)
    a_im = mag * jnp.sin(lam_im * dt)
    a_re_ref[...] = a_re
    a_im_ref[...] = a_im
    num_re = a_re - 1.0
    den = lam_re * lam_re + lam_im * lam_im
    c_re = ((num_re * lam_re + a_im * lam_im) / den)[:, None, :]
    c_im = ((a_im * lam_re - num_re * lam_im) / den)[:, None, :]
    b_re = _block_diag(b_re_ref, SSM_GROUP, SSM_STATE)
    b_im = _block_diag(b_im_ref, SSM_GROUP, SSM_STATE)
    wb_re_ref[...] = (c_re * b_re - c_im * b_im).astype(wb_re_ref.dtype)
    wb_im_ref[...] = (c_re * b_im + c_im * b_re).astype(wb_im_ref.dtype)
    wc_re_ref[...] = _block_diag(c_re_ref, SSM_STATE, SSM_GROUP).astype(wc_re_ref.dtype)
    wc_im_ref[...] = _block_diag(c_im_ref, SSM_STATE, SSM_GROUP).astype(wc_im_ref.dtype)


def _discretize(lam_re, lam_im, log_dt, b_re, b_im, c_re, c_im):
    shape4 = (N_SLAB, SLAB_STATES)
    log_dt_full = jnp.broadcast_to(log_dt[:, None], (N_SSM_GROUPS, SSM_STATE)).reshape(shape4)
    b_rows = lambda b: jnp.swapaxes(b, 1, 2).reshape(N_SLAB, SLAB_IN, SSM_STATE)
    c_rows = lambda c: jnp.swapaxes(c, 1, 2).reshape(N_SLAB, SLAB_STATES, SSM_GROUP)
    out_shape = (jax.ShapeDtypeStruct(shape4, F32), jax.ShapeDtypeStruct(shape4, F32),
                 jax.ShapeDtypeStruct((N_SLAB, SLAB_IN, SLAB_STATES), BF16),
                 jax.ShapeDtypeStruct((N_SLAB, SLAB_IN, SLAB_STATES), BF16),
                 jax.ShapeDtypeStruct((N_SLAB, SLAB_STATES, SLAB_IN), BF16),
                 jax.ShapeDtypeStruct((N_SLAB, SLAB_STATES, SLAB_IN), BF16))
    return pl.pallas_call(
        _discretize_kernel, out_shape=out_shape, name="s5_discretize",
        compiler_params=pltpu.CompilerParams(vmem_limit_bytes=_vmem_limit(32 << 20)),
    )(lam_re.reshape(shape4), lam_im.reshape(shape4), log_dt_full,
      b_rows(b_re), b_rows(b_im), c_rows(c_re), c_rows(c_im))


def _s5_kernel(u_ref, gate_ref, a_re_ref, a_im_ref, wb_re_ref, wb_im_ref, wc_re_ref, wc_im_ref,
               dskip_ref, wglu_ref, bglu_ref, g_ref, o_ref,
               s_re_ref, s_im_ref, x_re_ref, x_im_ref, *, t_blk, n_batch):
    n_row = n_batch * N_SLAB
    assert n_row == SUBLANES

    @pl.when(pl.program_id(0) == 0)
    def _():
        x_re_ref[...] = jnp.zeros_like(x_re_ref)
        x_im_ref[...] = jnp.zeros_like(x_im_ref)

    for j in range(N_SLAB):
        cols = slice(j * SLAB_IN, (j + 1) * SLAB_IN)
        uj = jnp.concatenate([u_ref[b, :, cols] for b in range(n_batch)], axis=0)
        bu_re = jnp.dot(uj, wb_re_ref[j], preferred_element_type=F32)
        bu_im = jnp.dot(uj, wb_im_ref[j], preferred_element_type=F32)
        for b in range(n_batch):
            rows = slice(b * t_blk, (b + 1) * t_blk)
            k = b * N_SLAB + j
            for c in range(N_CHUNK):
                lanes = slice(c * LANES, (c + 1) * LANES)
                s_re_ref[c, pl.ds(k, t_blk, stride=TILE_PITCH), :] = bu_re[rows, lanes]
                s_im_ref[c, pl.ds(k, t_blk, stride=TILE_PITCH), :] = bu_im[rows, lanes]

    a_re = [a_re_ref[:, c * LANES:(c + 1) * LANES] for c in range(N_CHUNK)]
    a_im = [a_im_ref[:, c * LANES:(c + 1) * LANES] for c in range(N_CHUNK)]
    x_re0 = tuple(x_re_ref[:, c * LANES:(c + 1) * LANES] for c in range(N_CHUNK))
    x_im0 = tuple(x_im_ref[:, c * LANES:(c + 1) * LANES] for c in range(N_CHUNK))

    def step(t, state):
        x_re, x_im = state
        rows = pl.ds(t * TILE_PITCH, n_row)
        new_re, new_im = [], []
        for c in range(N_CHUNK):
            nr = a_re[c] * x_re[c] - a_im[c] * x_im[c] + s_re_ref[c, rows, :]
            ni = a_re[c] * x_im[c] + a_im[c] * x_re[c] + s_im_ref[c, rows, :]
            s_re_ref[c, rows, :] = nr
            s_im_ref[c, rows, :] = ni
            new_re.append(nr)
            new_im.append(ni)
        return tuple(new_re), tuple(new_im)

    x_re, x_im = lax.fori_loop(0, t_blk, step, (x_re0, x_im0), unroll=4)
    for c in range(N_CHUNK):
        x_re_ref[:, c * LANES:(c + 1) * LANES] = x_re[c]
        x_im_ref[:, c * LANES:(c + 1) * LANES] = x_im[c]

    def states(s_ref, j):
        return jnp.concatenate(
            [jnp.concatenate([s_ref[c, pl.ds(b * N_SLAB + j, t_blk, stride=TILE_PITCH), :] for c in range(N_CHUNK)],
                             axis=1) for b in range(n_batch)], axis=0).astype(BF16)

    ys = [jnp.dot(states(s_re_ref, j), wc_re_ref[j], preferred_element_type=F32)
          - jnp.dot(states(s_im_ref, j), wc_im_ref[j], preferred_element_type=F32) for j in range(N_SLAB)]
    u_all = jnp.concatenate([u_ref[b] for b in range(n_batch)], axis=0).astype(F32)
    y = jnp.concatenate(ys, axis=1) + dskip_ref[...] * u_all
    h = jax.nn.gelu(y)
    glu = jnp.dot(h.astype(BF16), wglu_ref[...], preferred_element_type=F32) + bglu_ref[...]
    out = glu[:, :D_SSM] * jax.nn.sigmoid(glu[:, D_SSM:])
    out = out * lax.rsqrt(jnp.mean(out * out, axis=-1, keepdims=True) + EPS) * g_ref[...]
    for b in range(n_batch):
        rows = slice(b * t_blk, (b + 1) * t_blk)
        o_ref[b] = (out[rows] * _silu(gate_ref[b].astype(F32))).astype(o_ref.dtype)


def _s5_branch(proj3, a_re, a_im, wb_re, wb_im, wc_re, wc_im, d_skip, w_glu, b_glu, g_ssm, t_blk=256):
    b, l, _ = proj3.shape
    n_row = b * N_SLAB
    a_re_t = jnp.tile(a_re, (b, 1))
    a_im_t = jnp.tile(a_im, (b, 1))
    kern = functools.partial(_s5_kernel, t_blk=t_blk, n_batch=b)
    weights = 4 * N_SLAB * SLAB_IN * SLAB_STATES * 2 + D_SSM * 2 * D_SSM * 2
    est = (weights + 2 * N_CHUNK * t_blk * TILE_PITCH * LANES * 4 + 2 * 3 * b * t_blk * D_SSM * 2
           + 6 * b * t_blk * 2 * D_SSM * 4)
    return pl.pallas_call(
        kern,
        grid=(l // t_blk,),
        in_specs=[
            pl.BlockSpec((b, t_blk, D_SSM), lambda i: (0, i, COL_SU_1024)),
            pl.BlockSpec((b, t_blk, D_SSM), lambda i: (0, i, COL_SGATE_1024)),
            _resident(n_row, SLAB_STATES), _resident(n_row, SLAB_STATES),
            _resident(N_SLAB, SLAB_IN, SLAB_STATES), _resident(N_SLAB, SLAB_IN, SLAB_STATES),
            _resident(N_SLAB, SLAB_STATES, SLAB_IN), _resident(N_SLAB, SLAB_STATES, SLAB_IN),
            _resident(1, D_SSM), _resident(D_SSM, 2 * D_SSM), _resident(1, 2 * D_SSM), _resident(1, D_SSM),
        ],
        out_specs=pl.BlockSpec((b, t_blk, D_SSM), lambda i: (0, i, 0)),
        out_shape=jax.ShapeDtypeStruct((b, l, D_SSM), BF16),
        scratch_shapes=[pltpu.VMEM((N_CHUNK, t_blk * TILE_PITCH, LANES), F32),
                        pltpu.VMEM((N_CHUNK, t_blk * TILE_PITCH, LANES), F32),
                        pltpu.VMEM((n_row, SLAB_STATES), F32),
                        pltpu.VMEM((n_row, SLAB_STATES), F32)],
        compiler_params=pltpu.CompilerParams(
            dimension_semantics=("arbitrary",),
            vmem_limit_bytes=_vmem_limit(est)),
        name="s5_branch",
    )(proj3, proj3, a_re_t, a_im_t, wb_re, wb_im, wc_re, wc_im,
      d_skip.reshape(1, D_SSM), w_glu, b_glu.reshape(1, 2 * D_SSM), g_ssm.reshape(1, D_SSM))


def _mixers(proj, bsz, w_pool, pool_scale, lam_re, lam_im, log_dt,
            b_re, b_im, c_re, c_im, d_skip, w_glu, b_glu, branch_g):
    m = proj.shape[0]
    proj3 = proj.reshape(bsz, m // bsz, D_IN)
    g_pool = branch_g[:D_POOL]
    g_attn = branch_g[D_POOL:D_POOL + D_ATTN]
    g_ssm = branch_g[D_POOL + D_ATTN:]
    y_pool = _pool_branch(proj3, w_pool.astype(BF16), pool_scale, g_pool)
    y_attn = _attn_branch(proj3, g_attn)
    a_re, a_im, wb_re, wb_im, wc_re, wc_im = _discretize(lam_re, lam_im, log_dt, b_re, b_im, c_re, c_im)
    y_ssm = _s5_branch(proj3, a_re, a_im, wb_re, wb_im, wc_re, wc_im, d_skip, w_glu.astype(BF16), b_glu, g_ssm)
    return y_pool.reshape(m, D_POOL), y_attn.reshape(m, D_ATTN), y_ssm.reshape(m, D_SSM)


def kernel(x, ln_g, w_in, w_pool, pool_scale, lam_re, lam_im, log_dt, b_re, b_im, c_re, c_im,
           d_skip, w_glu, b_glu, branch_g, w_out, final_g):
    bsz, l, d = x.shape
    assert ln_g.shape[0] == 2
    mixer_params = (w_pool, pool_scale, lam_re, lam_im, log_dt, b_re, b_im, c_re, c_im,
                    d_skip, w_glu, b_glu, branch_g)
    x0 = x.reshape(bsz * l, d)

    h = _rmsnorm(x0, ln_g[0], BF16)
    proj, w_in1 = _in_proj_f32w(h, w_in, 0)
    ys = _mixers(proj, bsz, *[p[0] for p in mixer_params])
    x1, xg, ssq = _out_proj_f32w(*ys, w_out, 0, x0, ln_g[1])

    proj, w_out1 = _in_proj_scaled(xg, ssq, w_in1, w_out, 1)
    ys = _mixers(proj, bsz, *[p[1] for p in mixer_params])
    x2 = _out_proj(*ys, w_out1, x1)
    return _rmsnorm(x2, final_g, x.dtype).reshape(bsz, l, d)
```

```python
import functools
import math

import jax
import jax.numpy as jnp
from jax import lax
from jax.experimental import pallas as pl
from jax.experimental.pallas import tpu as pltpu

D_MODEL = 4096
D_POOL = 1024
D_ATTN = 2048
D_SSM = 1024
POOL_WINDOWS = (2, 4, 8, 16)
POOL_GROUP = 256
POOL_SUB = 256
HEAD_DIM = 128
N_HEADS = 16
SSM_GROUP = 16
SSM_STATE = 64
N_SSM_GROUPS = 64
D_IN = 12288
EPS = 1e-6

COL_PX_1024 = 0
COL_PGATE_1024 = 1
COL_Q_2048 = 1
COL_K_2048 = 2
COL_V_2048 = 3
COL_AGATE_2048 = 4
COL_SU_1024 = 10
COL_SGATE_1024 = 11

V7X_VMEM_BYTES = 64 * 1024 * 1024
VMEM_RESERVE_BYTES = 6 * 1024 * 1024
SUBLANES = 8
LANES = 128

LOG2_E = 1.0 / math.log(2.0)
F32_EXP2_ZERO_ABOVE = 150.0
NO_BLOCK_CARRY = 1e30
MASKED_LOGIT = -1e30

BF16 = jnp.bfloat16
F32 = jnp.float32


def _vmem_limit(nbytes):
    return int(min(nbytes * 3 // 2 + (4 << 20), V7X_VMEM_BYTES - VMEM_RESERVE_BYTES))


def _silu(x):
    return x * jax.nn.sigmoid(x)


def _resident(*shape):
    return pl.BlockSpec(shape, lambda *_: (0,) * len(shape), pipeline_mode=pl.Buffered(1))


def _rmsnorm_kernel(x_ref, g_ref, o_ref):
    x = x_ref[...]
    y = x * lax.rsqrt(jnp.mean(x * x, axis=-1, keepdims=True) + EPS)
    o_ref[...] = (y * g_ref[...]).astype(o_ref.dtype)


def _rmsnorm(x2d, g, out_dtype, tm=512):
    m, d = x2d.shape
    return pl.pallas_call(
        _rmsnorm_kernel,
        grid=(m // tm,),
        in_specs=[pl.BlockSpec((tm, d), lambda i: (i, 0)),
                  pl.BlockSpec((1, d), lambda i: (0, 0))],
        out_specs=pl.BlockSpec((tm, d), lambda i: (i, 0)),
        out_shape=jax.ShapeDtypeStruct((m, d), out_dtype),
        compiler_params=pltpu.CompilerParams(
            dimension_semantics=("parallel",),
            vmem_limit_bytes=_vmem_limit(2 * tm * d * 8)),
        name="rmsnorm",
    )(x2d, g.reshape(1, d))


W_CHUNKS = 4


def _in_proj_f32w_kernel(a_ref, w_ref, wnext_ref, o_ref, wnext_bf_ref, wbf_ref, *, n_col, steps_per_chunk):
    c = pl.program_id(0)
    i = pl.program_id(1)
    chunk_rows = w_ref.shape[0]

    @pl.when(jnp.logical_and(c < n_col, i % steps_per_chunk == 0))
    def _():
        rows = pl.ds(pl.multiple_of((i // steps_per_chunk) * chunk_rows, chunk_rows), chunk_rows)
        wbf_ref[c % 2, rows, :] = w_ref[...].astype(BF16)

    @pl.when(c > 0)
    def _():
        o_ref[...] = jnp.dot(a_ref[...], wbf_ref[(c + 1) % 2], preferred_element_type=F32).astype(o_ref.dtype)
        wnext_bf_ref[...] = wnext_ref[...].astype(BF16)


def _in_proj_f32w(h, w_all, layer, tm=1024, tn=1024):
    m, k = h.shape
    n = w_all.shape[2]
    n_col, n_row = n // tn, m // tm
    steps_per_chunk = n_row // W_CHUNKS
    chunk_rows = k // W_CHUNKS
    side_rows = k // n_row
    col = lambda c: jnp.maximum(c - 1, 0)
    row = lambda c, i: jnp.where(c == 0, 0, i)
    est = (2 * (tm * k * 2 + chunk_rows * tn * 4 + tm * tn * 2 + side_rows * tn * 6) + 2 * k * tn * 2)
    kern = functools.partial(_in_proj_f32w_kernel, n_col=n_col, steps_per_chunk=steps_per_chunk)
    return pl.pallas_call(
        kern,
        grid=(n_col + 1, n_row),
        in_specs=[pl.BlockSpec((tm, k), lambda c, i: (row(c, i), 0)),
                  pl.BlockSpec((None, chunk_rows, tn),
                               lambda c, i: (layer, i // steps_per_chunk, jnp.minimum(c, n_col - 1))),
                  pl.BlockSpec((None, side_rows, tn), lambda c, i: (layer + 1, row(c, i), col(c)))],
        out_specs=(pl.BlockSpec((tm, tn), lambda c, i: (row(c, i), col(c))),
                   pl.BlockSpec((side_rows, tn), lambda c, i: (row(c, i), col(c)))),
        out_shape=(jax.ShapeDtypeStruct((m, n), BF16), jax.ShapeDtypeStruct((k, n), BF16)),
        scratch_shapes=[pltpu.VMEM((2, k, tn), BF16)],
        compiler_params=pltpu.CompilerParams(
            dimension_semantics=("arbitrary", "arbitrary"),
            vmem_limit_bytes=_vmem_limit(est)),
        name="in_proj_f32w",
    )(h, w_all, w_all)


def _in_proj_scaled_kernel(a_ref, ssq_ref, w_ref, wout_ref, o_ref, wout_bf_ref, r_ref, *, n_cast_cols):
    j = pl.program_id(1)

    @pl.when(j == 0)
    def _():
        ssq = jnp.sum(jnp.sum(ssq_ref[...], axis=0), axis=-1, keepdims=True)
        r_ref[...] = lax.rsqrt(ssq / D_MODEL + EPS)

    acc = jnp.dot(a_ref[...], w_ref[...], preferred_element_type=F32)
    o_ref[...] = (acc * r_ref[...]).astype(o_ref.dtype)

    @pl.when(j < n_cast_cols)
    def _():
        wout_bf_ref[...] = wout_ref[...].astype(BF16)


def _in_proj_scaled(xg, ssq, w, w_out_all, layer, tm=1024, tn=1024, cast_cols=512):
    m, k = xg.shape
    n = w.shape[1]
    parts = ssq.shape[0]
    n_row, n_col = m // tm, n // tn
    ko, no = w_out_all.shape[1:]
    cast_rows = ko // n_row
    n_cast_cols = no // cast_cols
    assert n_cast_cols <= n_col
    cast_col = lambda j: jnp.minimum(j, n_cast_cols - 1)
    est = (2 * (tm * k * 2 + k * tn * 2 + tm * tn * 2 + parts * tm * LANES * 4 + cast_rows * cast_cols * 6)
           + tm * LANES * 4)
    return pl.pallas_call(
        functools.partial(_in_proj_scaled_kernel, n_cast_cols=n_cast_cols),
        grid=(n_row, n_col),
        in_specs=[pl.BlockSpec((tm, k), lambda i, j: (i, 0)),
                  pl.BlockSpec((parts, tm, LANES), lambda i, j: (0, i, 0)),
                  pl.BlockSpec((k, tn), lambda i, j: (0, j)),
                  pl.BlockSpec((None, cast_rows, cast_cols), lambda i, j: (layer, i, cast_col(j)))],
        out_specs=(pl.BlockSpec((tm, tn), lambda i, j: (i, j)),
                   pl.BlockSpec((cast_rows, cast_cols), lambda i, j: (i, cast_col(j)))),
        out_shape=(jax.ShapeDtypeStruct((m, n), BF16), jax.ShapeDtypeStruct((ko, no), BF16)),
        scratch_shapes=[pltpu.VMEM((tm, 1), F32)],
        compiler_params=pltpu.CompilerParams(
            dimension_semantics=("parallel", "arbitrary"),
            vmem_limit_bytes=_vmem_limit(est)),
        name="in_proj_scaled",
    )(xg, ssq, w, w_out_all)


def _branch_matmul(yp_ref, ya_ref, ys_ref, w_ref):
    acc = jnp.dot(yp_ref[...], w_ref[0:D_POOL, :], preferred_element_type=F32)
    acc += jnp.dot(ya_ref[...], w_ref[D_POOL:D_POOL + D_ATTN, :], preferred_element_type=F32)
    return acc + jnp.dot(ys_ref[...], w_ref[D_POOL + D_ATTN:, :], preferred_element_type=F32)


def _out_proj_f32w_kernel(yp_ref, ya_ref, ys_ref, w_ref, x_ref, g_ref, xo_ref, xg_ref, ssq_ref, wbf_ref):
    @pl.when(pl.program_id(1) == 0)
    def _():
        wbf_ref[...] = w_ref[...].astype(BF16)

    x_new = x_ref[...] + _branch_matmul(yp_ref, ya_ref, ys_ref, wbf_ref)
    xo_ref[...] = x_new
    xg_ref[...] = (x_new * g_ref[...]).astype(xg_ref.dtype)
    sq = x_new * x_new
    ssq_ref[...] = functools.reduce(
        lambda a, b: a + b, [sq[:, c * LANES:(c + 1) * LANES] for c in range(sq.shape[1] // LANES)])


def _out_proj_f32w(y_pool, y_attn, y_ssm, w_all, layer, x2d, next_g, tm=512, tn=1024):
    m = x2d.shape[0]
    k, n = w_all.shape[1:]
    est = 2 * (tm * k * 2 + tm * tn * (4 + 4 + 2) + tm * LANES * 4) + k * tn * (4 + 2)
    tile = pl.BlockSpec((tm, tn), lambda j, i: (i, j))
    return pl.pallas_call(
        _out_proj_f32w_kernel,
        grid=(n // tn, m // tm),
        in_specs=[pl.BlockSpec((tm, D_POOL), lambda j, i: (i, 0)),
                  pl.BlockSpec((tm, D_ATTN), lambda j, i: (i, 0)),
                  pl.BlockSpec((tm, D_SSM), lambda j, i: (i, 0)),
                  pl.BlockSpec((None, k, tn), lambda j, i: (layer, 0, j), pipeline_mode=pl.Buffered(1)),
                  tile,
                  pl.BlockSpec((1, tn), lambda j, i: (0, j))],
        out_specs=(tile, tile, pl.BlockSpec((None, tm, LANES), lambda j, i: (j, i, 0))),
        out_shape=(jax.ShapeDtypeStruct((m, n), F32), jax.ShapeDtypeStruct((m, n), BF16),
                   jax.ShapeDtypeStruct((n // tn, m, LANES), F32)),
        scratch_shapes=[pltpu.VMEM((k, tn), BF16)],
        compiler_params=pltpu.CompilerParams(
            dimension_semantics=("parallel", "arbitrary"),
            vmem_limit_bytes=_vmem_limit(est)),
        name="out_proj_f32w",
    )(y_pool, y_attn, y_ssm, w_all, x2d, next_g.reshape(1, n))


def _out_proj_final_kernel(yp_ref, ya_ref, ys_ref, w_ref, x_ref, g_ref, o_ref):
    x_new = x_ref[...] + _branch_matmul(yp_ref, ya_ref, ys_ref, w_ref)
    y = x_new * lax.rsqrt(jnp.mean(x_new * x_new, axis=-1, keepdims=True) + EPS)
    o_ref[...] = (y * g_ref[...]).astype(o_ref.dtype)


def _out_proj_final(y_pool, y_attn, y_ssm, w, x2d, g, tm=256):
    m = x2d.shape[0]
    k, n = w.shape
    est = k * n * 2 + 2 * (tm * k * 2 + 2 * tm * n * 4) + tm * n * 4
    return pl.pallas_call(
        _out_proj_final_kernel,
        grid=(m // tm,),
        in_specs=[pl.BlockSpec((tm, D_POOL), lambda i: (i, 0)),
                  pl.BlockSpec((tm, D_ATTN), lambda i: (i, 0)),
                  pl.BlockSpec((tm, D_SSM), lambda i: (i, 0)),
                  _resident(k, n),
                  pl.BlockSpec((tm, n), lambda i: (i, 0)),
                  _resident(1, n)],
        out_specs=pl.BlockSpec((tm, n), lambda i: (i, 0)),
        out_shape=jax.ShapeDtypeStruct((m, n), F32),
        compiler_params=pltpu.CompilerParams(
            dimension_semantics=("parallel",),
            vmem_limit_bytes=_vmem_limit(est)),
        name="out_proj_final",
    )(y_pool, y_attn, y_ssm, w, x2d, g.reshape(1, n))


def _pool_kernel(px_ref, prev_ref, gate_ref, w_ref, scale_ref, g_ref, o_ref, *, t_blk):
    i = pl.program_id(1)
    x_bf = px_ref[0]
    x = x_bf.astype(F32)
    x_ext = jnp.concatenate([jnp.where(i > 0, prev_ref[0], jnp.zeros_like(prev_ref[0])), x_bf], axis=0)
    pos = (i * t_blk + 1 + lax.broadcasted_iota(jnp.int32, (t_blk, 1), 0)).astype(F32)
    out_row = lax.broadcasted_iota(jnp.int32, (POOL_SUB, 2 * POOL_SUB), 0)
    in_row = lax.broadcasted_iota(jnp.int32, (POOL_SUB, 2 * POOL_SUB), 1)
    lag = out_row + POOL_SUB - in_row
    mixed = []
    for g, w in enumerate(POOL_WINDOWS):
        cols = slice(g * POOL_GROUP, (g + 1) * POOL_GROUP)
        band = jnp.where(lag >= 0, jnp.where(lag < w, 1.0, 0.0), 0.0).astype(BF16)
        s = jnp.concatenate(
            [jnp.dot(band, x_ext[r:r + 2 * POOL_SUB, cols], preferred_element_type=F32)
             for r in range(0, t_blk, POOL_SUB)], axis=0)
        pooled = s / jnp.minimum(pos, float(w)) - x[:, cols]
        mixed.append(jnp.dot(pooled.astype(BF16), w_ref[g], preferred_element_type=F32))
    y = jnp.concatenate(mixed, axis=1) * scale_ref[...]
    y = y * lax.rsqrt(jnp.mean(y * y, axis=-1, keepdims=True) + EPS) * g_ref[...]
    o_ref[0] = (y * _silu(gate_ref[0].astype(F32))).astype(o_ref.dtype)


def _pool_branch(proj3, w_pool, pool_scale, g_pool, t_blk=1024):
    b, l, _ = proj3.shape
    assert max(POOL_WINDOWS) <= POOL_SUB and t_blk % POOL_SUB == 0
    sub_per_blk = t_blk // POOL_SUB
    kern = functools.partial(_pool_kernel, t_blk=t_blk)
    return pl.pallas_call(
        kern,
        grid=(b, l // t_blk),
        in_specs=[
            pl.BlockSpec((1, t_blk, D_POOL), lambda bi, i: (bi, i, COL_PX_1024)),
            pl.BlockSpec((1, POOL_SUB, D_POOL),
                         lambda bi, i: (bi, jnp.maximum(i * sub_per_blk - 1, 0), COL_PX_1024)),
            pl.BlockSpec((1, t_blk, D_POOL), lambda bi, i: (bi, i, COL_PGATE_1024)),
            pl.BlockSpec((len(POOL_WINDOWS), POOL_GROUP, POOL_GROUP), lambda bi, i: (0, 0, 0)),
            pl.BlockSpec((1, D_POOL), lambda bi, i: (0, 0)),
            pl.BlockSpec((1, D_POOL), lambda bi, i: (0, 0)),
        ],
        out_specs=pl.BlockSpec((1, t_blk, D_POOL), lambda bi, i: (bi, i, 0)),
        out_shape=jax.ShapeDtypeStruct((b, l, D_POOL), BF16),
        compiler_params=pltpu.CompilerParams(
            dimension_semantics=("parallel", "parallel"),
            vmem_limit_bytes=_vmem_limit(8 * t_blk * D_POOL * 4)),
        name="pool_branch",
    )(proj3, proj3, proj3, w_pool, pool_scale.reshape(1, D_POOL), g_pool.reshape(1, D_POOL))


def _attn_kernel(q_ref, proj_hbm_ref, gate_ref, g_ref, o_ref, k_ref, v_ref, kv_sem, *, t_blk, n_before):
    n_head = N_HEADS
    bi = pl.program_id(0)
    i = pl.program_id(1)

    def kv_copy(which, blk):
        col0 = (COL_K_2048, COL_V_2048)[which] * D_ATTN
        rows = pl.ds(pl.multiple_of(blk * t_blk, t_blk), t_blk)
        return pltpu.make_async_copy(proj_hbm_ref.at[bi, rows, pl.ds(col0, D_ATTN)],
                                     (k_ref, v_ref)[which].at[rows, :], kv_sem.at[which, blk])

    @pl.when(i == 0)
    def _():
        for blk in range(k_ref.shape[0] // t_blk):
            kv_copy(0, blk).start()
            kv_copy(1, blk).start()

    kv_copy(0, i).wait()
    kv_copy(1, i).wait()

    row = lax.broadcasted_iota(jnp.int32, (t_blk, t_blk), 0)
    col = lax.broadcasted_iota(jnp.int32, (t_blk, t_blk), 1)
    later = jnp.where(row > col, 1.0, 0.0).astype(BF16)
    causal = col < row
    qs = [(q_ref[0, :, h * HEAD_DIM:(h + 1) * HEAD_DIM].astype(F32) * (HEAD_DIM ** -0.5 * LOG2_E)).astype(BF16)
          for h in range(n_head)]

    def tile(h, j, carry, diagonal):
        lanes = slice(h * HEAD_DIM, (h + 1) * HEAD_DIM)
        start = pl.multiple_of(j * t_blk, t_blk)
        kj = k_ref[pl.ds(start, t_blk), lanes]
        vj = v_ref[pl.ds(start, t_blk), lanes]
        z = lax.dot_general(qs[h], kj, (((1,), (1,)), ((), ())), preferred_element_type=F32)
        if diagonal:
            z = jnp.where(causal, z, MASKED_LOGIT)
        softplus = jnp.maximum(z, 0.0) + jnp.log(1.0 + jnp.exp2(-jnp.abs(z))) * LOG2_E
        suffix = jnp.dot(softplus.astype(BF16), later, preferred_element_type=F32)
        wts = jnp.exp2(z - softplus - suffix - carry)
        pv = jnp.dot(wts.astype(BF16), vj, preferred_element_type=F32)
        return pv, carry + jnp.sum(softplus, axis=1, keepdims=True)

    accs, carries = [], []
    for h in range(n_head):
        acc, carry = tile(h, i, jnp.zeros((t_blk, 1), F32), True)
        for d in range(1, n_before + 1):
            carry = jnp.where(i >= d, carry, NO_BLOCK_CARRY)
            pv, carry = tile(h, jnp.maximum(i - d, 0), carry, False)
            acc = acc + pv
        accs.append(acc)
        carries.append(carry)

    def alive_of(carries):
        return jnp.min(functools.reduce(jnp.minimum, carries)) < F32_EXP2_ZERO_ABOVE

    def cond(state):
        j, _, _, alive = state
        return jnp.logical_and(j >= 0, alive)

    def body(state):
        j, accs, carries, _ = state
        new_accs, new_carries = [], []
        for h in range(n_head):
            pv, carry = tile(h, j, carries[h], False)
            new_accs.append(accs[h] + pv)
            new_carries.append(carry)
        return j - 1, tuple(new_accs), tuple(new_carries), alive_of(new_carries)

    def finish(accs):
        y = jnp.concatenate(accs, axis=1)
        y = y * lax.rsqrt(jnp.mean(y * y, axis=-1, keepdims=True) + EPS) * g_ref[...]
        o_ref[0] = (y * _silu(gate_ref[0].astype(F32))).astype(o_ref.dtype)

    finish(accs)
    first = i - n_before - 1

    @pl.when(jnp.logical_and(first >= 0, alive_of(carries)))
    def _():
        _, more, _, _ = lax.while_loop(cond, body, (first, tuple(accs), tuple(carries), True))
        finish(more)


def _attn_branch(proj3, g_attn, t_blk=256, n_before=1):
    b, l, _ = proj3.shape
    kern = functools.partial(_attn_kernel, t_blk=t_blk, n_before=n_before)
    est = 2 * l * D_ATTN * 2 + 2 * 3 * t_blk * D_ATTN * 2 + N_HEADS * 12 * t_blk * t_blk * 4
    q_rows = lambda col: pl.BlockSpec((1, t_blk, D_ATTN), lambda bi, i: (bi, i, col))
    return pl.pallas_call(
        kern,
        grid=(b, l // t_blk),
        in_specs=[q_rows(COL_Q_2048), pl.BlockSpec(memory_space=pl.ANY), q_rows(COL_AGATE_2048),
                  pl.BlockSpec((1, D_ATTN), lambda bi, i: (0, 0))],
        out_specs=pl.BlockSpec((1, t_blk, D_ATTN), lambda bi, i: (bi, i, 0)),
        out_shape=jax.ShapeDtypeStruct((b, l, D_ATTN), BF16),
        scratch_shapes=[pltpu.VMEM((l, D_ATTN), BF16), pltpu.VMEM((l, D_ATTN), BF16),
                        pltpu.SemaphoreType.DMA((2, l // t_blk))],
        compiler_params=pltpu.CompilerParams(
            dimension_semantics=("arbitrary", "arbitrary"),
            vmem_limit_bytes=_vmem_limit(est)),
        name="attn_branch",
    )(proj3, proj3, proj3, g_attn.reshape(1, D_ATTN))


N_SLAB = 4
SLAB_IN = D_SSM // N_SLAB
SLAB_STATES = 1024
GROUPS_PER_SLAB = N_SSM_GROUPS // N_SLAB
N_CHUNK = SLAB_STATES // LANES
TILE_PITCH = SUBLANES + 1


def _block_diag(ref, rows_per_group, cols_per_group):
    n_rows = ref.shape[1]
    n_cols = GROUPS_PER_SLAB * cols_per_group
    row_group = lax.broadcasted_iota(jnp.int32, (n_rows, n_cols), 0) // rows_per_group
    col_group = lax.broadcasted_iota(jnp.int32, (n_rows, n_cols), 1) // cols_per_group
    return jnp.where((row_group == col_group)[None], jnp.concatenate([ref[...]] * GROUPS_PER_SLAB, axis=-1), 0.0)


def _discretize_kernel(lam_re_ref, lam_im_ref, log_dt_ref, b_re_ref, b_im_ref, c_re_ref, c_im_ref,
                       a_re_ref, a_im_ref, wb_re_ref, wb_im_ref, wc_re_ref, wc_im_ref):
    lam_re = lam_re_ref[...]
    lam_im = lam_im_ref[...]
    dt = jnp.exp(log_dt_ref[...])
    mag = jnp.exp(lam_re * dt)
    a_re = mag * jnp.cos(lam_im * dt)
    a_im = mag * jnp.sin(lam_im * dt)
    a_re_ref[...] = a_re
    a_im_ref[...] = a_im
    num_re = a_re - 1.0
    den = lam_re * lam_re + lam_im * lam_im
    c_re = ((num_re * lam_re + a_im * lam_im) / den)[:, None, :]
    c_im = ((a_im * lam_re - num_re * lam_im) / den)[:, None, :]
    b_re = _block_diag(b_re_ref, SSM_GROUP, SSM_STATE)
    b_im = _block_diag(b_im_ref, SSM_GROUP, SSM_STATE)
    wb_re_ref[...] = (c_re * b_re - c_im * b_im).astype(wb_re_ref.dtype)
    wb_im_ref[...] = (c_re * b_im + c_im * b_re).astype(wb_im_ref.dtype)
    wc_re_ref[...] = _block_diag(c_re_ref, SSM_STATE, SSM_GROUP).astype(wc_re_ref.dtype)
    wc_im_ref[...] = _block_diag(c_im_ref, SSM_STATE, SSM_GROUP).astype(wc_im_ref.dtype)


def _discretize(lam_re, lam_im, log_dt, b_re, b_im, c_re, c_im):
    shape4 = (N_SLAB, SLAB_STATES)
    log_dt_full = jnp.broadcast_to(log_dt[:, None], (N_SSM_GROUPS, SSM_STATE)).reshape(shape4)
    b_rows = lambda b: jnp.swapaxes(b, 1, 2).reshape(N_SLAB, SLAB_IN, SSM_STATE)
    c_rows = lambda c: jnp.swapaxes(c, 1, 2).reshape(N_SLAB, SLAB_STATES, SSM_GROUP)
    out_shape = (jax.ShapeDtypeStruct(shape4, F32), jax.ShapeDtypeStruct(shape4, F32),
                 jax.ShapeDtypeStruct((N_SLAB, SLAB_IN, SLAB_STATES), BF16),
                 jax.ShapeDtypeStruct((N_SLAB, SLAB_IN, SLAB_STATES), BF16),
                 jax.ShapeDtypeStruct((N_SLAB, SLAB_STATES, SLAB_IN), BF16),
                 jax.ShapeDtypeStruct((N_SLAB, SLAB_STATES, SLAB_IN), BF16))
    return pl.pallas_call(
        _discretize_kernel, out_shape=out_shape, name="s5_discretize",
        compiler_params=pltpu.CompilerParams(vmem_limit_bytes=_vmem_limit(32 << 20)),
    )(lam_re.reshape(shape4), lam_im.reshape(shape4), log_dt_full,
      b_rows(b_re), b_rows(b_im), c_rows(c_re), c_rows(c_im))


def _s5_kernel(u_ref, gate_ref, a_re_ref, a_im_ref, wb_re_ref, wb_im_ref, wc_re_ref, wc_im_ref,
               dskip_ref, wglu_ref, bglu_ref, g_ref, o_ref,
               s_re_ref, s_im_ref, x_re_ref, x_im_ref, *, t_blk, n_batch):
    n_row = n_batch * N_SLAB
    assert n_row == SUBLANES

    @pl.when(pl.program_id(0) == 0)
    def _():
        x_re_ref[...] = jnp.zeros_like(x_re_ref)
        x_im_ref[...] = jnp.zeros_like(x_im_ref)

    for j in range(N_SLAB):
        cols = slice(j * SLAB_IN, (j + 1) * SLAB_IN)
        uj = jnp.concatenate([u_ref[b, :, cols] for b in range(n_batch)], axis=0)
        bu_re = jnp.dot(uj, wb_re_ref[j], preferred_element_type=F32)
        bu_im = jnp.dot(uj, wb_im_ref[j], preferred_element_type=F32)
        for b in range(n_batch):
            rows = slice(b * t_blk, (b + 1) * t_blk)
            k = b * N_SLAB + j
            for c in range(N_CHUNK):
                lanes = slice(c * LANES, (c + 1) * LANES)
                s_re_ref[c, pl.ds(k, t_blk, stride=TILE_PITCH), :] = bu_re[rows, lanes]
                s_im_ref[c, pl.ds(k, t_blk, stride=TILE_PITCH), :] = bu_im[rows, lanes]

    a_re = [a_re_ref[:, c * LANES:(c + 1) * LANES] for c in range(N_CHUNK)]
    a_im = [a_im_ref[:, c * LANES:(c + 1) * LANES] for c in range(N_CHUNK)]
    x_re0 = tuple(x_re_ref[:, c * LANES:(c + 1) * LANES] for c in range(N_CHUNK))
    x_im0 = tuple(x_im_ref[:, c * LANES:(c + 1) * LANES] for c in range(N_CHUNK))

    def step(t, state):
        x_re, x_im = state
        rows = pl.ds(t * TILE_PITCH, n_row)
        new_re, new_im = [], []
        for c in range(N_CHUNK):
            nr = a_re[c] * x_re[c] - a_im[c] * x_im[c] + s_re_ref[c, rows, :]
            ni = a_re[c] * x_im[c] + a_im[c] * x_re[c] + s_im_ref[c, rows, :]
            s_re_ref[c, rows, :] = nr
            s_im_ref[c, rows, :] = ni
            new_re.append(nr)
            new_im.append(ni)
        return tuple(new_re), tuple(new_im)

    x_re, x_im = lax.fori_loop(0, t_blk, step, (x_re0, x_im0), unroll=4)
    for c in range(N_CHUNK):
        x_re_ref[:, c * LANES:(c + 1) * LANES] = x_re[c]
        x_im_ref[:, c * LANES:(c + 1) * LANES] = x_im[c]

    def states(s_ref, j):
        return jnp.concatenate(
            [jnp.concatenate([s_ref[c, pl.ds(b * N_SLAB + j, t_blk, stride=TILE_PITCH), :] for c in range(N_CHUNK)],
                             axis=1) for b in range(n_batch)], axis=0).astype(BF16)

    ys = [jnp.dot(states(s_re_ref, j), wc_re_ref[j], preferred_element_type=F32)
          - jnp.dot(states(s_im_ref, j), wc_im_ref[j], preferred_element_type=F32) for j in range(N_SLAB)]
    u_all = jnp.concatenate([u_ref[b] for b in range(n_batch)], axis=0).astype(F32)
    y = jnp.concatenate(ys, axis=1) + dskip_ref[...] * u_all
    h = jax.nn.gelu(y)
    glu = jnp.dot(h.astype(BF16), wglu_ref[...], preferred_element_type=F32) + bglu_ref[...]
    out = glu[:, :D_SSM] * jax.nn.sigmoid(glu[:, D_SSM:])
    out = out * lax.rsqrt(jnp.mean(out * out, axis=-1, keepdims=True) + EPS) * g_ref[...]
    for b in range(n_batch):
        rows = slice(b * t_blk, (b + 1) * t_blk)
        o_ref[b] = (out[rows] * _silu(gate_ref[b].astype(F32))).astype(o_ref.dtype)


def _s5_branch(proj3, a_re, a_im, wb_re, wb_im, wc_re, wc_im, d_skip, w_glu, b_glu, g_ssm, t_blk=256):
    b, l, _ = proj3.shape
    n_row = b * N_SLAB
    a_re_t = jnp.tile(a_re, (b, 1))
    a_im_t = jnp.tile(a_im, (b, 1))
    kern = functools.partial(_s5_kernel, t_blk=t_blk, n_batch=b)
    weights = 4 * N_SLAB * SLAB_IN * SLAB_STATES * 2 + D_SSM * 2 * D_SSM * 2
    est = (weights + 2 * N_CHUNK * t_blk * TILE_PITCH * LANES * 4 + 2 * 3 * b * t_blk * D_SSM * 2
           + 6 * b * t_blk * 2 * D_SSM * 4)
    return pl.pallas_call(
        kern,
        grid=(l // t_blk,),
        in_specs=[
            pl.BlockSpec((b, t_blk, D_SSM), lambda i: (0, i, COL_SU_1024)),
            pl.BlockSpec((b, t_blk, D_SSM), lambda i: (0, i, COL_SGATE_1024)),
            _resident(n_row, SLAB_STATES), _resident(n_row, SLAB_STATES),
            _resident(N_SLAB, SLAB_IN, SLAB_STATES), _resident(N_SLAB, SLAB_IN, SLAB_STATES),
            _resident(N_SLAB, SLAB_STATES, SLAB_IN), _resident(N_SLAB, SLAB_STATES, SLAB_IN),
            _resident(1, D_SSM), _resident(D_SSM, 2 * D_SSM), _resident(1, 2 * D_SSM), _resident(1, D_SSM),
        ],
        out_specs=pl.BlockSpec((b, t_blk, D_SSM), lambda i: (0, i, 0)),
        out_shape=jax.ShapeDtypeStruct((b, l, D_SSM), BF16),
        scratch_shapes=[pltpu.VMEM((N_CHUNK, t_blk * TILE_PITCH, LANES), F32),
                        pltpu.VMEM((N_CHUNK, t_blk * TILE_PITCH, LANES), F32),
                        pltpu.VMEM((n_row, SLAB_STATES), F32),
                        pltpu.VMEM((n_row, SLAB_STATES), F32)],
        compiler_params=pltpu.CompilerParams(
            dimension_semantics=("arbitrary",),
            vmem_limit_bytes=_vmem_limit(est)),
        name="s5_branch",
    )(proj3, proj3, a_re_t, a_im_t, wb_re, wb_im, wc_re, wc_im,
      d_skip.reshape(1, D_SSM), w_glu, b_glu.reshape(1, 2 * D_SSM), g_ssm.reshape(1, D_SSM))


def _mixers(proj, bsz, w_pool, pool_scale, lam_re, lam_im, log_dt,
            b_re, b_im, c_re, c_im, d_skip, w_glu, b_glu, branch_g):
    m = proj.shape[0]
    proj3 = proj.reshape(bsz, m // bsz, D_IN)
    g_pool = branch_g[:D_POOL]
    g_attn = branch_g[D_POOL:D_POOL + D_ATTN]
    g_ssm = branch_g[D_POOL + D_ATTN:]
    y_pool = _pool_branch(proj3, w_pool.astype(BF16), pool_scale, g_pool)
    y_attn = _attn_branch(proj3, g_attn)
    a_re, a_im, wb_re, wb_im, wc_re, wc_im = _discretize(lam_re, lam_im, log_dt, b_re, b_im, c_re, c_im)
    y_ssm = _s5_branch(proj3, a_re, a_im, wb_re, wb_im, wc_re, wc_im, d_skip, w_glu.astype(BF16), b_glu, g_ssm)
    return y_pool.reshape(m, D_POOL), y_attn.reshape(m, D_ATTN), y_ssm.reshape(m, D_SSM)


def kernel(x, ln_g, w_in, w_pool, pool_scale, lam_re, lam_im, log_dt, b_re, b_im, c_re, c_im,
           d_skip, w_glu, b_glu, branch_g, w_out, final_g):
    bsz, l, d = x.shape
    assert ln_g.shape[0] == 2
    mixer_params = (w_pool, pool_scale, lam_re, lam_im, log_dt, b_re, b_im, c_re, c_im,
                    d_skip, w_glu, b_glu, branch_g)
    x0 = x.reshape(bsz * l, d)

    h = _rmsnorm(x0, ln_g[0], BF16)
    proj, w_in1 = _in_proj_f32w(h, w_in, 0)
    ys = _mixers(proj, bsz, *[p[0] for p in mixer_params])
    x1, xg, ssq = _out_proj_f32w(*ys, w_out, 0, x0, ln_g[1])

    proj, w_out1 = _in_proj_scaled(xg, ssq, w_in1, w_out, 1)
    ys = _mixers(proj, bsz, *[p[1] for p in mixer_params])
    return _out_proj_final(*ys, w_out1, x1, final_g).reshape(bsz, l, d)
```

```python
import functools
import math

import jax
import jax.numpy as jnp
from jax import lax
from jax.experimental import pallas as pl
from jax.experimental.pallas import tpu as pltpu

D_MODEL = 4096
D_POOL = 1024
D_ATTN = 2048
D_SSM = 1024
POOL_WINDOWS = (2, 4, 8, 16)
POOL_GROUP = 256
POOL_SUB = 256
HEAD_DIM = 128
N_HEADS = 16
SSM_GROUP = 16
SSM_STATE = 64
N_SSM_GROUPS = 64
D_IN = 12288
EPS = 1e-6

COL_PX_1024 = 0
COL_PGATE_1024 = 1
COL_Q_2048 = 1
COL_K_2048 = 2
COL_V_2048 = 3
COL_AGATE_2048 = 4
COL_SU_1024 = 10
COL_SGATE_1024 = 11

V7X_VMEM_BYTES = 64 * 1024 * 1024
VMEM_RESERVE_BYTES = 6 * 1024 * 1024
SUBLANES = 8
LANES = 128

LOG2_E = 1.0 / math.log(2.0)
F32_EXP2_ZERO_ABOVE = 150.0
NO_BLOCK_CARRY = 1e30
MASKED_LOGIT = -1e30

BF16 = jnp.bfloat16
F32 = jnp.float32


def _vmem_limit(nbytes):
    return int(min(nbytes * 3 // 2 + (4 << 20), V7X_VMEM_BYTES - VMEM_RESERVE_BYTES))


def _silu(x):
    return x * jax.nn.sigmoid(x)


def _resident(*shape):
    return pl.BlockSpec(shape, lambda *_: (0,) * len(shape), pipeline_mode=pl.Buffered(1))


def _rmsnorm_kernel(x_ref, g_ref, o_ref):
    x = x_ref[...]
    y = x * lax.rsqrt(jnp.mean(x * x, axis=-1, keepdims=True) + EPS)
    o_ref[...] = (y * g_ref[...]).astype(o_ref.dtype)


def _rmsnorm(x2d, g, out_dtype, tm=512):
    m, d = x2d.shape
    return pl.pallas_call(
        _rmsnorm_kernel,
        grid=(m // tm,),
        in_specs=[pl.BlockSpec((tm, d), lambda i: (i, 0)),
                  pl.BlockSpec((1, d), lambda i: (0, 0))],
        out_specs=pl.BlockSpec((tm, d), lambda i: (i, 0)),
        out_shape=jax.ShapeDtypeStruct((m, d), out_dtype),
        compiler_params=pltpu.CompilerParams(
            dimension_semantics=("parallel",),
            vmem_limit_bytes=_vmem_limit(2 * tm * d * 8)),
        name="rmsnorm",
    )(x2d, g.reshape(1, d))


W_CHUNKS = 4


def _in_proj_f32w_kernel(a_ref, w_ref, wnext_ref, o_ref, wnext_bf_ref, wbf_ref, *, n_col, steps_per_chunk):
    c = pl.program_id(0)
    i = pl.program_id(1)
    chunk_rows = w_ref.shape[0]

    @pl.when(jnp.logical_and(c < n_col, i % steps_per_chunk == 0))
    def _():
        rows = pl.ds(pl.multiple_of((i // steps_per_chunk) * chunk_rows, chunk_rows), chunk_rows)
        wbf_ref[c % 2, rows, :] = w_ref[...].astype(BF16)

    @pl.when(c > 0)
    def _():
        o_ref[...] = jnp.dot(a_ref[...], wbf_ref[(c + 1) % 2], preferred_element_type=F32).astype(o_ref.dtype)
        wnext_bf_ref[...] = wnext_ref[...].astype(BF16)


def _in_proj_f32w(h, w_all, layer, tm=1024, tn=1024):
    m, k = h.shape
    n = w_all.shape[2]
    n_col, n_row = n // tn, m // tm
    steps_per_chunk = n_row // W_CHUNKS
    chunk_rows = k // W_CHUNKS
    side_rows = k // n_row
    col = lambda c: jnp.maximum(c - 1, 0)
    row = lambda c, i: jnp.where(c == 0, 0, i)
    est = (2 * (tm * k * 2 + chunk_rows * tn * 4 + tm * tn * 2 + side_rows * tn * 6) + 2 * k * tn * 2)
    kern = functools.partial(_in_proj_f32w_kernel, n_col=n_col, steps_per_chunk=steps_per_chunk)
    return pl.pallas_call(
        kern,
        grid=(n_col + 1, n_row),
        in_specs=[pl.BlockSpec((tm, k), lambda c, i: (row(c, i), 0)),
                  pl.BlockSpec((None, chunk_rows, tn),
                               lambda c, i: (layer, i // steps_per_chunk, jnp.minimum(c, n_col - 1))),
                  pl.BlockSpec((None, side_rows, tn), lambda c, i: (layer + 1, row(c, i), col(c)))],
        out_specs=(pl.BlockSpec((tm, tn), lambda c, i: (row(c, i), col(c))),
                   pl.BlockSpec((side_rows, tn), lambda c, i: (row(c, i), col(c)))),
        out_shape=(jax.ShapeDtypeStruct((m, n), BF16), jax.ShapeDtypeStruct((k, n), BF16)),
        scratch_shapes=[pltpu.VMEM((2, k, tn), BF16)],
        compiler_params=pltpu.CompilerParams(
            dimension_semantics=("arbitrary", "arbitrary"),
            vmem_limit_bytes=_vmem_limit(est)),
        name="in_proj_f32w",
    )(h, w_all, w_all)


def _in_proj_bf16w_kernel(a_ref, w_ref, wout_ref, o_ref, wout_bf_ref, *, n_cast_cols):
    o_ref[...] = jnp.dot(a_ref[...], w_ref[...], preferred_element_type=F32).astype(o_ref.dtype)

    @pl.when(pl.program_id(1) < n_cast_cols)
    def _():
        wout_bf_ref[...] = wout_ref[...].astype(BF16)


def _in_proj_bf16w(h, w, w_out_all, layer, tm=1024, tn=1024, cast_cols=512):
    m, k = h.shape
    n = w.shape[1]
    n_row, n_col = m // tm, n // tn
    ko, no = w_out_all.shape[1:]
    cast_rows = ko // n_row
    n_cast_cols = no // cast_cols
    assert n_cast_cols <= n_col
    cast_col = lambda j: jnp.minimum(j, n_cast_cols - 1)
    est = 2 * (tm * k * 2 + k * tn * 2 + tm * tn * 2 + cast_rows * cast_cols * 6)
    return pl.pallas_call(
        functools.partial(_in_proj_bf16w_kernel, n_cast_cols=n_cast_cols),
        grid=(n_row, n_col),
        in_specs=[pl.BlockSpec((tm, k), lambda i, j: (i, 0)),
                  pl.BlockSpec((k, tn), lambda i, j: (0, j)),
                  pl.BlockSpec((None, cast_rows, cast_cols), lambda i, j: (layer, i, cast_col(j)))],
        out_specs=(pl.BlockSpec((tm, tn), lambda i, j: (i, j)),
                   pl.BlockSpec((cast_rows, cast_cols), lambda i, j: (i, cast_col(j)))),
        out_shape=(jax.ShapeDtypeStruct((m, n), BF16), jax.ShapeDtypeStruct((ko, no), BF16)),
        compiler_params=pltpu.CompilerParams(
            dimension_semantics=("parallel", "arbitrary"),
            vmem_limit_bytes=_vmem_limit(est)),
        name="in_proj_bf16w",
    )(h, w, w_out_all)


def _branch_matmul(yp_ref, ya_ref, ys_ref, w_ref):
    acc = jnp.dot(yp_ref[...], w_ref[0:D_POOL, :], preferred_element_type=F32)
    acc += jnp.dot(ya_ref[...], w_ref[D_POOL:D_POOL + D_ATTN, :], preferred_element_type=F32)
    return acc + jnp.dot(ys_ref[...], w_ref[D_POOL + D_ATTN:, :], preferred_element_type=F32)


def _out_proj_mid_kernel(yp_ref, ya_ref, ys_ref, w_ref, x_ref, g_ref, xo_ref, h_ref):
    x_new = x_ref[...] + _branch_matmul(yp_ref, ya_ref, ys_ref, w_ref)
    xo_ref[...] = x_new
    y = x_new * lax.rsqrt(jnp.mean(x_new * x_new, axis=-1, keepdims=True) + EPS)
    h_ref[...] = (y * g_ref[...]).astype(h_ref.dtype)


def _out_proj_mid(y_pool, y_attn, y_ssm, w, x2d, next_g, tm=128):
    m = x2d.shape[0]
    k, n = w.shape
    est = k * n * 2 + 2 * (tm * k * 2 + 2 * tm * n * 4 + tm * n * 2) + tm * n * 4
    rows = pl.BlockSpec((tm, n), lambda i: (i, 0))
    return pl.pallas_call(
        _out_proj_mid_kernel,
        grid=(m // tm,),
        in_specs=[pl.BlockSpec((tm, D_POOL), lambda i: (i, 0)),
                  pl.BlockSpec((tm, D_ATTN), lambda i: (i, 0)),
                  pl.BlockSpec((tm, D_SSM), lambda i: (i, 0)),
                  _resident(k, n), rows, _resident(1, n)],
        out_specs=(rows, rows),
        out_shape=(jax.ShapeDtypeStruct((m, n), F32), jax.ShapeDtypeStruct((m, n), BF16)),
        compiler_params=pltpu.CompilerParams(
            dimension_semantics=("parallel",),
            vmem_limit_bytes=_vmem_limit(est)),
        name="out_proj_mid",
    )(y_pool, y_attn, y_ssm, w, x2d, next_g.reshape(1, n))


def _out_proj_final_kernel(yp_ref, ya_ref, ys_ref, w_ref, x_ref, g_ref, o_ref):
    x_new = x_ref[...] + _branch_matmul(yp_ref, ya_ref, ys_ref, w_ref)
    y = x_new * lax.rsqrt(jnp.mean(x_new * x_new, axis=-1, keepdims=True) + EPS)
    o_ref[...] = (y * g_ref[...]).astype(o_ref.dtype)


def _out_proj_final(y_pool, y_attn, y_ssm, w, x2d, g, tm=256):
    m = x2d.shape[0]
    k, n = w.shape
    est = k * n * 2 + 2 * (tm * k * 2 + 2 * tm * n * 4) + tm * n * 4
    return pl.pallas_call(
        _out_proj_final_kernel,
        grid=(m // tm,),
        in_specs=[pl.BlockSpec((tm, D_POOL), lambda i: (i, 0)),
                  pl.BlockSpec((tm, D_ATTN), lambda i: (i, 0)),
                  pl.BlockSpec((tm, D_SSM), lambda i: (i, 0)),
                  _resident(k, n),
                  pl.BlockSpec((tm, n), lambda i: (i, 0)),
                  _resident(1, n)],
        out_specs=pl.BlockSpec((tm, n), lambda i: (i, 0)),
        out_shape=jax.ShapeDtypeStruct((m, n), F32),
        compiler_params=pltpu.CompilerParams(
            dimension_semantics=("parallel",),
            vmem_limit_bytes=_vmem_limit(est)),
        name="out_proj_final",
    )(y_pool, y_attn, y_ssm, w, x2d, g.reshape(1, n))


def _pool_kernel(px_ref, prev_ref, gate_ref, w_ref, scale_ref, g_ref, *rest, t_blk, side_cast):
    if side_cast:
        wside_ref, o_ref, wside_bf_ref = rest
        wside_bf_ref[...] = wside_ref[...].astype(BF16)
    else:
        (o_ref,) = rest
    i = pl.program_id(1)
    x_bf = px_ref[0]
    x = x_bf.astype(F32)
    x_ext = jnp.concatenate([jnp.where(i > 0, prev_ref[0], jnp.zeros_like(prev_ref[0])), x_bf], axis=0)
    pos = (i * t_blk + 1 + lax.broadcasted_iota(jnp.int32, (t_blk, 1), 0)).astype(F32)
    out_row = lax.broadcasted_iota(jnp.int32, (POOL_SUB, 2 * POOL_SUB), 0)
    in_row = lax.broadcasted_iota(jnp.int32, (POOL_SUB, 2 * POOL_SUB), 1)
    lag = out_row + POOL_SUB - in_row
    mixed = []
    for g, w in enumerate(POOL_WINDOWS):
        cols = slice(g * POOL_GROUP, (g + 1) * POOL_GROUP)
        band = jnp.where(lag >= 0, jnp.where(lag < w, 1.0, 0.0), 0.0).astype(BF16)
        s = jnp.concatenate(
            [jnp.dot(band, x_ext[r:r + 2 * POOL_SUB, cols], preferred_element_type=F32)
             for r in range(0, t_blk, POOL_SUB)], axis=0)
        pooled = s / jnp.minimum(pos, float(w)) - x[:, cols]
        mixed.append(jnp.dot(pooled.astype(BF16), w_ref[g], preferred_element_type=F32))
    y = jnp.concatenate(mixed, axis=1) * scale_ref[...]
    y = y * lax.rsqrt(jnp.mean(y * y, axis=-1, keepdims=True) + EPS) * g_ref[...]
    o_ref[0] = (y * _silu(gate_ref[0].astype(F32))).astype(o_ref.dtype)


def _pool_branch(proj3, w_pool, pool_scale, g_pool, side_w=None, t_blk=1024):
    b, l, _ = proj3.shape
    assert max(POOL_WINDOWS) <= POOL_SUB and t_blk % POOL_SUB == 0
    sub_per_blk = t_blk // POOL_SUB
    n_t = l // t_blk
    kern = functools.partial(_pool_kernel, t_blk=t_blk, side_cast=side_w is not None)
    in_specs = [
        pl.BlockSpec((1, t_blk, D_POOL), lambda bi, i: (bi, i, COL_PX_1024)),
        pl.BlockSpec((1, POOL_SUB, D_POOL),
                     lambda bi, i: (bi, jnp.maximum(i * sub_per_blk - 1, 0), COL_PX_1024)),
        pl.BlockSpec((1, t_blk, D_POOL), lambda bi, i: (bi, i, COL_PGATE_1024)),
        pl.BlockSpec((len(POOL_WINDOWS), POOL_GROUP, POOL_GROUP), lambda bi, i: (0, 0, 0)),
        pl.BlockSpec((1, D_POOL), lambda bi, i: (0, 0)),
        pl.BlockSpec((1, D_POOL), lambda bi, i: (0, 0)),
    ]
    out_specs = pl.BlockSpec((1, t_blk, D_POOL), lambda bi, i: (bi, i, 0))
    out_shape = jax.ShapeDtypeStruct((b, l, D_POOL), BF16)
    operands = [proj3, proj3, proj3, w_pool, pool_scale.reshape(1, D_POOL), g_pool.reshape(1, D_POOL)]
    est = 8 * t_blk * D_POOL * 4
    if side_w is not None:
        w_all, layer = side_w
        ko, no = w_all.shape[1:]
        side_rows = ko // (b * n_t)
        in_specs.append(pl.BlockSpec((None, side_rows, no), lambda bi, i: (layer, bi * n_t + i, 0)))
        out_specs = (out_specs, pl.BlockSpec((side_rows, no), lambda bi, i: (bi * n_t + i, 0)))
        out_shape = (out_shape, jax.ShapeDtypeStruct((ko, no), BF16))
        operands.append(w_all)
        est += 2 * side_rows * no * 6
    return pl.pallas_call(
        kern,
        grid=(b, n_t),
        in_specs=in_specs,
        out_specs=out_specs,
        out_shape=out_shape,
        compiler_params=pltpu.CompilerParams(
            dimension_semantics=("parallel", "parallel"),
            vmem_limit_bytes=_vmem_limit(est)),
        name="pool_branch",
    )(*operands)


def _attn_kernel(q_ref, proj_hbm_ref, gate_ref, g_ref, o_ref, k_ref, v_ref, kv_sem, *, t_blk, n_before):
    n_head = N_HEADS
    bi = pl.program_id(0)
    i = pl.program_id(1)

    def kv_copy(which, blk):
        col0 = (COL_K_2048, COL_V_2048)[which] * D_ATTN
        rows = pl.ds(pl.multiple_of(blk * t_blk, t_blk), t_blk)
        return pltpu.make_async_copy(proj_hbm_ref.at[bi, rows, pl.ds(col0, D_ATTN)],
                                     (k_ref, v_ref)[which].at[rows, :], kv_sem.at[which, blk])

    @pl.when(i == 0)
    def _():
        for blk in range(k_ref.shape[0] // t_blk):
            kv_copy(0, blk).start()
            kv_copy(1, blk).start()

    kv_copy(0, i).wait()
    kv_copy(1, i).wait()

    row = lax.broadcasted_iota(jnp.int32, (t_blk, t_blk), 0)
    col = lax.broadcasted_iota(jnp.int32, (t_blk, t_blk), 1)
    later = jnp.where(row > col, 1.0, 0.0).astype(BF16)
    causal = col < row
    qs = [(q_ref[0, :, h * HEAD_DIM:(h + 1) * HEAD_DIM].astype(F32) * (HEAD_DIM ** -0.5 * LOG2_E)).astype(BF16)
          for h in range(n_head)]

    def tile(h, j, carry, diagonal):
        lanes = slice(h * HEAD_DIM, (h + 1) * HEAD_DIM)
        start = pl.multiple_of(j * t_blk, t_blk)
        kj = k_ref[pl.ds(start, t_blk), lanes]
        vj = v_ref[pl.ds(start, t_blk), lanes]
        z = lax.dot_general(qs[h], kj, (((1,), (1,)), ((), ())), preferred_element_type=F32)
        if diagonal:
            z = jnp.where(causal, z, MASKED_LOGIT)
        softplus = jnp.maximum(z, 0.0) + jnp.log(1.0 + jnp.exp2(-jnp.abs(z))) * LOG2_E
        suffix = jnp.dot(softplus.astype(BF16), later, preferred_element_type=F32)
        wts = jnp.exp2(z - softplus - suffix - carry)
        pv = jnp.dot(wts.astype(BF16), vj, preferred_element_type=F32)
        return pv, carry + jnp.sum(softplus, axis=1, keepdims=True)

    accs, carries = [], []
    for h in range(n_head):
        acc, carry = tile(h, i, jnp.zeros((t_blk, 1), F32), True)
        for d in range(1, n_before + 1):
            carry = jnp.where(i >= d, carry, NO_BLOCK_CARRY)
            pv, carry = tile(h, jnp.maximum(i - d, 0), carry, False)
            acc = acc + pv
        accs.append(acc)
        carries.append(carry)

    def alive_of(carries):
        return jnp.min(functools.reduce(jnp.minimum, carries)) < F32_EXP2_ZERO_ABOVE

    def cond(state):
        j, _, _, alive = state
        return jnp.logical_and(j >= 0, alive)

    def body(state):
        j, accs, carries, _ = state
        new_accs, new_carries = [], []
        for h in range(n_head):
            pv, carry = tile(h, j, carries[h], False)
            new_accs.append(accs[h] + pv)
            new_carries.append(carry)
        return j - 1, tuple(new_accs), tuple(new_carries), alive_of(new_carries)

    def finish(accs):
        y = jnp.concatenate(accs, axis=1)
        y = y * lax.rsqrt(jnp.mean(y * y, axis=-1, keepdims=True) + EPS) * g_ref[...]
        o_ref[0] = (y * _silu(gate_ref[0].astype(F32))).astype(o_ref.dtype)

    finish(accs)
    first = i - n_before - 1

    @pl.when(jnp.logical_and(first >= 0, alive_of(carries)))
    def _():
        _, more, _, _ = lax.while_loop(cond, body, (first, tuple(accs), tuple(carries), True))
        finish(more)


def _attn_branch(proj3, g_attn, t_blk=256, n_before=1):
    b, l, _ = proj3.shape
    kern = functools.partial(_attn_kernel, t_blk=t_blk, n_before=n_before)
    est = 2 * l * D_ATTN * 2 + 2 * 3 * t_blk * D_ATTN * 2 + N_HEADS * 12 * t_blk * t_blk * 4
    q_rows = lambda col: pl.BlockSpec((1, t_blk, D_ATTN), lambda bi, i: (bi, i, col))
    return pl.pallas_call(
        kern,
        grid=(b, l // t_blk),
        in_specs=[q_rows(COL_Q_2048), pl.BlockSpec(memory_space=pl.ANY), q_rows(COL_AGATE_2048),
                  pl.BlockSpec((1, D_ATTN), lambda bi, i: (0, 0))],
        out_specs=pl.BlockSpec((1, t_blk, D_ATTN), lambda bi, i: (bi, i, 0)),
        out_shape=jax.ShapeDtypeStruct((b, l, D_ATTN), BF16),
        scratch_shapes=[pltpu.VMEM((l, D_ATTN), BF16), pltpu.VMEM((l, D_ATTN), BF16),
                        pltpu.SemaphoreType.DMA((2, l // t_blk))],
        compiler_params=pltpu.CompilerParams(
            dimension_semantics=("arbitrary", "arbitrary"),
            vmem_limit_bytes=_vmem_limit(est)),
        name="attn_branch",
    )(proj3, proj3, proj3, g_attn.reshape(1, D_ATTN))


N_SLAB = 4
SLAB_IN = D_SSM // N_SLAB
SLAB_STATES = 1024
GROUPS_PER_SLAB = N_SSM_GROUPS // N_SLAB
N_CHUNK = SLAB_STATES // LANES
TILE_PITCH = SUBLANES + 1


def _block_diag(ref, rows_per_group, cols_per_group):
    n_rows = ref.shape[1]
    n_cols = GROUPS_PER_SLAB * cols_per_group
    row_group = lax.broadcasted_iota(jnp.int32, (n_rows, n_cols), 0) // rows_per_group
    col_group = lax.broadcasted_iota(jnp.int32, (n_rows, n_cols), 1) // cols_per_group
    return jnp.where((row_group == col_group)[None], jnp.concatenate([ref[...]] * GROUPS_PER_SLAB, axis=-1), 0.0)


def _discretize_kernel(lam_re_ref, lam_im_ref, log_dt_ref, b_re_ref, b_im_ref, c_re_ref, c_im_ref,
                       a_re_ref, a_im_ref, wb_re_ref, wb_im_ref, wc_re_ref, wc_im_ref):
    lam_re = lam_re_ref[...]
    lam_im = lam_im_ref[...]
    dt = jnp.exp(log_dt_ref[...])
    mag = jnp.exp(lam_re * dt)
    a_re = mag * jnp.cos(lam_im * dt)
    a_im = mag * jnp.sin(lam_im * dt)
    a_re_ref[...] = a_re
    a_im_ref[...] = a_im
    num_re = a_re - 1.0
    den = lam_re * lam_re + lam_im * lam_im
    c_re = ((num_re * lam_re + a_im * lam_im) / den)[:, None, :]
    c_im = ((a_im * lam_re - num_re * lam_im) / den)[:, None, :]
    b_re = _block_diag(b_re_ref, SSM_GROUP, SSM_STATE)
    b_im = _block_diag(b_im_ref, SSM_GROUP, SSM_STATE)
    wb_re_ref[...] = (c_re * b_re - c_im * b_im).astype(wb_re_ref.dtype)
    wb_im_ref[...] = (c_re * b_im + c_im * b_re).astype(wb_im_ref.dtype)
    wc_re_ref[...] = _block_diag(c_re_ref, SSM_STATE, SSM_GROUP).astype(wc_re_ref.dtype)
    wc_im_ref[...] = _block_diag(c_im_ref, SSM_STATE, SSM_GROUP).astype(wc_im_ref.dtype)


def _discretize(lam_re, lam_im, log_dt, b_re, b_im, c_re, c_im):
    shape4 = (N_SLAB, SLAB_STATES)
    log_dt_full = jnp.broadcast_to(log_dt[:, None], (N_SSM_GROUPS, SSM_STATE)).reshape(shape4)
    b_rows = lambda b: jnp.swapaxes(b, 1, 2).reshape(N_SLAB, SLAB_IN, SSM_STATE)
    c_rows = lambda c: jnp.swapaxes(c, 1, 2).reshape(N_SLAB, SLAB_STATES, SSM_GROUP)
    out_shape = (jax.ShapeDtypeStruct(shape4, F32), jax.ShapeDtypeStruct(shape4, F32),
                 jax.ShapeDtypeStruct((N_SLAB, SLAB_IN, SLAB_STATES), BF16),
                 jax.ShapeDtypeStruct((N_SLAB, SLAB_IN, SLAB_STATES), BF16),
                 jax.ShapeDtypeStruct((N_SLAB, SLAB_STATES, SLAB_IN), BF16),
                 jax.ShapeDtypeStruct((N_SLAB, SLAB_STATES, SLAB_IN), BF16))
    return pl.pallas_call(
        _discretize_kernel, out_shape=out_shape, name="s5_discretize",
        compiler_params=pltpu.CompilerParams(vmem_limit_bytes=_vmem_limit(32 << 20)),
    )(lam_re.reshape(shape4), lam_im.reshape(shape4), log_dt_full,
      b_rows(b_re), b_rows(b_im), c_rows(c_re), c_rows(c_im))


def _s5_kernel(u_ref, gate_ref, a_re_ref, a_im_ref, wb_re_ref, wb_im_ref, wc_re_ref, wc_im_ref,
               dskip_ref, wglu_ref, bglu_ref, g_ref, o_ref,
               s_re_ref, s_im_ref, x_re_ref, x_im_ref, *, t_blk, n_batch):
    n_row = n_batch * N_SLAB
    assert n_row == SUBLANES

    @pl.when(pl.program_id(0) == 0)
    def _():
        x_re_ref[...] = jnp.zeros_like(x_re_ref)
        x_im_ref[...] = jnp.zeros_like(x_im_ref)

    for j in range(N_SLAB):
        cols = slice(j * SLAB_IN, (j + 1) * SLAB_IN)
        uj = jnp.concatenate([u_ref[b, :, cols] for b in range(n_batch)], axis=0)
        bu_re = jnp.dot(uj, wb_re_ref[j], preferred_element_type=F32)
        bu_im = jnp.dot(uj, wb_im_ref[j], preferred_element_type=F32)
        for b in range(n_batch):
            rows = slice(b * t_blk, (b + 1) * t_blk)
            k = b * N_SLAB + j
            for c in range(N_CHUNK):
                lanes = slice(c * LANES, (c + 1) * LANES)
                s_re_ref[c, pl.ds(k, t_blk, stride=TILE_PITCH), :] = bu_re[rows, lanes]
                s_im_ref[c, pl.ds(k, t_blk, stride=TILE_PITCH), :] = bu_im[rows, lanes]

    a_re = [a_re_ref[:, c * LANES:(c + 1) * LANES] for c in range(N_CHUNK)]
    a_im = [a_im_ref[:, c * LANES:(c + 1) * LANES] for c in range(N_CHUNK)]
    x_re0 = tuple(x_re_ref[:, c * LANES:(c + 1) * LANES] for c in range(N_CHUNK))
    x_im0 = tuple(x_im_ref[:, c * LANES:(c + 1) * LANES] for c in range(N_CHUNK))

    def step(t, state):
        x_re, x_im = state
        rows = pl.ds(t * TILE_PITCH, n_row)
        new_re, new_im = [], []
        for c in range(N_CHUNK):
            nr = a_re[c] * x_re[c] - a_im[c] * x_im[c] + s_re_ref[c, rows, :]
            ni = a_re[c] * x_im[c] + a_im[c] * x_re[c] + s_im_ref[c, rows, :]
            s_re_ref[c, rows, :] = nr
            s_im_ref[c, rows, :] = ni
            new_re.append(nr)
            new_im.append(ni)
        return tuple(new_re), tuple(new_im)

    x_re, x_im = lax.fori_loop(0, t_blk, step, (x_re0, x_im0), unroll=4)
    for c in range(N_CHUNK):
        x_re_ref[:, c * LANES:(c + 1) * LANES] = x_re[c]
        x_im_ref[:, c * LANES:(c + 1) * LANES] = x_im[c]

    def states(s_ref, j):
        return jnp.concatenate(
            [jnp.concatenate([s_ref[c, pl.ds(b * N_SLAB + j, t_blk, stride=TILE_PITCH), :] for c in range(N_CHUNK)],
                             axis=1) for b in range(n_batch)], axis=0).astype(BF16)

    ys = [jnp.dot(states(s_re_ref, j), wc_re_ref[j], preferred_element_type=F32)
          - jnp.dot(states(s_im_ref, j), wc_im_ref[j], preferred_element_type=F32) for j in range(N_SLAB)]
    u_all = jnp.concatenate([u_ref[b] for b in range(n_batch)], axis=0).astype(F32)
    y = jnp.concatenate(ys, axis=1) + dskip_ref[...] * u_all
    h = jax.nn.gelu(y)
    glu = jnp.dot(h.astype(BF16), wglu_ref[...], preferred_element_type=F32) + bglu_ref[...]
    out = glu[:, :D_SSM] * jax.nn.sigmoid(glu[:, D_SSM:])
    out = out * lax.rsqrt(jnp.mean(out * out, axis=-1, keepdims=True) + EPS) * g_ref[...]
    for b in range(n_batch):
        rows = slice(b * t_blk, (b + 1) * t_blk)
        o_ref[b] = (out[rows] * _silu(gate_ref[b].astype(F32))).astype(o_ref.dtype)


def _s5_branch(proj3, a_re, a_im, wb_re, wb_im, wc_re, wc_im, d_skip, w_glu, b_glu, g_ssm, t_blk=256):
    b, l, _ = proj3.shape
    n_row = b * N_SLAB
    a_re_t = jnp.tile(a_re, (b, 1))
    a_im_t = jnp.tile(a_im, (b, 1))
    kern = functools.partial(_s5_kernel, t_blk=t_blk, n_batch=b)
    weights = 4 * N_SLAB * SLAB_IN * SLAB_STATES * 2 + D_SSM * 2 * D_SSM * 2
    est = (weights + 2 * N_CHUNK * t_blk * TILE_PITCH * LANES * 4 + 2 * 3 * b * t_blk * D_SSM * 2
           + 6 * b * t_blk * 2 * D_SSM * 4)
    return pl.pallas_call(
        kern,
        grid=(l // t_blk,),
        in_specs=[
            pl.BlockSpec((b, t_blk, D_SSM), lambda i: (0, i, COL_SU_1024)),
            pl.BlockSpec((b, t_blk, D_SSM), lambda i: (0, i, COL_SGATE_1024)),
            _resident(n_row, SLAB_STATES), _resident(n_row, SLAB_STATES),
            _resident(N_SLAB, SLAB_IN, SLAB_STATES), _resident(N_SLAB, SLAB_IN, SLAB_STATES),
            _resident(N_SLAB, SLAB_STATES, SLAB_IN), _resident(N_SLAB, SLAB_STATES, SLAB_IN),
            _resident(1, D_SSM), _resident(D_SSM, 2 * D_SSM), _resident(1, 2 * D_SSM), _resident(1, D_SSM),
        ],
        out_specs=pl.BlockSpec((b, t_blk, D_SSM), lambda i: (0, i, 0)),
        out_shape=jax.ShapeDtypeStruct((b, l, D_SSM), BF16),
        scratch_shapes=[pltpu.VMEM((N_CHUNK, t_blk * TILE_PITCH, LANES), F32),
                        pltpu.VMEM((N_CHUNK, t_blk * TILE_PITCH, LANES), F32),
                        pltpu.VMEM((n_row, SLAB_STATES), F32),
                        pltpu.VMEM((n_row, SLAB_STATES), F32)],
        compiler_params=pltpu.CompilerParams(
            dimension_semantics=("arbitrary",),
            vmem_limit_bytes=_vmem_limit(est)),
        name="s5_branch",
    )(proj3, proj3, a_re_t, a_im_t, wb_re, wb_im, wc_re, wc_im,
      d_skip.reshape(1, D_SSM), w_glu, b_glu.reshape(1, 2 * D_SSM), g_ssm.reshape(1, D_SSM))


def _mixers(proj, bsz, w_pool, pool_scale, lam_re, lam_im, log_dt,
            b_re, b_im, c_re, c_im, d_skip, w_glu, b_glu, branch_g, side_w=None):
    m = proj.shape[0]
    proj3 = proj.reshape(bsz, m // bsz, D_IN)
    g_pool = branch_g[:D_POOL]
    g_attn = branch_g[D_POOL:D_POOL + D_ATTN]
    g_ssm = branch_g[D_POOL + D_ATTN:]
    side = None
    if side_w is None:
        y_pool = _pool_branch(proj3, w_pool.astype(BF16), pool_scale, g_pool)
    else:
        y_pool, side = _pool_branch(proj3, w_pool.astype(BF16), pool_scale, g_pool, side_w, t_blk=512)
    y_attn = _attn_branch(proj3, g_attn)
    a_re, a_im, wb_re, wb_im, wc_re, wc_im = _discretize(lam_re, lam_im, log_dt, b_re, b_im, c_re, c_im)
    y_ssm = _s5_branch(proj3, a_re, a_im, wb_re, wb_im, wc_re, wc_im, d_skip, w_glu.astype(BF16), b_glu, g_ssm)
    return (y_pool.reshape(m, D_POOL), y_attn.reshape(m, D_ATTN), y_ssm.reshape(m, D_SSM)), side


def kernel(x, ln_g, w_in, w_pool, pool_scale, lam_re, lam_im, log_dt, b_re, b_im, c_re, c_im,
           d_skip, w_glu, b_glu, branch_g, w_out, final_g):
    bsz, l, d = x.shape
    assert ln_g.shape[0] == 2
    mixer_params = (w_pool, pool_scale, lam_re, lam_im, log_dt, b_re, b_im, c_re, c_im,
                    d_skip, w_glu, b_glu, branch_g)
    x0 = x.reshape(bsz * l, d)

    h = _rmsnorm(x0, ln_g[0], BF16)
    proj, w_in1 = _in_proj_f32w(h, w_in, 0)
    ys, w_out0 = _mixers(proj, bsz, *[p[0] for p in mixer_params], side_w=(w_out, 0))
    x1, h = _out_proj_mid(*ys, w_out0, x0, ln_g[1])

    proj, w_out1 = _in_proj_bf16w(h, w_in1, w_out, 1)
    ys, _ = _mixers(proj, bsz, *[p[1] for p in mixer_params])
    return _out_proj_final(*ys, w_out1, x1, final_g).reshape(bsz, l, d)
```

```python
import functools
import math

import jax
import jax.numpy as jnp
from jax import lax
from jax.experimental import pallas as pl
from jax.experimental.pallas import tpu as pltpu

D_MODEL = 4096
D_POOL = 1024
D_ATTN = 2048
D_SSM = 1024
POOL_WINDOWS = (2, 4, 8, 16)
POOL_GROUP = 256
POOL_SUB = 256
HEAD_DIM = 128
N_HEADS = 16
SSM_GROUP = 16
SSM_STATE = 64
N_SSM_GROUPS = 64
D_IN = 12288
EPS = 1e-6

COL_PX_1024 = 0
COL_PGATE_1024 = 1
COL_Q_2048 = 1
COL_K_2048 = 2
COL_V_2048 = 3
COL_AGATE_2048 = 4
COL_SU_1024 = 10
COL_SGATE_1024 = 11

V7X_VMEM_BYTES = 64 * 1024 * 1024
VMEM_RESERVE_BYTES = 6 * 1024 * 1024
SUBLANES = 8
LANES = 128

LOG2_E = 1.0 / math.log(2.0)
F32_EXP2_ZERO_ABOVE = 150.0
NO_BLOCK_CARRY = 1e30
MASKED_LOGIT = -1e30

BF16 = jnp.bfloat16
F32 = jnp.float32


def _vmem_limit(nbytes):
    return int(min(nbytes * 3 // 2 + (4 << 20), V7X_VMEM_BYTES - VMEM_RESERVE_BYTES))


def _silu(x):
    return x * jax.nn.sigmoid(x)


def _resident(*shape):
    return pl.BlockSpec(shape, lambda *_: (0,) * len(shape), pipeline_mode=pl.Buffered(1))


def _rmsnorm_kernel(x_ref, g_ref, o_ref):
    x = x_ref[...]
    y = x * lax.rsqrt(jnp.mean(x * x, axis=-1, keepdims=True) + EPS)
    o_ref[...] = (y * g_ref[...]).astype(o_ref.dtype)


def _rmsnorm(x2d, g, out_dtype, tm=512):
    m, d = x2d.shape
    return pl.pallas_call(
        _rmsnorm_kernel,
        grid=(m // tm,),
        in_specs=[pl.BlockSpec((tm, d), lambda i: (i, 0)),
                  pl.BlockSpec((1, d), lambda i: (0, 0))],
        out_specs=pl.BlockSpec((tm, d), lambda i: (i, 0)),
        out_shape=jax.ShapeDtypeStruct((m, d), out_dtype),
        compiler_params=pltpu.CompilerParams(
            dimension_semantics=("parallel",),
            vmem_limit_bytes=_vmem_limit(2 * tm * d * 8)),
        name="rmsnorm",
    )(x2d, g.reshape(1, d))


W_CHUNKS = 4


def _in_proj_f32w_kernel(a_ref, w_ref, wnext_ref, o_ref, wnext_bf_ref, wbf_ref, *, n_col, steps_per_chunk):
    c = pl.program_id(0)
    i = pl.program_id(1)
    chunk_rows = w_ref.shape[0]

    @pl.when(jnp.logical_and(c < n_col, i % steps_per_chunk == 0))
    def _():
        rows = pl.ds(pl.multiple_of((i // steps_per_chunk) * chunk_rows, chunk_rows), chunk_rows)
        wbf_ref[c % 2, rows, :] = w_ref[...].astype(BF16)

    @pl.when(c > 0)
    def _():
        o_ref[...] = jnp.dot(a_ref[...], wbf_ref[(c + 1) % 2], preferred_element_type=F32).astype(o_ref.dtype)
        wnext_bf_ref[...] = wnext_ref[...].astype(BF16)


def _in_proj_f32w(h, w_all, layer, tm=1024, tn=1024):
    m, k = h.shape
    n = w_all.shape[2]
    n_col, n_row = n // tn, m // tm
    steps_per_chunk = n_row // W_CHUNKS
    chunk_rows = k // W_CHUNKS
    side_rows = k // n_row
    col = lambda c: jnp.maximum(c - 1, 0)
    row = lambda c, i: jnp.where(c == 0, 0, i)
    est = (2 * (tm * k * 2 + chunk_rows * tn * 4 + tm * tn * 2 + side_rows * tn * 6) + 2 * k * tn * 2)
    kern = functools.partial(_in_proj_f32w_kernel, n_col=n_col, steps_per_chunk=steps_per_chunk)
    return pl.pallas_call(
        kern,
        grid=(n_col + 1, n_row),
        in_specs=[pl.BlockSpec((tm, k), lambda c, i: (row(c, i), 0)),
                  pl.BlockSpec((None, chunk_rows, tn),
                               lambda c, i: (layer, i // steps_per_chunk, jnp.minimum(c, n_col - 1))),
                  pl.BlockSpec((None, side_rows, tn), lambda c, i: (layer + 1, row(c, i), col(c)))],
        out_specs=(pl.BlockSpec((tm, tn), lambda c, i: (row(c, i), col(c))),
                   pl.BlockSpec((side_rows, tn), lambda c, i: (row(c, i), col(c)))),
        out_shape=(jax.ShapeDtypeStruct((m, n), BF16), jax.ShapeDtypeStruct((k, n), BF16)),
        scratch_shapes=[pltpu.VMEM((2, k, tn), BF16)],
        compiler_params=pltpu.CompilerParams(
            dimension_semantics=("arbitrary", "arbitrary"),
            vmem_limit_bytes=_vmem_limit(est)),
        name="in_proj_f32w",
    )(h, w_all, w_all)


def _in_proj_scaled_kernel(a_ref, ssq_ref, w_ref, wout_ref, o_ref, wout_bf_ref, r_ref, *, n_cast_cols):
    j = pl.program_id(1)

    @pl.when(j == 0)
    def _():
        ssq = jnp.sum(jnp.sum(ssq_ref[...], axis=0), axis=-1, keepdims=True)
        r_ref[...] = lax.rsqrt(ssq / D_MODEL + EPS)

    acc = jnp.dot(a_ref[...], w_ref[...], preferred_element_type=F32)
    o_ref[...] = (acc * r_ref[...]).astype(o_ref.dtype)

    @pl.when(j < n_cast_cols)
    def _():
        wout_bf_ref[...] = wout_ref[...].astype(BF16)


def _in_proj_scaled(xg, ssq, w, w_out_all, layer, tm=1024, tn=1024, cast_cols=512):
    m, k = xg.shape
    n = w.shape[1]
    parts = ssq.shape[0]
    n_row, n_col = m // tm, n // tn
    ko, no = w_out_all.shape[1:]
    cast_rows = ko // n_row
    n_cast_cols = no // cast_cols
    assert n_cast_cols <= n_col
    cast_col = lambda j: jnp.minimum(j, n_cast_cols - 1)
    est = (2 * (tm * k * 2 + k * tn * 2 + tm * tn * 2 + parts * tm * LANES * 4 + cast_rows * cast_cols * 6)
           + tm * LANES * 4)
    return pl.pallas_call(
        functools.partial(_in_proj_scaled_kernel, n_cast_cols=n_cast_cols),
        grid=(n_row, n_col),
        in_specs=[pl.BlockSpec((tm, k), lambda i, j: (i, 0)),
                  pl.BlockSpec((parts, tm, LANES), lambda i, j: (0, i, 0)),
                  pl.BlockSpec((k, tn), lambda i, j: (0, j)),
                  pl.BlockSpec((None, cast_rows, cast_cols), lambda i, j: (layer, i, cast_col(j)))],
        out_specs=(pl.BlockSpec((tm, tn), lambda i, j: (i, j)),
                   pl.BlockSpec((cast_rows, cast_cols), lambda i, j: (i, cast_col(j)))),
        out_shape=(jax.ShapeDtypeStruct((m, n), BF16), jax.ShapeDtypeStruct((ko, no), BF16)),
        scratch_shapes=[pltpu.VMEM((tm, 1), F32)],
        compiler_params=pltpu.CompilerParams(
            dimension_semantics=("parallel", "arbitrary"),
            vmem_limit_bytes=_vmem_limit(est)),
        name="in_proj_scaled",
    )(xg, ssq, w, w_out_all)


def _branch_matmul(yp_ref, ya_ref, ys_ref, w_ref):
    acc = jnp.dot(yp_ref[...], w_ref[0:D_POOL, :], preferred_element_type=F32)
    acc += jnp.dot(ya_ref[...], w_ref[D_POOL:D_POOL + D_ATTN, :], preferred_element_type=F32)
    return acc + jnp.dot(ys_ref[...], w_ref[D_POOL + D_ATTN:, :], preferred_element_type=F32)


def _out_proj_f32w_kernel(yp_ref, ya_ref, ys_ref, w_ref, x_ref, g_ref, xo_ref, xg_ref, ssq_ref, wbf_ref):
    @pl.when(pl.program_id(1) == 0)
    def _():
        wbf_ref[...] = w_ref[...].astype(BF16)

    x_new = x_ref[...] + _branch_matmul(yp_ref, ya_ref, ys_ref, wbf_ref)
    xo_ref[...] = x_new
    xg_ref[...] = (x_new * g_ref[...]).astype(xg_ref.dtype)
    sq = x_new * x_new
    ssq_ref[...] = functools.reduce(
        lambda a, b: a + b, [sq[:, c * LANES:(c + 1) * LANES] for c in range(sq.shape[1] // LANES)])


def _out_proj_f32w(y_pool, y_attn, y_ssm, w_all, layer, x2d, next_g, tm=512, tn=1024):
    m = x2d.shape[0]
    k, n = w_all.shape[1:]
    est = 2 * (tm * k * 2 + tm * tn * (4 + 4 + 2) + tm * LANES * 4) + k * tn * (4 + 2)
    tile = pl.BlockSpec((tm, tn), lambda j, i: (i, j))
    return pl.pallas_call(
        _out_proj_f32w_kernel,
        grid=(n // tn, m // tm),
        in_specs=[pl.BlockSpec((tm, D_POOL), lambda j, i: (i, 0)),
                  pl.BlockSpec((tm, D_ATTN), lambda j, i: (i, 0)),
                  pl.BlockSpec((tm, D_SSM), lambda j, i: (i, 0)),
                  pl.BlockSpec((None, k, tn), lambda j, i: (layer, 0, j), pipeline_mode=pl.Buffered(1)),
                  tile,
                  pl.BlockSpec((1, tn), lambda j, i: (0, j))],
        out_specs=(tile, tile, pl.BlockSpec((None, tm, LANES), lambda j, i: (j, i, 0))),
        out_shape=(jax.ShapeDtypeStruct((m, n), F32), jax.ShapeDtypeStruct((m, n), BF16),
                   jax.ShapeDtypeStruct((n // tn, m, LANES), F32)),
        scratch_shapes=[pltpu.VMEM((k, tn), BF16)],
        compiler_params=pltpu.CompilerParams(
            dimension_semantics=("parallel", "arbitrary"),
            vmem_limit_bytes=_vmem_limit(est)),
        name="out_proj_f32w",
    )(y_pool, y_attn, y_ssm, w_all, x2d, next_g.reshape(1, n))


def _out_proj_final_kernel(yp_ref, ya_ref, ys_ref, w_ref, x_ref, g_ref, o_ref):
    x_new = x_ref[...] + _branch_matmul(yp_ref, ya_ref, ys_ref, w_ref)
    y = x_new * lax.rsqrt(jnp.mean(x_new * x_new, axis=-1, keepdims=True) + EPS)
    o_ref[...] = (y * g_ref[...]).astype(o_ref.dtype)


def _out_proj_final(y_pool, y_attn, y_ssm, w, x2d, g, tm=256):
    m = x2d.shape[0]
    k, n = w.shape
    est = k * n * 2 + 2 * (tm * k * 2 + 2 * tm * n * 4) + tm * n * 4
    return pl.pallas_call(
        _out_proj_final_kernel,
        grid=(m // tm,),
        in_specs=[pl.BlockSpec((tm, D_POOL), lambda i: (i, 0)),
                  pl.BlockSpec((tm, D_ATTN), lambda i: (i, 0)),
                  pl.BlockSpec((tm, D_SSM), lambda i: (i, 0)),
                  _resident(k, n),
                  pl.BlockSpec((tm, n), lambda i: (i, 0)),
                  _resident(1, n)],
        out_specs=pl.BlockSpec((tm, n), lambda i: (i, 0)),
        out_shape=jax.ShapeDtypeStruct((m, n), F32),
        compiler_params=pltpu.CompilerParams(
            dimension_semantics=("parallel",),
            vmem_limit_bytes=_vmem_limit(est)),
        name="out_proj_final",
    )(y_pool, y_attn, y_ssm, w, x2d, g.reshape(1, n))


def _pool_kernel(px_ref, prev_ref, gate_ref, w_ref, scale_ref, g_ref, o_ref, *, t_blk):
    i = pl.program_id(1)
    x_bf = px_ref[0]
    x = x_bf.astype(F32)
    x_ext = jnp.concatenate([jnp.where(i > 0, prev_ref[0], jnp.zeros_like(prev_ref[0])), x_bf], axis=0)
    pos = (i * t_blk + 1 + lax.broadcasted_iota(jnp.int32, (t_blk, 1), 0)).astype(F32)
    out_row = lax.broadcasted_iota(jnp.int32, (POOL_SUB, 2 * POOL_SUB), 0)
    in_row = lax.broadcasted_iota(jnp.int32, (POOL_SUB, 2 * POOL_SUB), 1)
    lag = out_row + POOL_SUB - in_row
    mixed = []
    for g, w in enumerate(POOL_WINDOWS):
        cols = slice(g * POOL_GROUP, (g + 1) * POOL_GROUP)
        band = jnp.where(lag >= 0, jnp.where(lag < w, 1.0, 0.0), 0.0).astype(BF16)
        s = jnp.concatenate(
            [jnp.dot(band, x_ext[r:r + 2 * POOL_SUB, cols], preferred_element_type=F32)
             for r in range(0, t_blk, POOL_SUB)], axis=0)
        pooled = s / jnp.minimum(pos, float(w)) - x[:, cols]
        mixed.append(jnp.dot(pooled.astype(BF16), w_ref[g], preferred_element_type=F32))
    y = jnp.concatenate(mixed, axis=1) * scale_ref[...]
    y = y * lax.rsqrt(jnp.mean(y * y, axis=-1, keepdims=True) + EPS) * g_ref[...]
    o_ref[0] = (y * _silu(gate_ref[0].astype(F32))).astype(o_ref.dtype)


def _pool_branch(proj3, w_pool, pool_scale, g_pool, t_blk=1024):
    b, l, _ = proj3.shape
    assert max(POOL_WINDOWS) <= POOL_SUB and t_blk % POOL_SUB == 0
    sub_per_blk = t_blk // POOL_SUB
    kern = functools.partial(_pool_kernel, t_blk=t_blk)
    return pl.pallas_call(
        kern,
        grid=(b, l // t_blk),
        in_specs=[
            pl.BlockSpec((1, t_blk, D_POOL), lambda bi, i: (bi, i, COL_PX_1024)),
            pl.BlockSpec((1, POOL_SUB, D_POOL),
                         lambda bi, i: (bi, jnp.maximum(i * sub_per_blk - 1, 0), COL_PX_1024)),
            pl.BlockSpec((1, t_blk, D_POOL), lambda bi, i: (bi, i, COL_PGATE_1024)),
            pl.BlockSpec((len(POOL_WINDOWS), POOL_GROUP, POOL_GROUP), lambda bi, i: (0, 0, 0)),
            pl.BlockSpec((1, D_POOL), lambda bi, i: (0, 0)),
            pl.BlockSpec((1, D_POOL), lambda bi, i: (0, 0)),
        ],
        out_specs=pl.BlockSpec((1, t_blk, D_POOL), lambda bi, i: (bi, i, 0)),
        out_shape=jax.ShapeDtypeStruct((b, l, D_POOL), BF16),
        compiler_params=pltpu.CompilerParams(
            dimension_semantics=("parallel", "parallel"),
            vmem_limit_bytes=_vmem_limit(8 * t_blk * D_POOL * 4)),
        name="pool_branch",
    )(proj3, proj3, proj3, w_pool, pool_scale.reshape(1, D_POOL), g_pool.reshape(1, D_POOL))


def _attn_kernel(q_ref, proj_hbm_ref, gate_ref, g_ref, o_ref, k_ref, v_ref, kv_sem, y_ref, c_ref,
                 *, t_blk, n_before):
    n_head = N_HEADS
    bi = pl.program_id(0)
    i = pl.program_id(1)

    def kv_copy(which, blk):
        col0 = (COL_K_2048, COL_V_2048)[which] * D_ATTN
        rows = pl.ds(pl.multiple_of(blk * t_blk, t_blk), t_blk)
        return pltpu.make_async_copy(proj_hbm_ref.at[bi, rows, pl.ds(col0, D_ATTN)],
                                     (k_ref, v_ref)[which].at[rows, :], kv_sem.at[which, blk])

    @pl.when(i == 0)
    def _():
        for blk in range(k_ref.shape[0] // t_blk):
            kv_copy(0, blk).start()
            kv_copy(1, blk).start()

    kv_copy(0, i).wait()
    kv_copy(1, i).wait()

    row = lax.broadcasted_iota(jnp.int32, (t_blk, t_blk), 0)
    col = lax.broadcasted_iota(jnp.int32, (t_blk, t_blk), 1)
    later = jnp.where(row > col, 1.0, 0.0).astype(BF16)
    causal = col < row
    qs = [(q_ref[0, :, h * HEAD_DIM:(h + 1) * HEAD_DIM].astype(F32) * (HEAD_DIM ** -0.5 * LOG2_E)).astype(BF16)
          for h in range(n_head)]

    def tile(h, j, carry, diagonal):
        lanes = slice(h * HEAD_DIM, (h + 1) * HEAD_DIM)
        start = pl.multiple_of(j * t_blk, t_blk)
        kj = k_ref[pl.ds(start, t_blk), lanes]
        vj = v_ref[pl.ds(start, t_blk), lanes]
        z = lax.dot_general(qs[h], kj, (((1,), (1,)), ((), ())), preferred_element_type=F32)
        if diagonal:
            z = jnp.where(causal, z, MASKED_LOGIT)
        softplus = jnp.maximum(z, 0.0) + jnp.log(1.0 + jnp.exp2(-jnp.abs(z))) * LOG2_E
        suffix = jnp.dot(softplus.astype(BF16), later, preferred_element_type=F32)
        wts = jnp.exp2(z - softplus - suffix - carry)
        pv = jnp.dot(wts.astype(BF16), vj, preferred_element_type=F32)
        return pv, carry + jnp.sum(softplus, axis=1, keepdims=True)

    for h in range(n_head):
        acc, carry = tile(h, i, jnp.zeros((t_blk, 1), F32), True)
        for d in range(1, n_before + 1):
            carry = jnp.where(i >= d, carry, NO_BLOCK_CARRY)
            pv, carry = tile(h, jnp.maximum(i - d, 0), carry, False)
            acc = acc + pv
        y_ref[:, h * HEAD_DIM:(h + 1) * HEAD_DIM] = acc
        c_ref[:, h:h + 1] = carry

    def alive():
        return jnp.min(c_ref[:, 0:n_head]) < F32_EXP2_ZERO_ABOVE

    def cond(state):
        j, go = state
        return jnp.logical_and(j >= 0, go)

    def body(state):
        j, _ = state
        for h in range(n_head):
            lanes = slice(h * HEAD_DIM, (h + 1) * HEAD_DIM)
            pv, carry = tile(h, j, c_ref[:, h:h + 1], False)
            y_ref[:, lanes] = y_ref[:, lanes] + pv
            c_ref[:, h:h + 1] = carry
        return j - 1, alive()

    def finish():
        y = y_ref[...]
        y = y * lax.rsqrt(jnp.mean(y * y, axis=-1, keepdims=True) + EPS) * g_ref[...]
        o_ref[0] = (y * _silu(gate_ref[0].astype(F32))).astype(o_ref.dtype)

    finish()
    first = i - n_before - 1

    @pl.when(jnp.logical_and(first >= 0, alive()))
    def _():
        lax.while_loop(cond, body, (first, True))
        finish()


def _attn_branch(proj3, g_attn, t_blk=256, n_before=1):
    b, l, _ = proj3.shape
    kern = functools.partial(_attn_kernel, t_blk=t_blk, n_before=n_before)
    est = 2 * l * D_ATTN * 2 + 2 * 3 * t_blk * D_ATTN * 2 + N_HEADS * 12 * t_blk * t_blk * 4
    q_rows = lambda col: pl.BlockSpec((1, t_blk, D_ATTN), lambda bi, i: (bi, i, col))
    return pl.pallas_call(
        kern,
        grid=(b, l // t_blk),
        in_specs=[q_rows(COL_Q_2048), pl.BlockSpec(memory_space=pl.ANY), q_rows(COL_AGATE_2048),
                  pl.BlockSpec((1, D_ATTN), lambda bi, i: (0, 0))],
        out_specs=pl.BlockSpec((1, t_blk, D_ATTN), lambda bi, i: (bi, i, 0)),
        out_shape=jax.ShapeDtypeStruct((b, l, D_ATTN), BF16),
        scratch_shapes=[pltpu.VMEM((l, D_ATTN), BF16), pltpu.VMEM((l, D_ATTN), BF16),
                        pltpu.SemaphoreType.DMA((2, l // t_blk)),
                        pltpu.VMEM((t_blk, D_ATTN), F32), pltpu.VMEM((t_blk, LANES), F32)],
        compiler_params=pltpu.CompilerParams(
            dimension_semantics=("arbitrary", "arbitrary"),
            vmem_limit_bytes=_vmem_limit(est)),
        name="attn_branch",
    )(proj3, proj3, proj3, g_attn.reshape(1, D_ATTN))


N_SLAB = 4
SLAB_IN = D_SSM // N_SLAB
SLAB_STATES = 1024
GROUPS_PER_SLAB = N_SSM_GROUPS // N_SLAB
N_CHUNK = SLAB_STATES // LANES
TILE_PITCH = SUBLANES + 1


def _block_diag(ref, rows_per_group, cols_per_group):
    n_rows = ref.shape[1]
    n_cols = GROUPS_PER_SLAB * cols_per_group
    row_group = lax.broadcasted_iota(jnp.int32, (n_rows, n_cols), 0) // rows_per_group
    col_group = lax.broadcasted_iota(jnp.int32, (n_rows, n_cols), 1) // cols_per_group
    return jnp.where((row_group == col_group)[None], jnp.concatenate([ref[...]] * GROUPS_PER_SLAB, axis=-1), 0.0)


def _discretize_kernel(lam_re_ref, lam_im_ref, log_dt_ref, b_re_ref, b_im_ref, c_re_ref, c_im_ref,
                       a_re_ref, a_im_ref, wb_re_ref, wb_im_ref, wc_re_ref, wc_im_ref):
    lam_re = lam_re_ref[...]
    lam_im = lam_im_ref[...]
    dt = jnp.exp(log_dt_ref[...])
    mag = jnp.exp(lam_re * dt)
    a_re = mag * jnp.cos(lam_im * dt)
    a_im = mag * jnp.sin(lam_im * dt)
    a_re_ref[...] = a_re
    a_im_ref[...] = a_im
    num_re = a_re - 1.0
    den = lam_re * lam_re + lam_im * lam_im
    c_re = ((num_re * lam_re + a_im * lam_im) / den)[:, None, :]
    c_im = ((a_im * lam_re - num_re * lam_im) / den)[:, None, :]
    b_re = _block_diag(b_re_ref, SSM_GROUP, SSM_STATE)
    b_im = _block_diag(b_im_ref, SSM_GROUP, SSM_STATE)
    wb_re_ref[...] = (c_re * b_re - c_im * b_im).astype(wb_re_ref.dtype)
    wb_im_ref[...] = (c_re * b_im + c_im * b_re).astype(wb_im_ref.dtype)
    wc_re_ref[...] = _block_diag(c_re_ref, SSM_STATE, SSM_GROUP).astype(wc_re_ref.dtype)
    wc_im_ref[...] = _block_diag(c_im_ref, SSM_STATE, SSM_GROUP).astype(wc_im_ref.dtype)


def _discretize(lam_re, lam_im, log_dt, b_re, b_im, c_re, c_im):
    shape4 = (N_SLAB, SLAB_STATES)
    log_dt_full = jnp.broadcast_to(log_dt[:, None], (N_SSM_GROUPS, SSM_STATE)).reshape(shape4)
    b_rows = lambda b: jnp.swapaxes(b, 1, 2).reshape(N_SLAB, SLAB_IN, SSM_STATE)
    c_rows = lambda c: jnp.swapaxes(c, 1, 2).reshape(N_SLAB, SLAB_STATES, SSM_GROUP)
    out_shape = (jax.ShapeDtypeStruct(shape4, F32), jax.ShapeDtypeStruct(shape4, F32),
                 jax.ShapeDtypeStruct((N_SLAB, SLAB_IN, SLAB_STATES), BF16),
                 jax.ShapeDtypeStruct((N_SLAB, SLAB_IN, SLAB_STATES), BF16),
                 jax.ShapeDtypeStruct((N_SLAB, SLAB_STATES, SLAB_IN), BF16),
                 jax.ShapeDtypeStruct((N_SLAB, SLAB_STATES, SLAB_IN), BF16))
    return pl.pallas_call(
        _discretize_kernel, out_shape=out_shape, name="s5_discretize",
        compiler_params=pltpu.CompilerParams(vmem_limit_bytes=_vmem_limit(32 << 20)),
    )(lam_re.reshape(shape4), lam_im.reshape(shape4), log_dt_full,
      b_rows(b_re), b_rows(b_im), c_rows(c_re), c_rows(c_im))


def _s5_kernel(u_ref, gate_ref, a_re_ref, a_im_ref, wb_re_ref, wb_im_ref, wc_re_ref, wc_im_ref,
               dskip_ref, wglu_ref, bglu_ref, g_ref, o_ref,
               s_re_ref, s_im_ref, x_re_ref, x_im_ref, *, t_blk, n_batch):
    n_row = n_batch * N_SLAB
    assert n_row == SUBLANES

    @pl.when(pl.program_id(0) == 0)
    def _():
        x_re_ref[...] = jnp.zeros_like(x_re_ref)
        x_im_ref[...] = jnp.zeros_like(x_im_ref)

    for j in range(N_SLAB):
        cols = slice(j * SLAB_IN, (j + 1) * SLAB_IN)
        uj = jnp.concatenate([u_ref[b, :, cols] for b in range(n_batch)], axis=0)
        bu_re = jnp.dot(uj, wb_re_ref[j], preferred_element_type=F32)
        bu_im = jnp.dot(uj, wb_im_ref[j], preferred_element_type=F32)
        for b in range(n_batch):
            rows = slice(b * t_blk, (b + 1) * t_blk)
            k = b * N_SLAB + j
            for c in range(N_CHUNK):
                lanes = slice(c * LANES, (c + 1) * LANES)
                s_re_ref[c, pl.ds(k, t_blk, stride=TILE_PITCH), :] = bu_re[rows, lanes]
                s_im_ref[c, pl.ds(k, t_blk, stride=TILE_PITCH), :] = bu_im[rows, lanes]

    a_re = [a_re_ref[:, c * LANES:(c + 1) * LANES] for c in range(N_CHUNK)]
    a_im = [a_im_ref[:, c * LANES:(c + 1) * LANES] for c in range(N_CHUNK)]
    x_re0 = tuple(x_re_ref[:, c * LANES:(c + 1) * LANES] for c in range(N_CHUNK))
    x_im0 = tuple(x_im_ref[:, c * LANES:(c + 1) * LANES] for c in range(N_CHUNK))

    def step(t, state):
        x_re, x_im = state
        rows = pl.ds(t * TILE_PITCH, n_row)
        new_re, new_im = [], []
        for c in range(N_CHUNK):
            nr = a_re[c] * x_re[c] - a_im[c] * x_im[c] + s_re_ref[c, rows, :]
            ni = a_re[c] * x_im[c] + a_im[c] * x_re[c] + s_im_ref[c, rows, :]
            s_re_ref[c, rows, :] = nr
            s_im_ref[c, rows, :] = ni
            new_re.append(nr)
            new_im.append(ni)
        return tuple(new_re), tuple(new_im)

    x_re, x_im = lax.fori_loop(0, t_blk, step, (x_re0, x_im0), unroll=4)
    for c in range(N_CHUNK):
        x_re_ref[:, c * LANES:(c + 1) * LANES] = x_re[c]
        x_im_ref[:, c * LANES:(c + 1) * LANES] = x_im[c]

    def states(s_ref, j):
        return jnp.concatenate(
            [jnp.concatenate([s_ref[c, pl.ds(b * N_SLAB + j, t_blk, stride=TILE_PITCH), :] for c in range(N_CHUNK)],
                             axis=1) for b in range(n_batch)], axis=0).astype(BF16)

    ys = [jnp.dot(states(s_re_ref, j), wc_re_ref[j], preferred_element_type=F32)
          - jnp.dot(states(s_im_ref, j), wc_im_ref[j], preferred_element_type=F32) for j in range(N_SLAB)]
    u_all = jnp.concatenate([u_ref[b] for b in range(n_batch)], axis=0).astype(F32)
    y = jnp.concatenate(ys, axis=1) + dskip_ref[...] * u_all
    h = jax.nn.gelu(y)
    glu = jnp.dot(h.astype(BF16), wglu_ref[...], preferred_element_type=F32) + bglu_ref[...]
    out = glu[:, :D_SSM] * jax.nn.sigmoid(glu[:, D_SSM:])
    out = out * lax.rsqrt(jnp.mean(out * out, axis=-1, keepdims=True) + EPS) * g_ref[...]
    for b in range(n_batch):
        rows = slice(b * t_blk, (b + 1) * t_blk)
        o_ref[b] = (out[rows] * _silu(gate_ref[b].astype(F32))).astype(o_ref.dtype)


def _s5_branch(proj3, a_re, a_im, wb_re, wb_im, wc_re, wc_im, d_skip, w_glu, b_glu, g_ssm, t_blk=256):
    b, l, _ = proj3.shape
    n_row = b * N_SLAB
    a_re_t = jnp.tile(a_re, (b, 1))
    a_im_t = jnp.tile(a_im, (b, 1))
    kern = functools.partial(_s5_kernel, t_blk=t_blk, n_batch=b)
    weights = 4 * N_SLAB * SLAB_IN * SLAB_STATES * 2 + D_SSM * 2 * D_SSM * 2
    est = (weights + 2 * N_CHUNK * t_blk * TILE_PITCH * LANES * 4 + 2 * 3 * b * t_blk * D_SSM * 2
           + 6 * b * t_blk * 2 * D_SSM * 4)
    return pl.pallas_call(
        kern,
        grid=(l // t_blk,),
        in_specs=[
            pl.BlockSpec((b, t_blk, D_SSM), lambda i: (0, i, COL_SU_1024)),
            pl.BlockSpec((b, t_blk, D_SSM), lambda i: (0, i, COL_SGATE_1024)),
            _resident(n_row, SLAB_STATES), _resident(n_row, SLAB_STATES),
            _resident(N_SLAB, SLAB_IN, SLAB_STATES), _resident(N_SLAB, SLAB_IN, SLAB_STATES),
            _resident(N_SLAB, SLAB_STATES, SLAB_IN), _resident(N_SLAB, SLAB_STATES, SLAB_IN),
            _resident(1, D_SSM), _resident(D_SSM, 2 * D_SSM), _resident(1, 2 * D_SSM), _resident(1, D_SSM),
        ],
        out_specs=pl.BlockSpec((b, t_blk, D_SSM), lambda i: (0, i, 0)),
        out_shape=jax.ShapeDtypeStruct((b, l, D_SSM), BF16),
        scratch_shapes=[pltpu.VMEM((N_CHUNK, t_blk * TILE_PITCH, LANES), F32),
                        pltpu.VMEM((N_CHUNK, t_blk * TILE_PITCH, LANES), F32),
                        pltpu.VMEM((n_row, SLAB_STATES), F32),
                        pltpu.VMEM((n_row, SLAB_STATES), F32)],
        compiler_params=pltpu.CompilerParams(
            dimension_semantics=("arbitrary",),
            vmem_limit_bytes=_vmem_limit(est)),
        name="s5_branch",
    )(proj3, proj3, a_re_t, a_im_t, wb_re, wb_im, wc_re, wc_im,
      d_skip.reshape(1, D_SSM), w_glu, b_glu.reshape(1, 2 * D_SSM), g_ssm.reshape(1, D_SSM))


def _mixers(proj, bsz, w_pool, pool_scale, lam_re, lam_im, log_dt,
            b_re, b_im, c_re, c_im, d_skip, w_glu, b_glu, branch_g):
    m = proj.shape[0]
    proj3 = proj.reshape(bsz, m // bsz, D_IN)
    g_pool = branch_g[:D_POOL]
    g_attn = branch_g[D_POOL:D_POOL + D_ATTN]
    g_ssm = branch_g[D_POOL + D_ATTN:]
    y_pool = _pool_branch(proj3, w_pool.astype(BF16), pool_scale, g_pool)
    y_attn = _attn_branch(proj3, g_attn)
    a_re, a_im, wb_re, wb_im, wc_re, wc_im = _discretize(lam_re, lam_im, log_dt, b_re, b_im, c_re, c_im)
    y_ssm = _s5_branch(proj3, a_re, a_im, wb_re, wb_im, wc_re, wc_im, d_skip, w_glu.astype(BF16), b_glu, g_ssm)
    return y_pool.reshape(m, D_POOL), y_attn.reshape(m, D_ATTN), y_ssm.reshape(m, D_SSM)


def kernel(x, ln_g, w_in, w_pool, pool_scale, lam_re, lam_im, log_dt, b_re, b_im, c_re, c_im,
           d_skip, w_glu, b_glu, branch_g, w_out, final_g):
    bsz, l, d = x.shape
    assert ln_g.shape[0] == 2
    mixer_params = (w_pool, pool_scale, lam_re, lam_im, log_dt, b_re, b_im, c_re, c_im,
                    d_skip, w_glu, b_glu, branch_g)
    x0 = x.reshape(bsz * l, d)

    h = _rmsnorm(x0, ln_g[0], BF16)
    proj, w_in1 = _in_proj_f32w(h, w_in, 0)
    ys = _mixers(proj, bsz, *[p[0] for p in mixer_params])
    x1, xg, ssq = _out_proj_f32w(*ys, w_out, 0, x0, ln_g[1])

    proj, w_out1 = _in_proj_scaled(xg, ssq, w_in1, w_out, 1)
    ys = _mixers(proj, bsz, *[p[1] for p in mixer_params])
    return _out_proj_final(*ys, w_out1, x1, final_g).reshape(bsz, l, d)
```
